```python
import math
import jax, jax.numpy as jnp
from jax import lax
import numpy as np

D_MODEL = 1024
BATCH = 4
SEQ = 4096
DEPTH = 1
DEC_BATCH = 32
DEC_SEQ = 64
PAST_LEN = 4096

CHUNK = 64
EPS = 1e-6
POOL_WIDTH = D_MODEL // 2
POOL_WINDOWS = (2, 4, 8, 16)
POOL_GROUPS = len(POOL_WINDOWS)
POOL_GW = POOL_WIDTH // POOL_GROUPS
POOL_PAD = max(POOL_WINDOWS) - 1
HEAD_DIM = 64
N_HEADS = (D_MODEL // 2) // HEAD_DIM
N_KV_HEADS = 2
GQA_GROUP = N_HEADS // N_KV_HEADS
ATTN_WIDTH = N_HEADS * HEAD_DIM
KV_WIDTH = N_KV_HEADS * HEAD_DIM
WINDOW = 128
N_PREV_CHUNKS = WINDOW // CHUNK
NUM_BUCKETS = 32
REL_MAX_DIST = 128
IN_WIDTH = POOL_WIDTH + ATTN_WIDTH + 2 * KV_WIDTH + 2 * D_MODEL
IN_SPLITS = (POOL_WIDTH, POOL_WIDTH + ATTN_WIDTH, POOL_WIDTH + ATTN_WIDTH + KV_WIDTH,
             POOL_WIDTH + ATTN_WIDTH + 2 * KV_WIDTH, POOL_WIDTH + ATTN_WIDTH + 2 * KV_WIDTH + D_MODEL)
N_EXPERTS = 32
TOP_K = 4
N_GROUPS = 4
TOPK_GROUPS = 2
EXPERTS_PER_GROUP = N_EXPERTS // N_GROUPS
D_EXPERT = 256
D_SHARED = 256
ROUTED_SCALE = 2.5

kernel_name = 'hybrid_pool_swa_moe_stream_step'

F32 = jnp.float32


def rmsnorm(x, g):
    xf = x.astype(F32)
    y = xf * lax.rsqrt(jnp.mean(xf * xf, axis=-1, keepdims=True) + EPS)
    return (y * g.astype(F32)).astype(x.dtype)


def ada_mod(c, w_ada, b_ada):
    m = jax.nn.silu(c) @ w_ada + b_ada
    return jnp.split(m[:, None, :], 6, axis=-1)


def modulate(h, shift, scale):
    return h * (1 + scale) + shift


def t5_bucket(rel):
    nb = NUM_BUCKETS // 2
    max_exact = nb // 2
    ret = jnp.where(rel > 0, nb, 0)
    n = jnp.abs(rel)
    nf = jnp.maximum(n, 1).astype(F32)
    large = max_exact + (jnp.log(nf / max_exact) / math.log(REL_MAX_DIST / max_exact)
                         * (nb - max_exact)).astype(jnp.int32)
    large = jnp.minimum(large, nb - 1)
    return ret + jnp.where(n < max_exact, n, large)


def rel_bias(table, lq, lk):
    rel = jnp.arange(lk)[None, :] - WINDOW - jnp.arange(lq)[:, None]
    return jnp.transpose(table[t5_bucket(rel)], (2, 0, 1)).astype(F32)


def pool_branch(u_ext, pos, w_pool_group, pool_scale):
    B = u_ext.shape[0]
    L = pos.shape[0]
    uf = u_ext.astype(F32)
    cs = jnp.pad(jnp.cumsum(uf, axis=1), ((0, 0), (1, 0), (0, 0)))
    u = uf[:, POOL_PAD:]
    hi = cs[:, POOL_PAD + 1:POOL_PAD + 1 + L]
    outs = []
    for g, w in enumerate(POOL_WINDOWS):
        sl = slice(g * POOL_GW, (g + 1) * POOL_GW)
        lo = cs[:, POOL_PAD + 1 - w:POOL_PAD + 1 - w + L, sl]
        cnt = jnp.minimum(pos + 1, w).astype(F32)[None, :, None]
        outs.append((hi[..., sl] - lo) / cnt - u[..., sl])
    p = jnp.stack(outs, axis=2).astype(u_ext.dtype)
    y = jnp.einsum('blgc,gcd->blgd', p, w_pool_group).reshape(B, L, POOL_WIDTH)
    return y * pool_scale


def band_attention(q, k, v, valid, bias, sinks):
    B, N, Lq = q.shape[:3]
    Lk = k.shape[2]
    qg = q.reshape(B, N, Lq, N_KV_HEADS, GQA_GROUP, HEAD_DIM)
    s = jnp.einsum('bnqkgd,bnskd->bnkgqs', qg, k).astype(F32) * (HEAD_DIM ** -0.5)
    s = s + bias.reshape(N_KV_HEADS, GQA_GROUP, Lq, Lk)
    s = jnp.where(valid[None, :, None, None], s, -1e30)
    sink = jnp.broadcast_to(sinks.astype(F32).reshape(N_KV_HEADS, GQA_GROUP, 1, 1), s.shape[:-1] + (1,))
    p = jax.nn.softmax(jnp.concatenate([s, sink], axis=-1), axis=-1)[..., :-1]
    o = jnp.einsum('bnkgqs,bnskd->bnqkgd', p.astype(v.dtype), v)
    return o.reshape(B, N * Lq, ATTN_WIDTH)


def split_in(h, w_in):
    return jnp.split(h @ w_in, IN_SPLITS, axis=-1)


def merge_branches(pool_y, attn_y, gp, ga, w_pool_proj, w_attn_proj, w_out):
    m = jax.nn.sigmoid(gp) * (pool_y @ w_pool_proj) + jax.nn.sigmoid(ga) * (attn_y @ w_attn_proj)
    return m @ w_out


def mixer_prompt(h, w_in, w_pool_group, pool_scale, attn_sinks, w_pool_proj, w_attn_proj, w_out, rel_table):
    B, S, _ = h.shape
    n_chunks = S // CHUNK
    u, q, k, v, gp, ga = split_in(h, w_in)
    u_ext = jnp.pad(u, ((0, 0), (POOL_PAD, 0), (0, 0)))
    pool_y = pool_branch(u_ext, jnp.arange(S), w_pool_group, pool_scale)
    k = k.reshape(B, S, N_KV_HEADS, HEAD_DIM)
    v = v.reshape(B, S, N_KV_HEADS, HEAD_DIM)

    def band(t):
        tp = jnp.pad(t, ((0, 0), (WINDOW, 0), (0, 0), (0, 0))).reshape(
            B, n_chunks + N_PREV_CHUNKS, CHUNK, N_KV_HEADS, HEAD_DIM)
        return jnp.concatenate([tp[:, j:j + n_chunks] for j in range(N_PREV_CHUNKS + 1)], axis=2)

    lk = WINDOW + CHUNK
    valid = (jnp.arange(n_chunks)[:, None, None] * CHUNK + jnp.arange(lk)[None, None, :]) >= WINDOW
    qb = q.reshape(B, n_chunks, CHUNK, N_HEADS, HEAD_DIM)
    attn_y = band_attention(qb, band(k), band(v), valid, rel_bias(rel_table, CHUNK, lk), attn_sinks)
    y = merge_branches(pool_y, attn_y, gp, ga, w_pool_proj, w_attn_proj, w_out)
    return y, k[:, -WINDOW:], v[:, -WINDOW:], u[:, -POOL_PAD:]


def mixer_sample(h, cache_k, cache_v, state_pool, w_in, w_pool_group, pool_scale, attn_sinks,
                 w_pool_proj, w_attn_proj, w_out, rel_table):
    DB, DS, _ = h.shape
    u, q, k, v, gp, ga = split_in(h, w_in)
    u_ext = jnp.concatenate([state_pool.astype(u.dtype), u], axis=1)
    pool_y = pool_branch(u_ext, PAST_LEN + jnp.arange(DS), w_pool_group, pool_scale)
    k_all = jnp.concatenate([cache_k.astype(k.dtype), k.reshape(DB, DS, N_KV_HEADS, HEAD_DIM)], axis=1)
    v_all = jnp.concatenate([cache_v.astype(v.dtype), v.reshape(DB, DS, N_KV_HEADS, HEAD_DIM)], axis=1)
    lk = WINDOW + DS
    valid = jnp.ones((1, 1, lk), dtype=bool)
    qb = q.reshape(DB, 1, DS, N_HEADS, HEAD_DIM)
    attn_y = band_attention(qb, k_all[:, None], v_all[:, None], valid, rel_bias(rel_table, DS, lk), attn_sinks)
    y = merge_branches(pool_y, attn_y, gp, ga, w_pool_proj, w_attn_proj, w_out)
    return y, k_all[:, -WINDOW:], v_all[:, -WINDOW:], u_ext[:, -POOL_PAD:]


def swiglu(h, wg, wu, wd):
    return (jax.nn.silu(h @ wg) * (h @ wu)) @ wd


def moe(h, w_router, router_bias, w_exp_gate, w_exp_up, w_exp_down, w_sh_gate, w_sh_up, w_sh_down):
    T = h.shape[0]
    scores = jax.nn.sigmoid((h @ w_router).astype(F32))
    sel = scores + router_bias.astype(F32)
    grp_score = jnp.sum(lax.top_k(sel.reshape(T, N_GROUPS, EXPERTS_PER_GROUP), 2)[0], axis=-1)
    _, gidx = lax.top_k(grp_score, TOPK_GROUPS)
    gmask = jnp.any(gidx[..., None] == jnp.arange(N_GROUPS), axis=1)
    sel = jnp.where(jnp.repeat(gmask, EXPERTS_PER_GROUP, axis=1), sel, -jnp.inf)
    _, eidx = lax.top_k(sel, TOP_K)
    w = jnp.take_along_axis(scores, eidx, axis=1)
    w = w / jnp.sum(w, axis=-1, keepdims=True) * ROUTED_SCALE
    gates = jnp.sum((eidx[..., None] == jnp.arange(N_EXPERTS)).astype(F32) * w[..., None], axis=1)
    gates = gates.astype(h.dtype)
    y = swiglu(h, w_sh_gate, w_sh_up, w_sh_down)
    for e in range(N_EXPERTS):
        y = y + gates[:, e:e + 1] * swiglu(h, w_exp_gate[e], w_exp_up[e], w_exp_down[e])
    return y


def setup_inputs(seed: int = 0) -> dict:
    key = jax.random.key(seed)
    ks = jax.random.split(key, 32)
    D = D_MODEL
    nrm = jax.random.normal
    return {
        'x_prompt': nrm(ks[0], (BATCH, SEQ, D), F32),
        'x_sample': nrm(ks[1], (DEC_BATCH, DEC_SEQ, D), F32),
        'c_prompt': nrm(ks[2], (BATCH, D), F32),
        'c_sample': nrm(ks[3], (DEC_BATCH, D), F32),
        'cache_k': nrm(ks[4], (DEPTH, DEC_BATCH, WINDOW, N_KV_HEADS, HEAD_DIM), F32),
        'cache_v': nrm(ks[5], (DEPTH, DEC_BATCH, WINDOW, N_KV_HEADS, HEAD_DIM), F32),
        'state_pool': nrm(ks[6], (DEPTH, DEC_BATCH, POOL_PAD, POOL_WIDTH), F32),
        'w_ada': nrm(ks[7], (DEPTH, D, 6 * D), F32) * (0.5 * D ** -0.5),
        'b_ada': nrm(ks[8], (DEPTH, 6 * D), F32) * 0.02,
        'g_norm_mix': 1.0 + 0.05 * nrm(ks[9], (DEPTH, D), F32),
        'g_norm_ffn': 1.0 + 0.05 * nrm(ks[10], (DEPTH, D), F32),
        'w_in': nrm(ks[11], (DEPTH, D, IN_WIDTH), F32) * D ** -0.5,
        'w_pool_group': nrm(ks[12], (DEPTH, POOL_GROUPS, POOL_GW, POOL_GW), F32) * POOL_GW ** -0.5,
        'pool_scale': 1.0 + 0.1 * nrm(ks[13], (DEPTH, POOL_WIDTH), F32),
        'attn_sinks': nrm(ks[14], (DEPTH, N_HEADS), F32),
        'w_pool_proj': nrm(ks[15], (DEPTH, POOL_WIDTH, D), F32) * POOL_WIDTH ** -0.5,
        'w_attn_proj': nrm(ks[16], (DEPTH, ATTN_WIDTH, D), F32) * ATTN_WIDTH ** -0.5,
        'w_out': nrm(ks[17], (DEPTH, D, D), F32) * D ** -0.5,
        'rel_table': 0.5 * nrm(ks[18], (NUM_BUCKETS, N_HEADS), F32),
        'w_router': nrm(ks[19], (DEPTH, D, N_EXPERTS), F32) * D ** -0.5,
        'router_bias': 0.01 * nrm(ks[20], (DEPTH, N_EXPERTS), F32),
        'w_exp_gate': nrm(ks[21], (DEPTH, N_EXPERTS, D, D_EXPERT), F32) * D ** -0.5,
        'w_exp_up': nrm(ks[22], (DEPTH, N_EXPERTS, D, D_EXPERT), F32) * D ** -0.5,
        'w_exp_down': nrm(ks[23], (DEPTH, N_EXPERTS, D_EXPERT, D), F32) * D_EXPERT ** -0.5,
        'w_sh_gate': nrm(ks[24], (DEPTH, D, D_SHARED), F32) * D ** -0.5,
        'w_sh_up': nrm(ks[25], (DEPTH, D, D_SHARED), F32) * D ** -0.5,
        'w_sh_down': nrm(ks[26], (DEPTH, D_SHARED, D), F32) * D_SHARED ** -0.5,
        'g_final': 1.0 + 0.05 * nrm(ks[27], (D,), F32),
    }


def reference(x_prompt, x_sample, c_prompt, c_sample, cache_k, cache_v, state_pool,
              w_ada, b_ada, g_norm_mix, g_norm_ffn, w_in, w_pool_group, pool_scale, attn_sinks,
              w_pool_proj, w_attn_proj, w_out, rel_table, w_router, router_bias,
              w_exp_gate, w_exp_up, w_exp_down, w_sh_gate, w_sh_up, w_sh_down, g_final):
    xp, xs = x_prompt, x_sample
    B, S, D = xp.shape
    DB, DS, _ = xs.shape
    nk_p, nv_p, npool_p, nk_s, nv_s, npool_s = [], [], [], [], [], []
    for l in range(DEPTH):
        mp = ada_mod(c_prompt, w_ada[l], b_ada[l])
        ms = ada_mod(c_sample, w_ada[l], b_ada[l])
        hp = modulate(rmsnorm(xp, g_norm_mix[l]), mp[0], mp[1])
        hs = modulate(rmsnorm(xs, g_norm_mix[l]), ms[0], ms[1])
        yp, kp, vp, up = mixer_prompt(hp, w_in[l], w_pool_group[l], pool_scale[l], attn_sinks[l],
                                      w_pool_proj[l], w_attn_proj[l], w_out[l], rel_table)
        ys, kk, vv, us = mixer_sample(hs, cache_k[l], cache_v[l], state_pool[l], w_in[l], w_pool_group[l],
                                      pool_scale[l], attn_sinks[l], w_pool_proj[l], w_attn_proj[l],
                                      w_out[l], rel_table)
        xp = xp + mp[2] * yp
        xs = xs + ms[2] * ys
        nk_p.append(kp); nv_p.append(vp); npool_p.append(up)
        nk_s.append(kk); nv_s.append(vv); npool_s.append(us)
        hp = modulate(rmsnorm(xp, g_norm_ffn[l]), mp[3], mp[4])
        hs = modulate(rmsnorm(xs, g_norm_ffn[l]), ms[3], ms[4])
        h_all = jnp.concatenate([hp.reshape(B * S, D), hs.reshape(DB * DS, D)], axis=0)
        f = moe(h_all, w_router[l], router_bias[l], w_exp_gate[l], w_exp_up[l], w_exp_down[l],
                w_sh_gate[l], w_sh_up[l], w_sh_down[l])
        xp = xp + mp[5] * f[:B * S].reshape(B, S, D)
        xs = xs + ms[5] * f[B * S:].reshape(DB, DS, D)
    y_prompt = rmsnorm(xp, g_final)
    y_sample = rmsnorm(xs, g_final)
    return (y_prompt, y_sample, jnp.stack(nk_p), jnp.stack(nv_p), jnp.stack(npool_p),
            jnp.stack(nk_s), jnp.stack(nv_s), jnp.stack(npool_s))
```

```python
import functools
import math

import jax
import jax.numpy as jnp
from jax import lax
from jax.experimental import pallas as pl
from jax.experimental.pallas import tpu as pltpu

F32 = jnp.float32
BF16 = jnp.bfloat16

D_MODEL = 1024
CHUNK = 64
EPS = 1e-6
POOL_WIDTH = 512
POOL_WINDOWS = (2, 4, 8, 16)
POOL_GW = 128
POOL_PAD = 15
HEAD_DIM = 64
N_HEADS = 8
N_KV_HEADS = 2
ATTN_WIDTH = 512
KV_WIDTH = 128
WINDOW = 128
BAND = WINDOW + CHUNK
NUM_BUCKETS = 32
REL_MAX_DIST = 128
PAST_LEN = 4096
N_EXPERTS = 32
TOP_K = 4
N_GROUPS = 4
TOPK_GROUPS = 2
EXPERTS_PER_GROUP = 8
D_EXPERT = 256
ROUTED_SCALE = 2.5

OFF_U = 0
OFF_Q = OFF_U + POOL_WIDTH
OFF_K = OFF_Q + ATTN_WIDTH
OFF_V = OFF_K + KV_WIDTH
OFF_GP = OFF_V + KV_WIDTH
OFF_GA = OFF_GP + D_MODEL
IN_WIDTH = OFF_GA + D_MODEL

HIST = 16
MIX_ROWS = 512
MOE_ROWS = 1024
MERGE_PANEL = 256
VMEM_LIMIT = 56 * 1024 * 1024


def _dot(a, b):
    return jnp.dot(a, b, preferred_element_type=F32)


def _dot_nt(a, b):
    return lax.dot_general(a, b, (((1,), (1,)), ((), ())), preferred_element_type=F32)


def _norm_mod(x, g, shift, scale):
    ms = jnp.mean(x * x, axis=-1, keepdims=True)
    y = x * lax.rsqrt(ms + EPS) * g
    return y * (1.0 + scale) + shift


def _silu(x):
    return x * jax.nn.sigmoid(x)


def _ada_kernel(c_ref, w_ref, b_ref, o_ref):
    a = _silu(c_ref[...]).astype(BF16)
    o_ref[...] = _dot(a, w_ref[...].astype(BF16)) + b_ref[...]


def _ada(c_all, w_ada, b_ada):
    rows = c_all.shape[0]
    n = w_ada.shape[1]
    tn = 768
    return pl.pallas_call(
        _ada_kernel,
        grid=(n // tn,),
        in_specs=[
            pl.BlockSpec((rows, D_MODEL), lambda j: (0, 0)),
            pl.BlockSpec((D_MODEL, tn), lambda j: (0, j)),
            pl.BlockSpec((1, tn), lambda j: (0, j)),
        ],
        out_specs=pl.BlockSpec((rows, tn), lambda j: (0, j)),
        out_shape=jax.ShapeDtypeStruct((rows, n), F32),
        compiler_params=pltpu.CompilerParams(dimension_semantics=("arbitrary",)),
        name="ada",
    )(c_all, w_ada, b_ada.reshape(1, n))


def _pool_feats(ue, pos0, rows):
    s2 = ue + pltpu.roll(ue, 1, 0)
    s4 = s2 + pltpu.roll(s2, 2, 0)
    s8 = s4 + pltpu.roll(s4, 4, 0)
    s16 = s8 + pltpu.roll(s8, 8, 0)
    pos = pos0 + lax.broadcasted_iota(jnp.int32, (rows, POOL_GW), 0)
    outs = []
    for g, (w, sw) in enumerate(zip(POOL_WINDOWS, (s2, s4, s8, s16))):
        sl = slice(g * POOL_GW, (g + 1) * POOL_GW)
        cnt = jnp.minimum(pos + 1, w).astype(F32)
        outs.append(sw[HIST:, sl] / cnt - ue[HIST:, sl])
    return jnp.concatenate(outs, axis=1)


def _kv_variants(t):
    lane = lax.broadcasted_iota(jnp.int32, t.shape, 1)
    low = lane < HEAD_DIM
    swapped = pltpu.roll(t, HEAD_DIM, 1)
    zero = jnp.zeros_like(t)
    return (jnp.where(low, t, zero).astype(BF16),
            jnp.where(low, zero, swapped).astype(BF16),
            jnp.where(low, swapped, zero).astype(BF16),
            jnp.where(low, zero, t).astype(BF16))


def _attend_chunk(qc, kvar, vvar, bias_ref, sink_ref, valid):
    outs = []
    for g in range(N_KV_HEADS):
        qs = jnp.concatenate([qc[:, (2 * g) * 128:(2 * g + 1) * 128],
                              qc[:, (2 * g + 1) * 128:(2 * g + 2) * 128]], axis=0)
        o = None
        for par in range(2):
            s = _dot_nt(qs, kvar[2 * g + par]) + bias_ref[g, par]
            if valid is not None:
                s = jnp.where(valid, s, -1e30)
            sink = sink_ref[g, par][:, 0:1]
            m = jnp.maximum(jnp.max(s, axis=-1, keepdims=True), sink)
            e = jnp.exp(s - m)
            denom = jnp.sum(e, axis=-1, keepdims=True) + jnp.exp(sink - m)
            p = (e / denom).astype(BF16)
            pv = _dot(p, vvar[2 * g + par])
            o = pv if o is None else o + pv
        outs.append(o[0:CHUNK])
        outs.append(o[CHUNK:2 * CHUNK])
    return jnp.concatenate(outs, axis=1)


def _merge_out(h, pool_y, attn_y, win_ref, wpp_ref, wap_ref, wout_ref, m_scr):
    rows = h.shape[0]
    pool_b = pool_y.astype(BF16)
    attn_b = attn_y.astype(BF16)
    for n in range(D_MODEL // MERGE_PANEL):
        lo, hi = n * MERGE_PANEL, (n + 1) * MERGE_PANEL
        gp = _dot(h, win_ref[:, OFF_GP + lo:OFF_GP + hi])
        ga = _dot(h, win_ref[:, OFF_GA + lo:OFF_GA + hi])
        pp = _dot(pool_b, wpp_ref[:, lo:hi])
        ap = _dot(attn_b, wap_ref[:, lo:hi])
        m_scr[0:rows, lo:hi] = (jax.nn.sigmoid(gp) * pp + jax.nn.sigmoid(ga) * ap).astype(BF16)
    return _dot(m_scr[0:rows, :], wout_ref[...])


def _mixer_prompt_kernel(x_ref, mod_ref, g_ref, win_ref, wpg_ref, ps_ref, bias_ref, sink_ref,
                         wpp_ref, wap_ref, wout_ref,
                         x1_ref, nk_ref, nv_ref, npool_ref,
                         uext, k0, k1, k2, k3, v0, v1, v2, v3, m_scr):
    ts = MIX_ROWS
    s = pl.program_id(1)
    kext = (k0, k1, k2, k3)
    vext = (v0, v1, v2, v3)

    @pl.when(s == 0)
    def _():
        uext[0:HIST, :] = jnp.zeros((HIST, POOL_WIDTH), F32)
        for r in kext + vext:
            r[0:WINDOW, :] = jnp.zeros((WINDOW, KV_WIDTH), BF16)

    x = x_ref[0]
    shift = mod_ref[0, :, 0:D_MODEL]
    scale = mod_ref[0, :, D_MODEL:2 * D_MODEL]
    gate = mod_ref[0, :, 2 * D_MODEL:3 * D_MODEL]
    h = _norm_mod(x, g_ref[...], shift, scale).astype(BF16)

    u = _dot(h, win_ref[:, OFF_U:OFF_U + POOL_WIDTH])
    q = (_dot(h, win_ref[:, OFF_Q:OFF_Q + ATTN_WIDTH]) * (HEAD_DIM ** -0.5)).astype(BF16)
    k = _dot(h, win_ref[:, OFF_K:OFF_K + KV_WIDTH])
    v = _dot(h, win_ref[:, OFF_V:OFF_V + KV_WIDTH])
    nk_ref[0] = k[ts - WINDOW:ts, :]
    nv_ref[0] = v[ts - WINDOW:ts, :]
    npool_ref[0] = u[ts - HIST:ts, :]

    pos0 = s * ts
    uext[HIST:HIST + ts, :] = u
    p = _pool_feats(uext[...], pos0, ts)
    uext[0:HIST, :] = u[ts - HIST:ts, :]
    pool_y = _dot(p.astype(BF16), wpg_ref[...]) * ps_ref[...]

    for r, t in zip(kext, _kv_variants(k)):
        r[WINDOW:WINDOW + ts, :] = t
    for r, t in zip(vext, _kv_variants(v)):
        r[WINDOW:WINDOW + ts, :] = t
    key_j = lax.broadcasted_iota(jnp.int32, (1, BAND), 1)
    chunks = []
    for c in range(ts // CHUNK):
        lo = c * CHUNK
        valid = (pos0 + lo + key_j) >= WINDOW
        kvar = [r[lo:lo + BAND, :] for r in kext]
        vvar = [r[lo:lo + BAND, :] for r in vext]
        chunks.append(_attend_chunk(q[lo:lo + CHUNK, :], kvar, vvar, bias_ref, sink_ref, valid))
    attn_y = jnp.concatenate(chunks, axis=0)
    for r in kext + vext:
        r[0:WINDOW, :] = r[ts:ts + WINDOW, :]

    y = _merge_out(h, pool_y, attn_y, win_ref, wpp_ref, wap_ref, wout_ref, m_scr)
    x1_ref[0] = x + gate * y


def _const_spec(shape):
    nd = len(shape)
    return pl.BlockSpec(shape, lambda *_: (0,) * nd)


def _mixer_prompt(x, mod, g_mix, w_in, w_pg, pool_scale, bias4, sink4, w_pp, w_ap, w_out):
    b, s_len, d = x.shape
    ts = MIX_ROWS
    kv_scratch = [pltpu.VMEM((WINDOW + ts, KV_WIDTH), BF16) for _ in range(8)]
    return pl.pallas_call(
        _mixer_prompt_kernel,
        grid=(b, s_len // ts),
        in_specs=[
            pl.BlockSpec((1, ts, d), lambda i, j: (i, j, 0)),
            pl.BlockSpec((1, 1, 6 * d), lambda i, j: (i, 0, 0)),
            _const_spec((1, d)),
            _const_spec((d, IN_WIDTH)),
            _const_spec((POOL_WIDTH, POOL_WIDTH)),
            _const_spec((1, POOL_WIDTH)),
            _const_spec(bias4.shape),
            _const_spec(sink4.shape),
            _const_spec((POOL_WIDTH, d)),
            _const_spec((ATTN_WIDTH, d)),
            _const_spec((d, d)),
        ],
        out_specs=[
            pl.BlockSpec((1, ts, d), lambda i, j: (i, j, 0)),
            pl.BlockSpec((1, WINDOW, KV_WIDTH), lambda i, j: (i, 0, 0)),
            pl.BlockSpec((1, WINDOW, KV_WIDTH), lambda i, j: (i, 0, 0)),
            pl.BlockSpec((1, HIST, POOL_WIDTH), lambda i, j: (i, 0, 0)),
        ],
        out_shape=[
            jax.ShapeDtypeStruct((b, s_len, d), F32),
            jax.ShapeDtypeStruct((b, WINDOW, KV_WIDTH), F32),
            jax.ShapeDtypeStruct((b, WINDOW, KV_WIDTH), F32),
            jax.ShapeDtypeStruct((b, HIST, POOL_WIDTH), F32),
        ],
        scratch_shapes=[pltpu.VMEM((HIST + ts, POOL_WIDTH), F32)] + kv_scratch
                       + [pltpu.VMEM((ts, d), BF16)],
        compiler_params=pltpu.CompilerParams(
            dimension_semantics=("arbitrary", "arbitrary"), vmem_limit_bytes=VMEM_LIMIT),
        name="mixer_prompt",
    )(x, mod, g_mix, w_in, w_pg, pool_scale, bias4, sink4, w_pp, w_ap, w_out)


def _mixer_sample_kernel(x_ref, mod_ref, ck_ref, cv_ref, sp_ref, g_ref, win_ref, wpg_ref, ps_ref,
                         bias_ref, sink_ref, wpp_ref, wap_ref, wout_ref,
                         x1_ref, nk_ref, nv_ref, npool_ref, m_scr):
    nseq, ds, d = x_ref.shape
    rows = nseq * ds
    x = x_ref[...]
    shift = mod_ref[:, :, 0:D_MODEL]
    scale = mod_ref[:, :, D_MODEL:2 * D_MODEL]
    gate = mod_ref[:, :, 2 * D_MODEL:3 * D_MODEL]
    h = _norm_mod(x, g_ref[...], shift, scale).reshape(rows, d).astype(BF16)

    u = _dot(h, win_ref[:, OFF_U:OFF_U + POOL_WIDTH])
    q = (_dot(h, win_ref[:, OFF_Q:OFF_Q + ATTN_WIDTH]) * (HEAD_DIM ** -0.5)).astype(BF16)
    k = _dot(h, win_ref[:, OFF_K:OFF_K + KV_WIDTH])
    v = _dot(h, win_ref[:, OFF_V:OFF_V + KV_WIDTH])

    pool_parts = []
    attn_parts = []
    for b in range(nseq):
        lo = b * ds
        ub = u[lo:lo + ds, :]
        kb = k[lo:lo + ds, :]
        vb = v[lo:lo + ds, :]
        ck = ck_ref[b]
        cv = cv_ref[b]
        nk_ref[b] = jnp.concatenate([ck[ds:WINDOW, :], kb], axis=0)
        nv_ref[b] = jnp.concatenate([cv[ds:WINDOW, :], vb], axis=0)
        npool_ref[b] = ub[ds - HIST:ds, :]
        ue = jnp.concatenate([sp_ref[b], ub], axis=0)
        pool_parts.append(_pool_feats(ue, PAST_LEN, ds))
        kvar = _kv_variants(jnp.concatenate([ck, kb], axis=0))
        vvar = _kv_variants(jnp.concatenate([cv, vb], axis=0))
        attn_parts.append(_attend_chunk(q[lo:lo + ds, :], kvar, vvar, bias_ref, sink_ref, None))
    p = jnp.concatenate(pool_parts, axis=0)
    attn_y = jnp.concatenate(attn_parts, axis=0)
    pool_y = _dot(p.astype(BF16), wpg_ref[...]) * ps_ref[...]

    y = _merge_out(h, pool_y, attn_y, win_ref, wpp_ref, wap_ref, wout_ref, m_scr)
    x1_ref[...] = x + gate * y.reshape(nseq, ds, d)


def _mixer_sample(x, mod, ck, cv, sp16, g_mix, w_in, w_pg, pool_scale, bias4, sink4, w_pp, w_ap, w_out):
    db, ds, d = x.shape
    nseq = MIX_ROWS // ds
    seq_spec = lambda shape: pl.BlockSpec((nseq,) + shape, lambda i: (i, 0, 0))
    return pl.pallas_call(
        _mixer_sample_kernel,
        grid=(db // nseq,),
        in_specs=[
            seq_spec((ds, d)),
            seq_spec((1, 6 * d)),
            seq_spec((WINDOW, KV_WIDTH)),
            seq_spec((WINDOW, KV_WIDTH)),
            seq_spec((HIST, POOL_WIDTH)),
            _const_spec((1, d)),
            _const_spec((d, IN_WIDTH)),
            _const_spec((POOL_WIDTH, POOL_WIDTH)),
            _const_spec((1, POOL_WIDTH)),
            _const_spec(bias4.shape),
            _const_spec(sink4.shape),
            _const_spec((POOL_WIDTH, d)),
            _const_spec((ATTN_WIDTH, d)),
            _const_spec((d, d)),
        ],
        out_specs=[
            seq_spec((ds, d)),
            seq_spec((WINDOW, KV_WIDTH)),
            seq_spec((WINDOW, KV_WIDTH)),
            seq_spec((HIST, POOL_WIDTH)),
        ],
        out_shape=[
            jax.ShapeDtypeStruct((db, ds, d), F32),
            jax.ShapeDtypeStruct((db, WINDOW, KV_WIDTH), F32),
            jax.ShapeDtypeStruct((db, WINDOW, KV_WIDTH), F32),
            jax.ShapeDtypeStruct((db, HIST, POOL_WIDTH), F32),
        ],
        scratch_shapes=[pltpu.VMEM((MIX_ROWS, d), BF16)],
        compiler_params=pltpu.CompilerParams(
            dimension_semantics=("arbitrary",), vmem_limit_bytes=VMEM_LIMIT),
        name="mixer_sample",
    )(x, mod, ck, cv, sp16, g_mix, w_in, w_pg, pool_scale, bias4, sink4, w_pp, w_ap, w_out)


def _first_argmax(vals, idx, size):
    m = jnp.max(vals, axis=0, keepdims=True)
    first = jnp.min(jnp.where(vals == m, idx, size), axis=0, keepdims=True)
    return m, first


def _router_gates(h, wr_ref, rb_ref):
    t = h.shape[0]
    logits = _dot_nt(wr_ref[...], h)
    scores = jax.nn.sigmoid(logits)
    sel = scores + rb_ref[...]
    neg = jnp.float32(-jnp.inf)
    eidx = lax.broadcasted_iota(jnp.int32, (N_EXPERTS, t), 0)

    gscores = []
    bidx = lax.broadcasted_iota(jnp.int32, (EXPERTS_PER_GROUP, t), 0)
    for g in range(N_GROUPS):
        blk = sel[g * EXPERTS_PER_GROUP:(g + 1) * EXPERTS_PER_GROUP, :]
        m1, i1 = _first_argmax(blk, bidx, EXPERTS_PER_GROUP)
        m2 = jnp.max(jnp.where(bidx == i1, neg, blk), axis=0, keepdims=True)
        gscores.append(m1 + m2)
    gs = jnp.concatenate(gscores, axis=0)
    gidx = lax.broadcasted_iota(jnp.int32, (N_GROUPS, t), 0)
    _, g1 = _first_argmax(gs, gidx, N_GROUPS)
    _, g2 = _first_argmax(jnp.where(gidx == g1, neg, gs), gidx, N_GROUPS)
    egrp = eidx // EXPERTS_PER_GROUP
    cand = jnp.where((egrp == g1) | (egrp == g2), sel, neg)

    picked = jnp.zeros((N_EXPERTS, t), jnp.bool_)
    for _ in range(TOP_K):
        _, i = _first_argmax(cand, eidx, N_EXPERTS)
        hit = eidx == i
        picked = picked | hit
        cand = jnp.where(hit, neg, cand)
    w = jnp.where(picked, scores, 0.0)
    return w / jnp.sum(w, axis=0, keepdims=True) * ROUTED_SCALE


def _swiglu_rows(h, wg, wu, wd, rowscale=None):
    act = _silu(_dot(h, wg)) * _dot(h, wu)
    if rowscale is not None:
        act = act * rowscale
    return _dot(act.astype(BF16), wd)


def _moe_dense_kernel(x_ref, mod_ref, g_ref, gf_ref, wr_ref, rb_ref, wsg_ref, wsu_ref, wsd_ref,
                      weg_ref, weu_ref, wed_ref, o_ref, h_scr, gate_scr, acc_scr):
    ng, r, d = x_ref.shape
    rows = ng * r
    e = pl.program_id(1)

    @pl.when(e == 0)
    def _():
        shift = mod_ref[:, :, 3 * D_MODEL:4 * D_MODEL]
        scale = mod_ref[:, :, 4 * D_MODEL:5 * D_MODEL]
        h = _norm_mod(x_ref[...], g_ref[...], shift, scale).reshape(rows, d).astype(BF16)
        h_scr[...] = h
        gates = _router_gates(h, wr_ref, rb_ref)
        pad = jnp.zeros((128 - N_EXPERTS, rows), F32)
        gate_scr[...] = jnp.concatenate([gates, pad], axis=0).T
        acc_scr[...] = _swiglu_rows(h, wsg_ref[...], wsu_ref[...], wsd_ref[...])

    h = h_scr[...]
    lane = lax.broadcasted_iota(jnp.int32, (rows, 128), 1)
    col = jnp.sum(jnp.where(lane == e, gate_scr[...], 0.0), axis=1, keepdims=True)
    acc_scr[...] += _swiglu_rows(h, weg_ref[0], weu_ref[0], wed_ref[0], col)

    @pl.when(e == pl.num_programs(1) - 1)
    def _():
        gate = mod_ref[:, :, 5 * D_MODEL:6 * D_MODEL]
        xo = x_ref[...] + gate * acc_scr[...].reshape(ng, r, d)
        ms = jnp.mean(xo * xo, axis=-1, keepdims=True)
        o_ref[...] = xo * lax.rsqrt(ms + EPS) * gf_ref[...]


def _moe_dense(x1, mod, g_ffn, g_final, w_r_t, rb, wsg, wsu, wsd, weg, weu, wed):
    nb, r_all, d = x1.shape
    if r_all >= MOE_ROWS:
        ng, r = 1, MOE_ROWS
        grid0 = nb * (r_all // MOE_ROWS)
        per = r_all // MOE_ROWS
        x_map = lambda i, e: (i // per, i % per, 0)
        m_map = lambda i, e: (i // per, 0, 0)
    else:
        ng, r = MOE_ROWS // r_all, r_all
        grid0 = nb // ng
        x_map = lambda i, e: (i, 0, 0)
        m_map = lambda i, e: (i, 0, 0)
    rows = ng * r
    cst = lambda shape: pl.BlockSpec(shape, lambda i, e: (0,) * len(shape))
    return pl.pallas_call(
        _moe_dense_kernel,
        grid=(grid0, N_EXPERTS),
        in_specs=[
            pl.BlockSpec((ng, r, d), x_map),
            pl.BlockSpec((ng, 1, 6 * d), m_map),
            cst((1, d)), cst((1, d)),
            cst((N_EXPERTS, d)), cst((N_EXPERTS, 1)),
            cst((d, D_EXPERT)), cst((d, D_EXPERT)), cst((D_EXPERT, d)),
            pl.BlockSpec((1, d, D_EXPERT), lambda i, e: (e, 0, 0)),
            pl.BlockSpec((1, d, D_EXPERT), lambda i, e: (e, 0, 0)),
            pl.BlockSpec((1, D_EXPERT, d), lambda i, e: (e, 0, 0)),
        ],
        out_specs=pl.BlockSpec((ng, r, d), x_map),
        out_shape=jax.ShapeDtypeStruct(x1.shape, F32),
        scratch_shapes=[pltpu.VMEM((rows, d), BF16), pltpu.VMEM((rows, 128), F32),
                        pltpu.VMEM((rows, d), F32)],
        compiler_params=pltpu.CompilerParams(
            dimension_semantics=("arbitrary", "arbitrary"), vmem_limit_bytes=VMEM_LIMIT),
        name="moe_dense",
    )(x1, mod, g_ffn, g_final, w_r_t, rb, wsg, wsu, wsd, weg, weu, wed)


def _t5_bucket(rel):
    nb = NUM_BUCKETS // 2
    max_exact = nb // 2
    ret = jnp.where(rel > 0, nb, 0)
    n = jnp.abs(rel)
    nf = jnp.maximum(n, 1).astype(F32)
    large = max_exact + (jnp.log(nf / max_exact) / math.log(REL_MAX_DIST / max_exact)
                         * (nb - max_exact)).astype(jnp.int32)
    large = jnp.minimum(large, nb - 1)
    return ret + jnp.where(n < max_exact, n, large)


def _pair_layout(per_head):
    groups = []
    for g in range(N_KV_HEADS):
        pars = []
        for par in range(2):
            pars.append(jnp.concatenate([per_head[4 * g + par], per_head[4 * g + par + 2]], axis=0))
        groups.append(jnp.stack(pars))
    return jnp.stack(groups)


def kernel(x_prompt, x_sample, c_prompt, c_sample, cache_k, cache_v, state_pool, w_ada, b_ada, g_norm_mix, g_norm_ffn, w_in, w_pool_group, pool_scale, attn_sinks, w_pool_proj, w_attn_proj, w_out, rel_table, w_router, router_bias, w_exp_gate, w_exp_up, w_exp_down, w_sh_gate, w_sh_up, w_sh_down, g_final):
    b, s_len, d = x_prompt.shape
    db, ds, _ = x_sample.shape
    l = 0

    c_all = jnp.concatenate([c_prompt, c_sample, jnp.zeros((4, d), F32)], axis=0)
    mods = _ada(c_all, w_ada[l], b_ada[l])
    mod_p = mods[0:b].reshape(b, 1, 6 * d)
    mod_s = mods[b:b + db].reshape(db, 1, 6 * d)

    rel = jnp.arange(BAND)[None, :] - WINDOW - jnp.arange(CHUNK)[:, None]
    bias = jnp.transpose(rel_table[_t5_bucket(rel)], (2, 0, 1)).astype(F32)
    bias4 = _pair_layout(bias)
    sink4 = _pair_layout(jnp.broadcast_to(attn_sinks[l].astype(F32)[:, None, None], (N_HEADS, CHUNK, 128)))
    w_pg = jnp.zeros((POOL_WIDTH, POOL_WIDTH), F32)
    for g in range(len(POOL_WINDOWS)):
        w_pg = w_pg.at[g * POOL_GW:(g + 1) * POOL_GW, g * POOL_GW:(g + 1) * POOL_GW].set(w_pool_group[l, g])
    w_pg = w_pg.astype(BF16)
    w_in_b = w_in[l].astype(BF16)
    w_pp = w_pool_proj[l].astype(BF16)
    w_ap = w_attn_proj[l].astype(BF16)
    w_out_b = w_out[l].astype(BF16)
    g_mix = g_norm_mix[l].reshape(1, d)
    g_ffn = g_norm_ffn[l].reshape(1, d)
    ps = pool_scale[l].reshape(1, POOL_WIDTH)

    x1p, nk_p, nv_p, np_p = _mixer_prompt(x_prompt, mod_p, g_mix, w_in_b, w_pg, ps, bias4, sink4,
                                          w_pp, w_ap, w_out_b)
    ck = cache_k[l].reshape(db, WINDOW, KV_WIDTH)
    cv = cache_v[l].reshape(db, WINDOW, KV_WIDTH)
    sp16 = jnp.pad(state_pool[l], ((0, 0), (HIST - POOL_PAD, 0), (0, 0)))
    x1s, nk_s, nv_s, np_s = _mixer_sample(x_sample, mod_s, ck, cv, sp16, g_mix, w_in_b, w_pg, ps,
                                          bias4, sink4, w_pp, w_ap, w_out_b)

    moe_args = (g_ffn, g_final.reshape(1, d), w_router[l].T.astype(BF16),
                router_bias[l].astype(F32).reshape(N_EXPERTS, 1),
                w_sh_gate[l].astype(BF16), w_sh_up[l].astype(BF16), w_sh_down[l].astype(BF16),
                w_exp_gate[l].astype(BF16), w_exp_up[l].astype(BF16), w_exp_down[l].astype(BF16))
    y_p = _moe_dense(x1p, mod_p, *moe_args)
    y_s = _moe_dense(x1s, mod_s, *moe_args)

    kv_shape = lambda n: (1, n, WINDOW, N_KV_HEADS, HEAD_DIM)
    return (y_p, y_s,
            nk_p.reshape(kv_shape(b)), nv_p.reshape(kv_shape(b)),
            np_p[:, HIST - POOL_PAD:, :][None],
            nk_s.reshape(kv_shape(db)), nv_s.reshape(kv_shape(db)),
            np_s[:, HIST - POOL_PAD:, :][None])
```

```python
import functools
import math

import jax
import jax.numpy as jnp
from jax import lax
from jax.experimental import pallas as pl
from jax.experimental.pallas import tpu as pltpu

F32 = jnp.float32
BF16 = jnp.bfloat16

D_MODEL = 1024
CHUNK = 64
EPS = 1e-6
POOL_WIDTH = 512
POOL_WINDOWS = (2, 4, 8, 16)
POOL_GW = 128
POOL_PAD = 15
HEAD_DIM = 64
N_HEADS = 8
N_KV_HEADS = 2
ATTN_WIDTH = 512
KV_WIDTH = 128
WINDOW = 128
BAND = WINDOW + CHUNK
NUM_BUCKETS = 32
REL_MAX_DIST = 128
PAST_LEN = 4096
N_EXPERTS = 32
TOP_K = 4
N_GROUPS = 4
TOPK_GROUPS = 2
EXPERTS_PER_GROUP = 8
D_EXPERT = 256
ROUTED_SCALE = 2.5

OFF_U = 0
OFF_Q = OFF_U + POOL_WIDTH
OFF_K = OFF_Q + ATTN_WIDTH
OFF_V = OFF_K + KV_WIDTH
OFF_GP = OFF_V + KV_WIDTH
OFF_GA = OFF_GP + D_MODEL
IN_WIDTH = OFF_GA + D_MODEL

HIST = 16
MIX_ROWS = 512
MERGE_PANEL = 256
VMEM_LIMIT = 56 * 1024 * 1024

GROUP_ROWS = 64
TC = 1024
PAGE = 16
PAGE_SHIFT = 4
TILE_M = 512
PAGES_PER_TILE = TILE_M // PAGE
TILE_SHIFT = 5
LOCAL_ROWS = TOP_K * TC + N_EXPERTS * PAGE
LANE = 128
PLANES = D_MODEL // LANE
PLANE_PAD = 8
HP_STRIDE = TC + PLANE_PAD
LOC_STRIDE = LOCAL_ROWS + PLANE_PAD
COMBINE_PHASES = 4
PHASE_ROWS = TC // COMBINE_PHASES
Y_STRIDE = PHASE_ROWS + PLANE_PAD


def _dot(a, b):
    return jnp.dot(a, b, preferred_element_type=F32)


def _dot_nt(a, b):
    return lax.dot_general(a, b, (((1,), (1,)), ((), ())), preferred_element_type=F32)


def _norm_mod(x, g, shift, scale):
    ms = jnp.mean(x * x, axis=-1, keepdims=True)
    y = x * lax.rsqrt(ms + EPS) * g
    return y * (1.0 + scale) + shift


def _silu(x):
    return x * jax.nn.sigmoid(x)


def _ada_kernel(c_ref, w_ref, b_ref, o_ref):
    a = _silu(c_ref[...]).astype(BF16)
    o_ref[...] = _dot(a, w_ref[...].astype(BF16)) + b_ref[...]


def _ada(c_all, w_ada, b_ada):
    rows = c_all.shape[0]
    n = w_ada.shape[1]
    tn = 768
    return pl.pallas_call(
        _ada_kernel,
        grid=(n // tn,),
        in_specs=[
            pl.BlockSpec((rows, D_MODEL), lambda j: (0, 0)),
            pl.BlockSpec((D_MODEL, tn), lambda j: (0, j)),
            pl.BlockSpec((1, tn), lambda j: (0, j)),
        ],
        out_specs=pl.BlockSpec((rows, tn), lambda j: (0, j)),
        out_shape=jax.ShapeDtypeStruct((rows, n), F32),
        compiler_params=pltpu.CompilerParams(dimension_semantics=("arbitrary",)),
        name="ada",
    )(c_all, w_ada, b_ada.reshape(1, n))


def _pool_feats(ue, pos0, rows):
    s2 = ue + pltpu.roll(ue, 1, 0)
    s4 = s2 + pltpu.roll(s2, 2, 0)
    s8 = s4 + pltpu.roll(s4, 4, 0)
    s16 = s8 + pltpu.roll(s8, 8, 0)
    pos = pos0 + lax.broadcasted_iota(jnp.int32, (rows, POOL_GW), 0)
    outs = []
    for g, (w, sw) in enumerate(zip(POOL_WINDOWS, (s2, s4, s8, s16))):
        sl = slice(g * POOL_GW, (g + 1) * POOL_GW)
        cnt = jnp.minimum(pos + 1, w).astype(F32)
        outs.append(sw[HIST:, sl] / cnt - ue[HIST:, sl])
    return jnp.concatenate(outs, axis=1)


def _kv_variants(t):
    lane = lax.broadcasted_iota(jnp.int32, t.shape, 1)
    low = lane < HEAD_DIM
    swapped = pltpu.roll(t, HEAD_DIM, 1)
    zero = jnp.zeros_like(t)
    return (jnp.where(low, t, zero).astype(BF16),
            jnp.where(low, zero, swapped).astype(BF16),
            jnp.where(low, swapped, zero).astype(BF16),
            jnp.where(low, zero, t).astype(BF16))


def _attend_chunk(qc, kvar, vvar, bias_ref, sink_ref, valid):
    outs = []
    for g in range(N_KV_HEADS):
        qs = jnp.concatenate([qc[:, (2 * g) * 128:(2 * g + 1) * 128],
                              qc[:, (2 * g + 1) * 128:(2 * g + 2) * 128]], axis=0)
        o = None
        for par in range(2):
            s = _dot_nt(qs, kvar[2 * g + par]) + bias_ref[g, par]
            if valid is not None:
                s = jnp.where(valid, s, -1e30)
            sink = sink_ref[g, par][:, 0:1]
            m = jnp.maximum(jnp.max(s, axis=-1, keepdims=True), sink)
            e = jnp.exp(s - m)
            denom = jnp.sum(e, axis=-1, keepdims=True) + jnp.exp(sink - m)
            p = (e / denom).astype(BF16)
            pv = _dot(p, vvar[2 * g + par])
            o = pv if o is None else o + pv
        outs.append(o[0:CHUNK])
        outs.append(o[CHUNK:2 * CHUNK])
    return jnp.concatenate(outs, axis=1)


def _merge_out(h, pool_y, attn_y, win_ref, wpp_ref, wap_ref, wout_ref, m_scr):
    rows = h.shape[0]
    pool_b = pool_y.astype(BF16)
    attn_b = attn_y.astype(BF16)
    for n in range(D_MODEL // MERGE_PANEL):
        lo, hi = n * MERGE_PANEL, (n + 1) * MERGE_PANEL
        gp = _dot(h, win_ref[:, OFF_GP + lo:OFF_GP + hi])
        ga = _dot(h, win_ref[:, OFF_GA + lo:OFF_GA + hi])
        pp = _dot(pool_b, wpp_ref[:, lo:hi])
        ap = _dot(attn_b, wap_ref[:, lo:hi])
        m_scr[0:rows, lo:hi] = (jax.nn.sigmoid(gp) * pp + jax.nn.sigmoid(ga) * ap).astype(BF16)
    return _dot(m_scr[0:rows, :], wout_ref[...])


def _mixer_prompt_kernel(x_ref, mod_ref, g_ref, win_ref, wpg_ref, ps_ref, bias_ref, sink_ref,
                         wpp_ref, wap_ref, wout_ref,
                         x1_ref, nk_ref, nv_ref, npool_ref,
                         uext, k0, k1, k2, k3, v0, v1, v2, v3, m_scr):
    ts = MIX_ROWS
    s = pl.program_id(1)
    kext = (k0, k1, k2, k3)
    vext = (v0, v1, v2, v3)

    @pl.when(s == 0)
    def _():
        uext[0:HIST, :] = jnp.zeros((HIST, POOL_WIDTH), F32)
        for r in kext + vext:
            r[0:WINDOW, :] = jnp.zeros((WINDOW, KV_WIDTH), BF16)

    x = x_ref[0]
    shift = mod_ref[0, :, 0:D_MODEL]
    scale = mod_ref[0, :, D_MODEL:2 * D_MODEL]
    gate = mod_ref[0, :, 2 * D_MODEL:3 * D_MODEL]
    h = _norm_mod(x, g_ref[...], shift, scale).astype(BF16)

    u = _dot(h, win_ref[:, OFF_U:OFF_U + POOL_WIDTH])
    q = (_dot(h, win_ref[:, OFF_Q:OFF_Q + ATTN_WIDTH]) * (HEAD_DIM ** -0.5)).astype(BF16)
    k = _dot(h, win_ref[:, OFF_K:OFF_K + KV_WIDTH])
    v = _dot(h, win_ref[:, OFF_V:OFF_V + KV_WIDTH])
    nk_ref[0] = k[ts - WINDOW:ts, :]
    nv_ref[0] = v[ts - WINDOW:ts, :]
    npool_ref[0] = u[ts - HIST:ts, :]

    pos0 = s * ts
    uext[HIST:HIST + ts, :] = u
    p = _pool_feats(uext[...], pos0, ts)
    uext[0:HIST, :] = u[ts - HIST:ts, :]
    pool_y = _dot(p.astype(BF16), wpg_ref[...]) * ps_ref[...]

    for r, t in zip(kext, _kv_variants(k)):
        r[WINDOW:WINDOW + ts, :] = t
    for r, t in zip(vext, _kv_variants(v)):
        r[WINDOW:WINDOW + ts, :] = t
    key_j = lax.broadcasted_iota(jnp.int32, (1, BAND), 1)
    chunks = []
    for c in range(ts // CHUNK):
        lo = c * CHUNK
        valid = (pos0 + lo + key_j) >= WINDOW
        kvar = [r[lo:lo + BAND, :] for r in kext]
        vvar = [r[lo:lo + BAND, :] for r in vext]
        chunks.append(_attend_chunk(q[lo:lo + CHUNK, :], kvar, vvar, bias_ref, sink_ref, valid))
    attn_y = jnp.concatenate(chunks, axis=0)
    for r in kext + vext:
        r[0:WINDOW, :] = r[ts:ts + WINDOW, :]

    y = _merge_out(h, pool_y, attn_y, win_ref, wpp_ref, wap_ref, wout_ref, m_scr)
    x1_ref[0] = x + gate * y


def _const_spec(shape):
    nd = len(shape)
    return pl.BlockSpec(shape, lambda *_: (0,) * nd)


def _mixer_prompt(x, mod, g_mix, w_in, w_pg, pool_scale, bias4, sink4, w_pp, w_ap, w_out):
    b, s_len, d = x.shape
    ts = MIX_ROWS
    kv_scratch = [pltpu.VMEM((WINDOW + ts, KV_WIDTH), BF16) for _ in range(8)]
    return pl.pallas_call(
        _mixer_prompt_kernel,
        grid=(b, s_len // ts),
        in_specs=[
            pl.BlockSpec((1, ts, d), lambda i, j: (i, j, 0)),
            pl.BlockSpec((1, 1, 6 * d), lambda i, j: (i, 0, 0)),
            _const_spec((1, d)),
            _const_spec((d, IN_WIDTH)),
            _const_spec((POOL_WIDTH, POOL_WIDTH)),
            _const_spec((1, POOL_WIDTH)),
            _const_spec(bias4.shape),
            _const_spec(sink4.shape),
            _const_spec((POOL_WIDTH, d)),
            _const_spec((ATTN_WIDTH, d)),
            _const_spec((d, d)),
        ],
        out_specs=[
            pl.BlockSpec((1, ts, d), lambda i, j: (i, j, 0)),
            pl.BlockSpec((1, WINDOW, KV_WIDTH), lambda i, j: (i, 0, 0)),
            pl.BlockSpec((1, WINDOW, KV_WIDTH), lambda i, j: (i, 0, 0)),
            pl.BlockSpec((1, HIST, POOL_WIDTH), lambda i, j: (i, 0, 0)),
        ],
        out_shape=[
            jax.ShapeDtypeStruct((b, s_len, d), F32),
            jax.ShapeDtypeStruct((b, WINDOW, KV_WIDTH), F32),
            jax.ShapeDtypeStruct((b, WINDOW, KV_WIDTH), F32),
            jax.ShapeDtypeStruct((b, HIST, POOL_WIDTH), F32),
        ],
        scratch_shapes=[pltpu.VMEM((HIST + ts, POOL_WIDTH), F32)] + kv_scratch
                       + [pltpu.VMEM((ts, d), BF16)],
        compiler_params=pltpu.CompilerParams(
            dimension_semantics=("arbitrary", "arbitrary"), vmem_limit_bytes=VMEM_LIMIT),
        name="mixer_prompt",
    )(x, mod, g_mix, w_in, w_pg, pool_scale, bias4, sink4, w_pp, w_ap, w_out)


def _mixer_sample_kernel(x_ref, mod_ref, ck_ref, cv_ref, sp_ref, g_ref, win_ref, wpg_ref, ps_ref,
                         bias_ref, sink_ref, wpp_ref, wap_ref, wout_ref,
                         x1_ref, nk_ref, nv_ref, npool_ref, m_scr):
    nseq, ds, d = x_ref.shape
    rows = nseq * ds
    x = x_ref[...]
    shift = mod_ref[:, :, 0:D_MODEL]
    scale = mod_ref[:, :, D_MODEL:2 * D_MODEL]
    gate = mod_ref[:, :, 2 * D_MODEL:3 * D_MODEL]
    h = _norm_mod(x, g_ref[...], shift, scale).reshape(rows, d).astype(BF16)

    u = _dot(h, win_ref[:, OFF_U:OFF_U + POOL_WIDTH])
    q = (_dot(h, win_ref[:, OFF_Q:OFF_Q + ATTN_WIDTH]) * (HEAD_DIM ** -0.5)).astype(BF16)
    k = _dot(h, win_ref[:, OFF_K:OFF_K + KV_WIDTH])
    v = _dot(h, win_ref[:, OFF_V:OFF_V + KV_WIDTH])

    pool_parts = []
    attn_parts = []
    for b in range(nseq):
        lo = b * ds
        ub = u[lo:lo + ds, :]
        kb = k[lo:lo + ds, :]
        vb = v[lo:lo + ds, :]
        ck = ck_ref[b]
        cv = cv_ref[b]
        nk_ref[b] = jnp.concatenate([ck[ds:WINDOW, :], kb], axis=0)
        nv_ref[b] = jnp.concatenate([cv[ds:WINDOW, :], vb], axis=0)
        npool_ref[b] = ub[ds - HIST:ds, :]
        ue = jnp.concatenate([sp_ref[b], ub], axis=0)
        pool_parts.append(_pool_feats(ue, PAST_LEN, ds))
        kvar = _kv_variants(jnp.concatenate([ck, kb], axis=0))
        vvar = _kv_variants(jnp.concatenate([cv, vb], axis=0))
        attn_parts.append(_attend_chunk(q[lo:lo + ds, :], kvar, vvar, bias_ref, sink_ref, None))
    p = jnp.concatenate(pool_parts, axis=0)
    attn_y = jnp.concatenate(attn_parts, axis=0)
    pool_y = _dot(p.astype(BF16), wpg_ref[...]) * ps_ref[...]

    y = _merge_out(h, pool_y, attn_y, win_ref, wpp_ref, wap_ref, wout_ref, m_scr)
    x1_ref[...] = x + gate * y.reshape(nseq, ds, d)


def _mixer_sample(x, mod, ck, cv, sp16, g_mix, w_in, w_pg, pool_scale, bias4, sink4, w_pp, w_ap, w_out):
    db, ds, d = x.shape
    nseq = MIX_ROWS // ds
    seq_spec = lambda shape: pl.BlockSpec((nseq,) + shape, lambda i: (i, 0, 0))
    return pl.pallas_call(
        _mixer_sample_kernel,
        grid=(db // nseq,),
        in_specs=[
            seq_spec((ds, d)),
            seq_spec((1, 6 * d)),
            seq_spec((WINDOW, KV_WIDTH)),
            seq_spec((WINDOW, KV_WIDTH)),
            seq_spec((HIST, POOL_WIDTH)),
            _const_spec((1, d)),
            _const_spec((d, IN_WIDTH)),
            _const_spec((POOL_WIDTH, POOL_WIDTH)),
            _const_spec((1, POOL_WIDTH)),
            _const_spec(bias4.shape),
            _const_spec(sink4.shape),
            _const_spec((POOL_WIDTH, d)),
            _const_spec((ATTN_WIDTH, d)),
            _const_spec((d, d)),
        ],
        out_specs=[
            seq_spec((ds, d)),
            seq_spec((WINDOW, KV_WIDTH)),
            seq_spec((WINDOW, KV_WIDTH)),
            seq_spec((HIST, POOL_WIDTH)),
        ],
        out_shape=[
            jax.ShapeDtypeStruct((db, ds, d), F32),
            jax.ShapeDtypeStruct((db, WINDOW, KV_WIDTH), F32),
            jax.ShapeDtypeStruct((db, WINDOW, KV_WIDTH), F32),
            jax.ShapeDtypeStruct((db, HIST, POOL_WIDTH), F32),
        ],
        scratch_shapes=[pltpu.VMEM((MIX_ROWS, d), BF16)],
        compiler_params=pltpu.CompilerParams(
            dimension_semantics=("arbitrary",), vmem_limit_bytes=VMEM_LIMIT),
        name="mixer_sample",
    )(x, mod, ck, cv, sp16, g_mix, w_in, w_pg, pool_scale, bias4, sink4, w_pp, w_ap, w_out)


def _first_argmax(vals, idx, size):
    m = jnp.max(vals, axis=0, keepdims=True)
    first = jnp.min(jnp.where(vals == m, idx, size), axis=0, keepdims=True)
    return m, first


def _router_picks(h, wr_ref, rb_ref):
    t = h.shape[0]
    logits = _dot_nt(wr_ref[...], h)
    scores = jax.nn.sigmoid(logits)
    sel = scores + rb_ref[...]
    neg = jnp.float32(-jnp.inf)
    eidx = lax.broadcasted_iota(jnp.int32, (N_EXPERTS, t), 0)

    gscores = []
    bidx = lax.broadcasted_iota(jnp.int32, (EXPERTS_PER_GROUP, t), 0)
    for g in range(N_GROUPS):
        blk = sel[g * EXPERTS_PER_GROUP:(g + 1) * EXPERTS_PER_GROUP, :]
        m1, i1 = _first_argmax(blk, bidx, EXPERTS_PER_GROUP)
        m2 = jnp.max(jnp.where(bidx == i1, neg, blk), axis=0, keepdims=True)
        gscores.append(m1 + m2)
    gs = jnp.concatenate(gscores, axis=0)
    gidx = lax.broadcasted_iota(jnp.int32, (N_GROUPS, t), 0)
    _, g1 = _first_argmax(gs, gidx, N_GROUPS)
    _, g2 = _first_argmax(jnp.where(gidx == g1, neg, gs), gidx, N_GROUPS)
    egrp = eidx // EXPERTS_PER_GROUP
    cand = jnp.where((egrp == g1) | (egrp == g2), sel, neg)

    picks = []
    for _ in range(TOP_K):
        _, i = _first_argmax(cand, eidx, N_EXPERTS)
        hit = eidx == i
        picks.append(hit)
        cand = jnp.where(hit, neg, cand)
    return picks, scores


def _swiglu_rows(h, wg, wu, wd):
    act = _silu(_dot(h, wg)) * _dot(h, wu)
    return _dot(act.astype(BF16), wd)


def _route_kernel(xp_ref, xs_ref, modp_ref, mods_ref, g_ref, wr_ref, rb_ref, tri_ref,
                  h_ref, pos_ref, w_ref, cnt_ref, *, n_prompt_chunks):
    is_prompt = pl.program_id(0) < n_prompt_chunks
    x = jnp.where(is_prompt, xp_ref[...], xs_ref[...])
    shift = jnp.where(is_prompt, modp_ref[:, :, 0:D_MODEL], mods_ref[:, :, 0:D_MODEL])
    scale = jnp.where(is_prompt, modp_ref[:, :, D_MODEL:2 * D_MODEL], mods_ref[:, :, D_MODEL:2 * D_MODEL])
    h = _norm_mod(x, g_ref[...], shift, scale).reshape(TC, D_MODEL).astype(BF16)
    h_ref[...] = h
    picks, scores = _router_picks(h, wr_ref, rb_ref)
    picked = picks[0] | picks[1] | picks[2] | picks[3]
    cnt = _dot(jnp.where(picked, 1.0, 0.0).astype(BF16), tri_ref[...])
    n_e = jnp.max(cnt, axis=1, keepdims=True).astype(jnp.int32)
    n_b = jnp.broadcast_to(n_e, (N_EXPERTS, LANE))
    cnt_ref[0] = n_b
    padded = lax.shift_left(lax.shift_right_logical(n_b + (PAGE - 1), PAGE_SHIFT), PAGE_SHIFT)
    row = lax.broadcasted_iota(jnp.int32, (N_EXPERTS, LANE), 0)
    scan = padded
    for s in (1, 2, 4, 8, 16):
        scan = scan + jnp.where(row >= s, pltpu.roll(scan, s, 0), 0)
    base = (scan - padded)[:, 0:1].astype(F32)
    slot = base + cnt - 1.0
    pos = [jnp.sum(jnp.where(p, slot, 0.0), axis=0, keepdims=True) for p in picks]
    wk = [jnp.sum(jnp.where(p, scores, 0.0), axis=0, keepdims=True) for p in picks]
    wsum = wk[0] + wk[1] + wk[2] + wk[3]
    pos_ref[0] = jnp.concatenate(pos, axis=0).astype(jnp.int32)
    w_ref[0] = jnp.concatenate(wk, axis=0) / wsum * ROUTED_SCALE


def _plan_kernel(cnt_ref, te_ref, nt_ref, ebase_ref, cbase_ref, etp_ref, *, n_chunks, nt_max):
    def per_expert(e, tile0):
        def per_chunk(c, acc):
            cbase_ref[c * N_EXPERTS + e] = acc
            return acc + lax.shift_right_logical(cnt_ref[c, e] + (PAGE - 1), PAGE_SHIFT)
        tp = lax.fori_loop(0, n_chunks, per_chunk, 0)
        etp_ref[e] = tp
        ebase_ref[e] = tile0 * PAGES_PER_TILE
        nt = lax.shift_right_logical(tp + (PAGES_PER_TILE - 1), TILE_SHIFT)

        def fill(j, carry):
            te_ref[tile0 + j] = e
            return carry
        lax.fori_loop(0, nt, fill, 0)
        return tile0 + nt
    total = lax.fori_loop(0, N_EXPERTS, per_expert, 0)
    nt_ref[0] = total

    def tail(j, carry):
        te_ref[j] = N_EXPERTS - 1
        return carry
    lax.fori_loop(total, nt_max, tail, 0)


def _page_copy(src, src_page, dst, dst_page, sem):
    s = src.at[pl.ds(pl.multiple_of(src_page * PAGE, PAGE), PAGE), :]
    t = dst.at[pl.ds(pl.multiple_of(dst_page * PAGE, PAGE), PAGE), :]
    return pltpu.make_async_copy(s, t, sem)


def _for_each_run(c, cnt_ref, ebase_ref, cbase_ref, fn):
    def per_expert(e, lpage):
        n = cnt_ref[c, e]
        npages = lax.shift_right_logical(n + (PAGE - 1), PAGE_SHIFT)
        fn(n, lpage, ebase_ref[e] + cbase_ref[c * N_EXPERTS + e], npages)
        return lpage + npages
    return lax.fori_loop(0, N_EXPERTS, per_expert, 0)


def _row_slab(ref, row, stride):
    return ref.at[pl.ds(row, PLANES, stride=stride), :]


def _dispatch_kernel(cnt_ref, ebase_ref, cbase_ref, etp_ref, nt_ref, pos_ref, h_ref, xs_hbm,
                     hpl, loc, locb, zpage, ztile, sem, tsem, *, n_chunks, nt_max):
    c = pl.program_id(0)
    h = h_ref[...]
    for ch in range(PLANES):
        hpl[ch * HP_STRIDE:ch * HP_STRIDE + TC, :] = h[:, ch * LANE:(ch + 1) * LANE].astype(F32)
    zpage[...] = jnp.zeros(zpage.shape, BF16)
    ztile[...] = jnp.zeros(ztile.shape, BF16)
    zero_slab = jnp.zeros((PLANES, LANE), F32)

    def zero_pad_rows(n, lpage, gpage, npages):
        def per_row(r, carry):
            _row_slab(loc, r, LOC_STRIDE)[...] = zero_slab
            return carry
        lax.fori_loop(lpage * PAGE + n, (lpage + npages) * PAGE, per_row, 0)
    _for_each_run(c, cnt_ref, ebase_ref, cbase_ref, zero_pad_rows)

    def scatter(t8, carry):
        for u in range(8):
            t = t8 * 8 + u
            slab = _row_slab(hpl, t, HP_STRIDE)[...]
            for k in range(TOP_K):
                _row_slab(loc, pos_ref[0, 0, k * TC + t], LOC_STRIDE)[...] = slab
        return carry
    lax.fori_loop(0, TC // 8, scatter, 0)

    for ch in range(PLANES):
        locb[:, ch * LANE:(ch + 1) * LANE] = loc[ch * LOC_STRIDE:ch * LOC_STRIDE + LOCAL_ROWS, :].astype(BF16)

    def send_run(n, lpage, gpage, npages):
        def per_page(j, carry):
            _page_copy(locb, lpage + j, xs_hbm, gpage + j, sem).start()
            return carry
        lax.fori_loop(0, npages, per_page, 0)
    n_run = _for_each_run(c, cnt_ref, ebase_ref, cbase_ref, send_run)

    last = c == n_chunks - 1

    def pad_expert(e, n):
        tp = etp_ref[e]
        full = lax.shift_left(lax.shift_right_logical(tp + (PAGES_PER_TILE - 1), TILE_SHIFT), TILE_SHIFT)
        hi = jnp.where(last, full, tp)

        def per_page(j, carry):
            _page_copy(zpage, 0, xs_hbm, ebase_ref[e] + j, sem).start()
            return carry
        lax.fori_loop(tp, hi, per_page, 0)
        return n + hi - tp
    n_pad = lax.fori_loop(0, N_EXPERTS, pad_expert, 0)

    def tile_copy(i):
        dst = xs_hbm.at[pl.ds(pl.multiple_of(i * TILE_M, TILE_M), TILE_M), :]
        return pltpu.make_async_copy(ztile, dst, tsem)
    used = nt_ref[0]
    tail_hi = jnp.where(last, nt_max, used)

    def tail_start(i, carry):
        tile_copy(i).start()
        return carry
    lax.fori_loop(used, tail_hi, tail_start, 0)

    def drain(j, carry):
        _page_copy(zpage, 0, xs_hbm, 0, sem).wait()
        return carry
    lax.fori_loop(0, n_run + n_pad, drain, 0)

    def tail_wait(i, carry):
        tile_copy(i).wait()
        return carry
    lax.fori_loop(used, tail_hi, tail_wait, 0)


def _expert_kernel(te_ref, nt_ref, xs_ref, wg_ref, wu_ref, wd_ref, o_ref, wgb, wub, wdb):
    i = pl.program_id(0)

    @pl.when(i < nt_ref[0])
    def _():
        @pl.when((i == 0) | (te_ref[i] != te_ref[jnp.maximum(i - 1, 0)]))
        def _():
            wgb[...] = wg_ref[0].astype(BF16)
            wub[...] = wu_ref[0].astype(BF16)
            wdb[...] = wd_ref[0].astype(BF16)

        half = TILE_M // 2
        for part in range(2):
            rows = slice(part * half, (part + 1) * half)
            o_ref[rows, :] = _swiglu_rows(xs_ref[rows, :], wgb[...], wub[...], wdb[...]).astype(BF16)


def _combine_kernel(cnt_ref, ebase_ref, cbase_ref, pos_ref, w_ref, xp_ref, xs_ref, gp_ref, gs_ref,
                    h_ref, gf_ref, wsg_ref, wsu_ref, wsd_ref, eo_hbm, yp_ref, ys_ref,
                    loc, locb, ypl, sem, *, n_prompt_chunks):
    c = pl.program_id(0)
    ph = pl.program_id(1)

    @pl.when((c == 0) & (ph == 0))
    def _():
        locb[...] = jnp.zeros(locb.shape, BF16)

    @pl.when(ph == 0)
    def _():
        def fetch_run(n, lpage, gpage, npages):
            def per_page(j, carry):
                _page_copy(eo_hbm, gpage + j, locb, lpage + j, sem).start()
                return carry
            lax.fori_loop(0, npages, per_page, 0)
        n_run = _for_each_run(c, cnt_ref, ebase_ref, cbase_ref, fetch_run)

        def drain(j, carry):
            _page_copy(eo_hbm, 0, locb, 0, sem).wait()
            return carry
        lax.fori_loop(0, n_run, drain, 0)
        for ch in range(PLANES):
            loc[ch * LOC_STRIDE:ch * LOC_STRIDE + LOCAL_ROWS, :] = locb[:, ch * LANE:(ch + 1) * LANE].astype(F32)

    shared = _swiglu_rows(h_ref[...], wsg_ref[...], wsu_ref[...], wsd_ref[...])
    t0 = ph * PHASE_ROWS

    def gather(t8, carry):
        for u in range(8):
            t = t8 * 8 + u
            acc = jnp.zeros((PLANES, LANE), F32)
            for k in range(TOP_K):
                idx = k * TC + t0 + t
                acc = acc + w_ref[0, 0, idx] * _row_slab(loc, pos_ref[0, 0, idx], LOC_STRIDE)[...]
            _row_slab(ypl, t, Y_STRIDE)[...] = acc
        return carry
    lax.fori_loop(0, PHASE_ROWS // 8, gather, 0)
    routed = jnp.concatenate([ypl[ch * Y_STRIDE:ch * Y_STRIDE + PHASE_ROWS, :] for ch in range(PLANES)], axis=1)
    f = shared + routed

    def finish(x_ref, gate_ref, y_ref):
        xo = x_ref[...] + gate_ref[...] * f.reshape(x_ref.shape)
        ms = jnp.mean(xo * xo, axis=-1, keepdims=True)
        y_ref[...] = xo * lax.rsqrt(ms + EPS) * gf_ref[...]

    @pl.when(c < n_prompt_chunks)
    def _():
        finish(xp_ref, gp_ref, yp_ref)

    @pl.when(c >= n_prompt_chunks)
    def _():
        finish(xs_ref, gs_ref, ys_ref)


def _smem_spec():
    return pl.BlockSpec(memory_space=pltpu.SMEM)


def _moe_sparse(x1p, x1s, modp, mods, g_ffn, g_final, w_r_t, rb, wsg, wsu, wsd, weg, weu, wed):
    b, s_len, d = x1p.shape
    db, ds, _ = x1s.shape
    r = GROUP_ROWS
    xp = x1p.reshape(b * s_len // r, r, d)
    xs = x1s.reshape(db * ds // r, r, d)
    mods_g = jnp.repeat(mods, ds // r, axis=0) if ds != r else mods
    npc = b * s_len // TC
    nsc = db * ds // TC
    n_chunks = npc + nsc
    n_tok = n_chunks * TC
    cpb = s_len // TC
    gpc = TC // r
    gpp = PHASE_ROWS // r
    max_pages = (TOP_K * n_tok) // PAGE + n_chunks * N_EXPERTS
    nt_max = -(-(max_pages // PAGES_PER_TILE + N_EXPERTS) // 8) * 8
    sorted_rows = nt_max * TILE_M
    arb1 = pltpu.CompilerParams(dimension_semantics=("arbitrary",), vmem_limit_bytes=VMEM_LIMIT)
    arb2 = pltpu.CompilerParams(dimension_semantics=("arbitrary", "arbitrary"), vmem_limit_bytes=VMEM_LIMIT)
    cst = lambda shape: pl.BlockSpec(shape, lambda *_: (0,) * len(shape))
    tri = jnp.triu(jnp.ones((TC, TC), BF16))
    pclamp = lambda i: jnp.minimum(i, npc - 1)
    sclamp = lambda i: jnp.maximum(i - npc, 0)

    h2, pos, wts, cnt = pl.pallas_call(
        functools.partial(_route_kernel, n_prompt_chunks=npc),
        grid=(n_chunks,),
        in_specs=[
            pl.BlockSpec((gpc, r, d), lambda i: (pclamp(i), 0, 0)),
            pl.BlockSpec((gpc, r, d), lambda i: (sclamp(i), 0, 0)),
            pl.BlockSpec((1, 1, 2 * d), lambda i: (pclamp(i) // cpb, 0, 0)),
            pl.BlockSpec((gpc, 1, 2 * d), lambda i: (sclamp(i), 0, 0)),
            cst((1, d)), cst((N_EXPERTS, d)), cst((N_EXPERTS, 1)), cst((TC, TC)),
        ],
        out_specs=[
            pl.BlockSpec((TC, d), lambda i: (i, 0)),
            pl.BlockSpec((1, TOP_K, TC), lambda i: (i, 0, 0)),
            pl.BlockSpec((1, TOP_K, TC), lambda i: (i, 0, 0)),
            pl.BlockSpec((1, N_EXPERTS, LANE), lambda i: (i, 0, 0)),
        ],
        out_shape=[
            jax.ShapeDtypeStruct((n_tok, d), BF16),
            jax.ShapeDtypeStruct((n_chunks, TOP_K, TC), jnp.int32),
            jax.ShapeDtypeStruct((n_chunks, TOP_K, TC), F32),
            jax.ShapeDtypeStruct((n_chunks, N_EXPERTS, LANE), jnp.int32),
        ],
        compiler_params=arb1,
        name="moe_route",
    )(xp, xs, modp[:, :, 0:2 * d], mods_g[:, :, 0:2 * d], g_ffn, w_r_t, rb, tri)
    counts = cnt[:, :, 0]
    pos = pos.reshape(n_chunks, 1, TOP_K * TC)
    wts = wts.reshape(n_chunks, 1, TOP_K * TC)

    te, nt, ebase, cbase, etp = pl.pallas_call(
        functools.partial(_plan_kernel, n_chunks=n_chunks, nt_max=nt_max),
        in_specs=[_smem_spec()],
        out_specs=[_smem_spec()] * 5,
        out_shape=[
            jax.ShapeDtypeStruct((nt_max,), jnp.int32),
            jax.ShapeDtypeStruct((1,), jnp.int32),
            jax.ShapeDtypeStruct((N_EXPERTS,), jnp.int32),
            jax.ShapeDtypeStruct((n_chunks * N_EXPERTS,), jnp.int32),
            jax.ShapeDtypeStruct((N_EXPERTS,), jnp.int32),
        ],
        name="moe_plan",
    )(counts)

    xsort = pl.pallas_call(
        functools.partial(_dispatch_kernel, n_chunks=n_chunks, nt_max=nt_max),
        grid=(n_chunks,),
        in_specs=[_smem_spec(), _smem_spec(), _smem_spec(), _smem_spec(), _smem_spec(),
                  pl.BlockSpec((1, 1, TOP_K * TC), lambda i: (i, 0, 0), memory_space=pltpu.SMEM),
                  pl.BlockSpec((TC, d), lambda i: (i, 0))],
        out_specs=pl.BlockSpec(memory_space=pl.ANY),
        out_shape=jax.ShapeDtypeStruct((sorted_rows, d), BF16),
        scratch_shapes=[pltpu.VMEM((PLANES * HP_STRIDE, LANE), F32),
                        pltpu.VMEM((PLANES * LOC_STRIDE, LANE), F32),
                        pltpu.VMEM((LOCAL_ROWS, d), BF16),
                        pltpu.VMEM((PAGE, d), BF16),
                        pltpu.VMEM((TILE_M, d), BF16),
                        pltpu.SemaphoreType.DMA, pltpu.SemaphoreType.DMA],
        compiler_params=arb1,
        name="moe_dispatch",
    )(counts, ebase, cbase, etp, nt, pos, h2)

    tile_map = lambda i, te_ref, nt_ref: (jnp.minimum(i, nt_ref[0] - 1), 0)
    w_map = lambda i, te_ref, nt_ref: (te_ref[jnp.minimum(i, nt_ref[0] - 1)], 0, 0)
    eo = pl.pallas_call(
        _expert_kernel,
        grid_spec=pltpu.PrefetchScalarGridSpec(
            num_scalar_prefetch=2,
            grid=(nt_max,),
            in_specs=[
                pl.BlockSpec((TILE_M, d), tile_map),
                pl.BlockSpec((1, d, D_EXPERT), w_map),
                pl.BlockSpec((1, d, D_EXPERT), w_map),
                pl.BlockSpec((1, D_EXPERT, d), w_map),
            ],
            out_specs=pl.BlockSpec((TILE_M, d), tile_map),
            scratch_shapes=[pltpu.VMEM((d, D_EXPERT), BF16), pltpu.VMEM((d, D_EXPERT), BF16),
                            pltpu.VMEM((D_EXPERT, d), BF16)],
        ),
        out_shape=jax.ShapeDtypeStruct((sorted_rows, d), BF16),
        input_output_aliases={2: 0},
        compiler_params=arb1,
        name="moe_experts",
    )(te, nt, xsort, weg, weu, wed)

    nph = COMBINE_PHASES
    pstep = lambda i, j: jnp.minimum(i * nph + j, npc * nph - 1)
    sstep = lambda i, j: jnp.maximum((i - npc) * nph + j, 0)
    smem_chunk = pl.BlockSpec((1, 1, TOP_K * TC), lambda i, j: (i, 0, 0), memory_space=pltpu.SMEM)
    yp, ys = pl.pallas_call(
        functools.partial(_combine_kernel, n_prompt_chunks=npc),
        grid=(n_chunks, nph),
        in_specs=[_smem_spec(), _smem_spec(), _smem_spec(), smem_chunk, smem_chunk,
                  pl.BlockSpec((gpp, r, d), lambda i, j: (pstep(i, j), 0, 0)),
                  pl.BlockSpec((gpp, r, d), lambda i, j: (sstep(i, j), 0, 0)),
                  pl.BlockSpec((1, 1, d), lambda i, j: (pclamp(i) // cpb, 0, 0)),
                  pl.BlockSpec((gpp, 1, d), lambda i, j: (sstep(i, j), 0, 0)),
                  pl.BlockSpec((PHASE_ROWS, d), lambda i, j: (i * nph + j, 0)),
                  cst((1, d)), cst((d, D_EXPERT)), cst((d, D_EXPERT)), cst((D_EXPERT, d)),
                  pl.BlockSpec(memory_space=pl.ANY)],
        out_specs=[pl.BlockSpec((gpp, r, d), lambda i, j: (pstep(i, j), 0, 0)),
                   pl.BlockSpec((gpp, r, d), lambda i, j: (sstep(i, j), 0, 0))],
        out_shape=[jax.ShapeDtypeStruct(xp.shape, F32), jax.ShapeDtypeStruct(xs.shape, F32)],
        scratch_shapes=[pltpu.VMEM((PLANES * LOC_STRIDE, LANE), F32),
                        pltpu.VMEM((LOCAL_ROWS, d), BF16),
                        pltpu.VMEM((PLANES * Y_STRIDE, LANE), F32),
                        pltpu.SemaphoreType.DMA],
        compiler_params=arb2,
        name="moe_combine",
    )(counts, ebase, cbase, pos, wts, xp, xs, modp[:, :, 2 * d:3 * d], mods_g[:, :, 2 * d:3 * d],
      h2, g_final, wsg, wsu, wsd, eo)
    return yp.reshape(b, s_len, d), ys.reshape(db, ds, d)


def _t5_bucket(rel):
    nb = NUM_BUCKETS // 2
    max_exact = nb // 2
    ret = jnp.where(rel > 0, nb, 0)
    n = jnp.abs(rel)
    nf = jnp.maximum(n, 1).astype(F32)
    large = max_exact + (jnp.log(nf / max_exact) / math.log(REL_MAX_DIST / max_exact)
                         * (nb - max_exact)).astype(jnp.int32)
    large = jnp.minimum(large, nb - 1)
    return ret + jnp.where(n < max_exact, n, large)


def _pair_layout(per_head):
    groups = []
    for g in range(N_KV_HEADS):
        pars = []
        for par in range(2):
            pars.append(jnp.concatenate([per_head[4 * g + par], per_head[4 * g + par + 2]], axis=0))
        groups.append(jnp.stack(pars))
    return jnp.stack(groups)


def kernel(x_prompt, x_sample, c_prompt, c_sample, cache_k, cache_v, state_pool, w_ada, b_ada, g_norm_mix, g_norm_ffn, w_in, w_pool_group, pool_scale, attn_sinks, w_pool_proj, w_attn_proj, w_out, rel_table, w_router, router_bias, w_exp_gate, w_exp_up, w_exp_down, w_sh_gate, w_sh_up, w_sh_down, g_final):
    b, s_len, d = x_prompt.shape
    db, ds, _ = x_sample.shape
    l = 0

    c_all = jnp.concatenate([c_prompt, c_sample, jnp.zeros((4, d), F32)], axis=0)
    mods = _ada(c_all, w_ada[l], b_ada[l])
    mod_p = mods[0:b].reshape(b, 1, 6 * d)
    mod_s = mods[b:b + db].reshape(db, 1, 6 * d)

    rel = jnp.arange(BAND)[None, :] - WINDOW - jnp.arange(CHUNK)[:, None]
    bias = jnp.transpose(rel_table[_t5_bucket(rel)], (2, 0, 1)).astype(F32)
    bias4 = _pair_layout(bias)
    sink4 = _pair_layout(jnp.broadcast_to(attn_sinks[l].astype(F32)[:, None, None], (N_HEADS, CHUNK, 128)))
    w_pg = jnp.zeros((POOL_WIDTH, POOL_WIDTH), F32)
    for g in range(len(POOL_WINDOWS)):
        w_pg = w_pg.at[g * POOL_GW:(g + 1) * POOL_GW, g * POOL_GW:(g + 1) * POOL_GW].set(w_pool_group[l, g])
    w_pg = w_pg.astype(BF16)
    w_in_b = w_in[l].astype(BF16)
    w_pp = w_pool_proj[l].astype(BF16)
    w_ap = w_attn_proj[l].astype(BF16)
    w_out_b = w_out[l].astype(BF16)
    g_mix = g_norm_mix[l].reshape(1, d)
    g_ffn = g_norm_ffn[l].reshape(1, d)
    ps = pool_scale[l].reshape(1, POOL_WIDTH)

    x1p, nk_p, nv_p, np_p = _mixer_prompt(x_prompt, mod_p, g_mix, w_in_b, w_pg, ps, bias4, sink4,
                                          w_pp, w_ap, w_out_b)
    ck = cache_k[l].reshape(db, WINDOW, KV_WIDTH)
    cv = cache_v[l].reshape(db, WINDOW, KV_WIDTH)
    sp16 = jnp.pad(state_pool[l], ((0, 0), (HIST - POOL_PAD, 0), (0, 0)))
    x1s, nk_s, nv_s, np_s = _mixer_sample(x_sample, mod_s, ck, cv, sp16, g_mix, w_in_b, w_pg, ps,
                                          bias4, sink4, w_pp, w_ap, w_out_b)

    y_p, y_s = _moe_sparse(x1p, x1s, mod_p[:, :, 3 * d:], mod_s[:, :, 3 * d:], g_ffn, g_final.reshape(1, d),
                           w_router[l].T.astype(BF16), router_bias[l].astype(F32).reshape(N_EXPERTS, 1),
                           w_sh_gate[l].astype(BF16), w_sh_up[l].astype(BF16), w_sh_down[l].astype(BF16),
                           w_exp_gate[l], w_exp_up[l], w_exp_down[l])

    kv_shape = lambda n: (1, n, WINDOW, N_KV_HEADS, HEAD_DIM)
    return (y_p, y_s,
            nk_p.reshape(kv_shape(b)), nv_p.reshape(kv_shape(b)),
            np_p[:, HIST - POOL_PAD:, :][None],
            nk_s.reshape(kv_shape(db)), nv_s.reshape(kv_shape(db)),
            np_s[:, HIST - POOL_PAD:, :][None])
```

```python
import functools
import math

import jax
import jax.numpy as jnp
from jax import lax
from jax.experimental import pallas as pl
from jax.experimental.pallas import tpu as pltpu

F32 = jnp.float32
BF16 = jnp.bfloat16

D_MODEL = 1024
CHUNK = 64
EPS = 1e-6
POOL_WIDTH = 512
POOL_WINDOWS = (2, 4, 8, 16)
POOL_GW = 128
POOL_PAD = 15
HEAD_DIM = 64
N_HEADS = 8
N_KV_HEADS = 2
ATTN_WIDTH = 512
KV_WIDTH = 128
WINDOW = 128
BAND = WINDOW + CHUNK
PAIR_KEYS = BAND + CHUNK
NUM_BUCKETS = 32
REL_MAX_DIST = 128
PAST_LEN = 4096
N_EXPERTS = 32
TOP_K = 4
N_GROUPS = 4
TOPK_GROUPS = 2
EXPERTS_PER_GROUP = 8
D_EXPERT = 256
ROUTED_SCALE = 2.5

OFF_U = 0
OFF_Q = OFF_U + POOL_WIDTH
OFF_K = OFF_Q + ATTN_WIDTH
OFF_V = OFF_K + KV_WIDTH
OFF_GP = OFF_V + KV_WIDTH
OFF_GA = OFF_GP + D_MODEL
IN_WIDTH = OFF_GA + D_MODEL

HIST = 16
MIX_ROWS = 512
MERGE_PANEL = 256
VMEM_LIMIT = 56 * 1024 * 1024

GROUP_ROWS = 64
TC = 1024
PAGE = 16
PAGE_SHIFT = 4
TILE_M = 512
PAGES_PER_TILE = TILE_M // PAGE
TILE_SHIFT = 5
LOCAL_ROWS = TOP_K * TC + N_EXPERTS * PAGE
LANE = 128
PLANES = D_MODEL // LANE
PLANE_PAD = 8
HP_STRIDE = TC + PLANE_PAD
LOC_STRIDE = LOCAL_ROWS + PLANE_PAD
COMBINE_PHASES = 4
PHASE_ROWS = TC // COMBINE_PHASES
RUN_COPY = 4
RUN_COPY_SHIFT = 2
Y_STRIDE = PHASE_ROWS + PLANE_PAD


def _dot(a, b):
    return jnp.dot(a, b, preferred_element_type=F32)


def _dot_nt(a, b):
    return lax.dot_general(a, b, (((1,), (1,)), ((), ())), preferred_element_type=F32)


def _norm_mod(x, g, shift, scale):
    ms = jnp.mean(x * x, axis=-1, keepdims=True)
    y = x * lax.rsqrt(ms + EPS) * g
    return y * (1.0 + scale) + shift


def _silu(x):
    return x * jax.nn.sigmoid(x)


def _ada_kernel(c_ref, w_ref, b_ref, o_ref):
    a = _silu(c_ref[...]).astype(BF16)
    o_ref[...] = _dot(a, w_ref[...].astype(BF16)) + b_ref[...]


def _ada(c_all, w_ada, b_ada):
    rows = c_all.shape[0]
    n = w_ada.shape[1]
    tn = 768
    return pl.pallas_call(
        _ada_kernel,
        grid=(n // tn,),
        in_specs=[
            pl.BlockSpec((rows, D_MODEL), lambda j: (0, 0)),
            pl.BlockSpec((D_MODEL, tn), lambda j: (0, j)),
            pl.BlockSpec((1, tn), lambda j: (0, j)),
        ],
        out_specs=pl.BlockSpec((rows, tn), lambda j: (0, j)),
        out_shape=jax.ShapeDtypeStruct((rows, n), F32),
        compiler_params=pltpu.CompilerParams(dimension_semantics=("arbitrary",)),
        name="ada",
    )(c_all, w_ada, b_ada.reshape(1, n))


def _pool_feats(ue, pos0, rows):
    s2 = ue + pltpu.roll(ue, 1, 0)
    s4 = s2 + pltpu.roll(s2, 2, 0)
    s8 = s4 + pltpu.roll(s4, 4, 0)
    s16 = s8 + pltpu.roll(s8, 8, 0)
    pos = pos0 + lax.broadcasted_iota(jnp.int32, (rows, POOL_GW), 0)
    outs = []
    for g, (w, sw) in enumerate(zip(POOL_WINDOWS, (s2, s4, s8, s16))):
        sl = slice(g * POOL_GW, (g + 1) * POOL_GW)
        cnt = jnp.minimum(pos + 1, w).astype(F32)
        outs.append(sw[HIST:, sl] / cnt - ue[HIST:, sl])
    return jnp.concatenate(outs, axis=1)


def _kv_variants(t):
    lane = lax.broadcasted_iota(jnp.int32, t.shape, 1)
    low = lane < HEAD_DIM
    swapped = pltpu.roll(t, HEAD_DIM, 1)
    zero = jnp.zeros_like(t)
    return (jnp.where(low, t, zero).astype(BF16),
            jnp.where(low, zero, swapped).astype(BF16),
            jnp.where(low, swapped, zero).astype(BF16),
            jnp.where(low, zero, t).astype(BF16))


def _attend_chunk(qc, kvar, vvar, bias_ref, sink_ref, valid):
    outs = []
    for g in range(N_KV_HEADS):
        qs = jnp.concatenate([qc[:, (2 * g) * 128:(2 * g + 1) * 128],
                              qc[:, (2 * g + 1) * 128:(2 * g + 2) * 128]], axis=0)
        o = None
        for par in range(2):
            s = _dot_nt(qs, kvar[2 * g + par]) + bias_ref[g, par]
            if valid is not None:
                s = jnp.where(valid, s, -1e30)
            sink = sink_ref[g, par][:, 0:1]
            m = jnp.maximum(jnp.max(s, axis=-1, keepdims=True), sink)
            e = jnp.exp(s - m)
            denom = jnp.sum(e, axis=-1, keepdims=True) + jnp.exp(sink - m)
            p = (e / denom).astype(BF16)
            pv = _dot(p, vvar[2 * g + par])
            o = pv if o is None else o + pv
        outs.append(o[0:CHUNK])
        outs.append(o[CHUNK:2 * CHUNK])
    return jnp.concatenate(outs, axis=1)


def _merge_out(h, pool_y, attn_y, win_ref, wpp_ref, wap_ref, wout_ref, m_scr):
    rows = h.shape[0]
    pool_b = pool_y.astype(BF16)
    attn_b = attn_y.astype(BF16)
    for n in range(D_MODEL // MERGE_PANEL):
        lo, hi = n * MERGE_PANEL, (n + 1) * MERGE_PANEL
        gp = _dot(h, win_ref[:, OFF_GP + lo:OFF_GP + hi])
        ga = _dot(h, win_ref[:, OFF_GA + lo:OFF_GA + hi])
        pp = _dot(pool_b, wpp_ref[:, lo:hi])
        ap = _dot(attn_b, wap_ref[:, lo:hi])
        m_scr[0:rows, lo:hi] = (jax.nn.sigmoid(gp) * pp + jax.nn.sigmoid(ga) * ap).astype(BF16)
    return _dot(m_scr[0:rows, :], wout_ref[...])


def _mixer_prompt_kernel(x_ref, mod_ref, g_ref, win_ref, wpg_ref, ps_ref, bias_ref, sink_ref,
                         wpp_ref, wap_ref, wout_ref,
                         x1_ref, nk_ref, nv_ref, npool_ref,
                         uext, k0, k1, k2, k3, v0, v1, v2, v3, m_scr, s_scr, p_scr, a_scr):
    ts = MIX_ROWS
    s = pl.program_id(1)
    kext = (k0, k1, k2, k3)
    vext = (v0, v1, v2, v3)

    @pl.when(s == 0)
    def _():
        uext[0:HIST, :] = jnp.zeros((HIST, POOL_WIDTH), F32)
        for r in kext + vext:
            r[0:WINDOW, :] = jnp.zeros((WINDOW, KV_WIDTH), BF16)

    x = x_ref[0]
    shift = mod_ref[0, :, 0:D_MODEL]
    scale = mod_ref[0, :, D_MODEL:2 * D_MODEL]
    gate = mod_ref[0, :, 2 * D_MODEL:3 * D_MODEL]
    h = _norm_mod(x, g_ref[...], shift, scale).astype(BF16)

    u = _dot(h, win_ref[:, OFF_U:OFF_U + POOL_WIDTH])
    q = (_dot(h, win_ref[:, OFF_Q:OFF_Q + ATTN_WIDTH]) * (HEAD_DIM ** -0.5)).astype(BF16)
    k = _dot(h, win_ref[:, OFF_K:OFF_K + KV_WIDTH])
    v = _dot(h, win_ref[:, OFF_V:OFF_V + KV_WIDTH])
    nk_ref[0] = k[ts - WINDOW:ts, :]
    nv_ref[0] = v[ts - WINDOW:ts, :]
    npool_ref[0] = u[ts - HIST:ts, :]

    pos0 = s * ts
    uext[HIST:HIST + ts, :] = u
    p = _pool_feats(uext[...], pos0, ts)
    uext[0:HIST, :] = u[ts - HIST:ts, :]
    pool_y = _dot(p.astype(BF16), wpg_ref[...]) * ps_ref[...]

    for r, t in zip(kext, _kv_variants(k)):
        r[WINDOW:WINDOW + ts, :] = t
    for r, t in zip(vext, _kv_variants(v)):
        r[WINDOW:WINDOW + ts, :] = t
    key_j = lax.broadcasted_iota(jnp.int32, (1, PAIR_KEYS), 1)
    pair_rows = 2 * CHUNK
    n_pairs = ts // pair_rows
    for pr in range(n_pairs):
        lo = pr * pair_rows
        valid = (pos0 + lo + key_j) >= WINDOW
        for g in range(N_KV_HEADS):
            qs = jnp.concatenate([q[lo:lo + pair_rows, (2 * g) * LANE:(2 * g + 1) * LANE],
                                  q[lo:lo + pair_rows, (2 * g + 1) * LANE:(2 * g + 2) * LANE]], axis=0)
            for par in range(2):
                s = _dot_nt(qs, kext[2 * g + par][lo:lo + PAIR_KEYS, :]) + bias_ref[g, par]
                s_scr[(pr * N_KV_HEADS + g) * 2 + par] = jnp.where(valid, s, -1e30)
    per_pair = N_KV_HEADS * 2
    blk_rows = 2 * pair_rows
    sink = sink_ref[...].reshape(per_pair * blk_rows, LANE)[:, 0:1]
    for pr in range(n_pairs):
        s = s_scr[pr * per_pair:(pr + 1) * per_pair].reshape(per_pair * blk_rows, PAIR_KEYS)
        m = jnp.maximum(jnp.max(s, axis=-1, keepdims=True), sink)
        e = jnp.exp(s - m)
        denom = jnp.sum(e, axis=-1, keepdims=True) + jnp.exp(sink - m)
        p_scr[pr * per_pair:(pr + 1) * per_pair] = (
            (e / denom).astype(BF16).reshape(per_pair, blk_rows, PAIR_KEYS))
    for pr in range(n_pairs):
        lo = pr * pair_rows
        for g in range(N_KV_HEADS):
            blk = (pr * N_KV_HEADS + g) * 2
            o = (_dot(p_scr[blk], vext[2 * g][lo:lo + PAIR_KEYS, :])
                 + _dot(p_scr[blk + 1], vext[2 * g + 1][lo:lo + PAIR_KEYS, :]))
            a_scr[lo:lo + pair_rows, (2 * g) * LANE:(2 * g + 1) * LANE] = o[0:pair_rows].astype(BF16)
            a_scr[lo:lo + pair_rows, (2 * g + 1) * LANE:(2 * g + 2) * LANE] = o[pair_rows:].astype(BF16)
    for r in kext + vext:
        r[0:WINDOW, :] = r[ts:ts + WINDOW, :]

    y = _merge_out(h, pool_y, a_scr[...], win_ref, wpp_ref, wap_ref, wout_ref, m_scr)
    x1_ref[0] = x + gate * y


def _const_spec(shape):
    nd = len(shape)
    return pl.BlockSpec(shape, lambda *_: (0,) * nd)


def _mixer_prompt(x, mod, g_mix, w_in, w_pg, pool_scale, bias4, sink4, w_pp, w_ap, w_out):
    b, s_len, d = x.shape
    ts = MIX_ROWS
    pair_rows = 2 * CHUNK
    n_blocks = (ts // pair_rows) * N_KV_HEADS * 2
    kv_scratch = [pltpu.VMEM((WINDOW + ts, KV_WIDTH), BF16) for _ in range(8)]
    return pl.pallas_call(
        _mixer_prompt_kernel,
        grid=(b, s_len // ts),
        in_specs=[
            pl.BlockSpec((1, ts, d), lambda i, j: (i, j, 0)),
            pl.BlockSpec((1, 1, 6 * d), lambda i, j: (i, 0, 0)),
            _const_spec((1, d)),
            _const_spec((d, IN_WIDTH)),
            _const_spec((POOL_WIDTH, POOL_WIDTH)),
            _const_spec((1, POOL_WIDTH)),
            _const_spec(bias4.shape),
            _const_spec(sink4.shape),
            _const_spec((POOL_WIDTH, d)),
            _const_spec((ATTN_WIDTH, d)),
            _const_spec((d, d)),
        ],
        out_specs=[
            pl.BlockSpec((1, ts, d), lambda i, j: (i, j, 0)),
            pl.BlockSpec((1, WINDOW, KV_WIDTH), lambda i, j: (i, 0, 0)),
            pl.BlockSpec((1, WINDOW, KV_WIDTH), lambda i, j: (i, 0, 0)),
            pl.BlockSpec((1, HIST, POOL_WIDTH), lambda i, j: (i, 0, 0)),
        ],
        out_shape=[
            jax.ShapeDtypeStruct((b, s_len, d), F32),
            jax.ShapeDtypeStruct((b, WINDOW, KV_WIDTH), F32),
            jax.ShapeDtypeStruct((b, WINDOW, KV_WIDTH), F32),
            jax.ShapeDtypeStruct((b, HIST, POOL_WIDTH), F32),
        ],
        scratch_shapes=[pltpu.VMEM((HIST + ts, POOL_WIDTH), F32)] + kv_scratch
                       + [pltpu.VMEM((ts, d), BF16),
                          pltpu.VMEM((n_blocks, 2 * pair_rows, PAIR_KEYS), F32),
                          pltpu.VMEM((n_blocks, 2 * pair_rows, PAIR_KEYS), BF16),
                          pltpu.VMEM((ts, ATTN_WIDTH), BF16)],
        compiler_params=pltpu.CompilerParams(
            dimension_semantics=("arbitrary", "arbitrary"), vmem_limit_bytes=VMEM_LIMIT),
        name="mixer_prompt",
    )(x, mod, g_mix, w_in, w_pg, pool_scale, bias4, sink4, w_pp, w_ap, w_out)


def _mixer_sample_kernel(x_ref, mod_ref, ck_ref, cv_ref, sp_ref, g_ref, win_ref, wpg_ref, ps_ref,
                         bias_ref, sink_ref, wpp_ref, wap_ref, wout_ref,
                         x1_ref, nk_ref, nv_ref, npool_ref, m_scr):
    nseq, ds, d = x_ref.shape
    rows = nseq * ds
    x = x_ref[...]
    shift = mod_ref[:, :, 0:D_MODEL]
    scale = mod_ref[:, :, D_MODEL:2 * D_MODEL]
    gate = mod_ref[:, :, 2 * D_MODEL:3 * D_MODEL]
    h = _norm_mod(x, g_ref[...], shift, scale).reshape(rows, d).astype(BF16)

    u = _dot(h, win_ref[:, OFF_U:OFF_U + POOL_WIDTH])
    q = (_dot(h, win_ref[:, OFF_Q:OFF_Q + ATTN_WIDTH]) * (HEAD_DIM ** -0.5)).astype(BF16)
    k = _dot(h, win_ref[:, OFF_K:OFF_K + KV_WIDTH])
    v = _dot(h, win_ref[:, OFF_V:OFF_V + KV_WIDTH])

    pool_parts = []
    attn_parts = []
    for b in range(nseq):
        lo = b * ds
        ub = u[lo:lo + ds, :]
        kb = k[lo:lo + ds, :]
        vb = v[lo:lo + ds, :]
        ck = ck_ref[b]
        cv = cv_ref[b]
        nk_ref[b] = jnp.concatenate([ck[ds:WINDOW, :], kb], axis=0)
        nv_ref[b] = jnp.concatenate([cv[ds:WINDOW, :], vb], axis=0)
        npool_ref[b] = ub[ds - HIST:ds, :]
        ue = jnp.concatenate([sp_ref[b], ub], axis=0)
        pool_parts.append(_pool_feats(ue, PAST_LEN, ds))
        kvar = _kv_variants(jnp.concatenate([ck, kb], axis=0))
        vvar = _kv_variants(jnp.concatenate([cv, vb], axis=0))
        attn_parts.append(_attend_chunk(q[lo:lo + ds, :], kvar, vvar, bias_ref, sink_ref, None))
    p = jnp.concatenate(pool_parts, axis=0)
    attn_y = jnp.concatenate(attn_parts, axis=0)
    pool_y = _dot(p.astype(BF16), wpg_ref[...]) * ps_ref[...]

    y = _merge_out(h, pool_y, attn_y, win_ref, wpp_ref, wap_ref, wout_ref, m_scr)
    x1_ref[...] = x + gate * y.reshape(nseq, ds, d)


def _mixer_sample(x, mod, ck, cv, sp16, g_mix, w_in, w_pg, pool_scale, bias4, sink4, w_pp, w_ap, w_out):
    db, ds, d = x.shape
    nseq = MIX_ROWS // ds
    seq_spec = lambda shape: pl.BlockSpec((nseq,) + shape, lambda i: (i, 0, 0))
    return pl.pallas_call(
        _mixer_sample_kernel,
        grid=(db // nseq,),
        in_specs=[
            seq_spec((ds, d)),
            seq_spec((1, 6 * d)),
            seq_spec((WINDOW, KV_WIDTH)),
            seq_spec((WINDOW, KV_WIDTH)),
            seq_spec((HIST, POOL_WIDTH)),
            _const_spec((1, d)),
            _const_spec((d, IN_WIDTH)),
            _const_spec((POOL_WIDTH, POOL_WIDTH)),
            _const_spec((1, POOL_WIDTH)),
            _const_spec(bias4.shape),
            _const_spec(sink4.shape),
            _const_spec((POOL_WIDTH, d)),
            _const_spec((ATTN_WIDTH, d)),
            _const_spec((d, d)),
        ],
        out_specs=[
            seq_spec((ds, d)),
            seq_spec((WINDOW, KV_WIDTH)),
            seq_spec((WINDOW, KV_WIDTH)),
            seq_spec((HIST, POOL_WIDTH)),
        ],
        out_shape=[
            jax.ShapeDtypeStruct((db, ds, d), F32),
            jax.ShapeDtypeStruct((db, WINDOW, KV_WIDTH), F32),
            jax.ShapeDtypeStruct((db, WINDOW, KV_WIDTH), F32),
            jax.ShapeDtypeStruct((db, HIST, POOL_WIDTH), F32),
        ],
        scratch_shapes=[pltpu.VMEM((MIX_ROWS, d), BF16)],
        compiler_params=pltpu.CompilerParams(
            dimension_semantics=("arbitrary",), vmem_limit_bytes=VMEM_LIMIT),
        name="mixer_sample",
    )(x, mod, ck, cv, sp16, g_mix, w_in, w_pg, pool_scale, bias4, sink4, w_pp, w_ap, w_out)


def _first_argmax(vals, idx, size):
    m = jnp.max(vals, axis=0, keepdims=True)
    first = jnp.min(jnp.where(vals == m, idx, size), axis=0, keepdims=True)
    return m, first


def _router_picks(h, wr_ref, rb_ref):
    t = h.shape[0]
    logits = _dot_nt(wr_ref[...], h)
    scores = jax.nn.sigmoid(logits)
    sel = scores + rb_ref[...]
    neg = jnp.float32(-jnp.inf)
    eidx = lax.broadcasted_iota(jnp.int32, (N_EXPERTS, t), 0)

    gscores = []
    bidx = lax.broadcasted_iota(jnp.int32, (EXPERTS_PER_GROUP, t), 0)
    for g in range(N_GROUPS):
        blk = sel[g * EXPERTS_PER_GROUP:(g + 1) * EXPERTS_PER_GROUP, :]
        m1, i1 = _first_argmax(blk, bidx, EXPERTS_PER_GROUP)
        m2 = jnp.max(jnp.where(bidx == i1, neg, blk), axis=0, keepdims=True)
        gscores.append(m1 + m2)
    gs = jnp.concatenate(gscores, axis=0)
    gidx = lax.broadcasted_iota(jnp.int32, (N_GROUPS, t), 0)
    _, g1 = _first_argmax(gs, gidx, N_GROUPS)
    _, g2 = _first_argmax(jnp.where(gidx == g1, neg, gs), gidx, N_GROUPS)
    egrp = eidx // EXPERTS_PER_GROUP
    cand = jnp.where((egrp == g1) | (egrp == g2), sel, neg)

    picks = []
    for _ in range(TOP_K):
        _, i = _first_argmax(cand, eidx, N_EXPERTS)
        hit = eidx == i
        picks.append(hit)
        cand = jnp.where(hit, neg, cand)
    return picks, scores


def _swiglu_rows(h, wg, wu, wd):
    act = _silu(_dot(h, wg)) * _dot(h, wu)
    return _dot(act.astype(BF16), wd)


def _route_kernel(xp_ref, xs_ref, modp_ref, mods_ref, g_ref, wr_ref, rb_ref, tri_ref,
                  h_ref, pos_ref, w_ref, cnt_ref, *, n_prompt_chunks):
    is_prompt = pl.program_id(0) < n_prompt_chunks
    x = jnp.where(is_prompt, xp_ref[...], xs_ref[...])
    shift = jnp.where(is_prompt, modp_ref[:, :, 0:D_MODEL], mods_ref[:, :, 0:D_MODEL])
    scale = jnp.where(is_prompt, modp_ref[:, :, D_MODEL:2 * D_MODEL], mods_ref[:, :, D_MODEL:2 * D_MODEL])
    h = _norm_mod(x, g_ref[...], shift, scale).reshape(TC, D_MODEL).astype(BF16)
    h_ref[...] = h
    picks, scores = _router_picks(h, wr_ref, rb_ref)
    picked = picks[0] | picks[1] | picks[2] | picks[3]
    cnt = _dot(jnp.where(picked, 1.0, 0.0).astype(BF16), tri_ref[...])
    n_e = jnp.max(cnt, axis=1, keepdims=True).astype(jnp.int32)
    n_b = jnp.broadcast_to(n_e, (N_EXPERTS, LANE))
    cnt_ref[0] = n_b
    padded = lax.shift_left(lax.shift_right_logical(n_b + (PAGE - 1), PAGE_SHIFT), PAGE_SHIFT)
    row = lax.broadcasted_iota(jnp.int32, (N_EXPERTS, LANE), 0)
    scan = padded
    for s in (1, 2, 4, 8, 16):
        scan = scan + jnp.where(row >= s, pltpu.roll(scan, s, 0), 0)
    base = (scan - padded)[:, 0:1].astype(F32)
    slot = base + cnt - 1.0
    pos = [jnp.sum(jnp.where(p, slot, 0.0), axis=0, keepdims=True) for p in picks]
    wk = [jnp.sum(jnp.where(p, scores, 0.0), axis=0, keepdims=True) for p in picks]
    wsum = wk[0] + wk[1] + wk[2] + wk[3]
    pos_ref[0] = jnp.concatenate(pos, axis=0).astype(jnp.int32)
    w_ref[0] = jnp.concatenate(wk, axis=0) / wsum * ROUTED_SCALE


def _plan_kernel(cnt_ref, te_ref, nt_ref, ebase_ref, cbase_ref, etp_ref, *, n_chunks, nt_max):
    def per_expert(e, tile0):
        def per_chunk(c, acc):
            cbase_ref[c * N_EXPERTS + e] = acc
            return acc + lax.shift_right_logical(cnt_ref[c, e] + (PAGE - 1), PAGE_SHIFT)
        tp = lax.fori_loop(0, n_chunks, per_chunk, 0)
        etp_ref[e] = tp
        ebase_ref[e] = tile0 * PAGES_PER_TILE
        nt = lax.shift_right_logical(tp + (PAGES_PER_TILE - 1), TILE_SHIFT)

        def fill(j, carry):
            te_ref[tile0 + j] = e
            return carry
        lax.fori_loop(0, nt, fill, 0)
        return tile0 + nt
    total = lax.fori_loop(0, N_EXPERTS, per_expert, 0)
    nt_ref[0] = total

    def tail(j, carry):
        te_ref[j] = N_EXPERTS - 1
        return carry
    lax.fori_loop(total, nt_max, tail, 0)


def _page_copy(src, src_page, dst, dst_page, sem, pages=1):
    rows = pages * PAGE
    s = src.at[pl.ds(pl.multiple_of(src_page * PAGE, PAGE), rows), :]
    t = dst.at[pl.ds(pl.multiple_of(dst_page * PAGE, PAGE), rows), :]
    return pltpu.make_async_copy(s, t, sem)


def _for_each_run(c, cnt_ref, ebase_ref, cbase_ref, fn):
    def per_expert(e, carry):
        lpage, nbig, nsmall = carry
        n = cnt_ref[c, e]
        npages = lax.shift_right_logical(n + (PAGE - 1), PAGE_SHIFT)
        fn(n, lpage, ebase_ref[e] + cbase_ref[c * N_EXPERTS + e], npages)
        return (lpage + npages, nbig + lax.shift_right_logical(npages, RUN_COPY_SHIFT),
                nsmall + (npages & (RUN_COPY - 1)))
    return lax.fori_loop(0, N_EXPERTS, per_expert, (0, 0, 0))


def _start_run_copies(src, src_page, dst, dst_page, npages, sem_big, sem_small):
    nbig = lax.shift_right_logical(npages, RUN_COPY_SHIFT)

    def big(j, carry):
        _page_copy(src, src_page + j * RUN_COPY, dst, dst_page + j * RUN_COPY, sem_big, RUN_COPY).start()
        return carry
    lax.fori_loop(0, nbig, big, 0)

    def small(j, carry):
        _page_copy(src, src_page + j, dst, dst_page + j, sem_small).start()
        return carry
    lax.fori_loop(nbig * RUN_COPY, npages, small, 0)


def _wait_copies(src, dst, nbig, nsmall, sem_big, sem_small):
    def big(j, carry):
        _page_copy(src, 0, dst, 0, sem_big, RUN_COPY).wait()
        return carry
    lax.fori_loop(0, nbig, big, 0)

    def small(j, carry):
        _page_copy(src, 0, dst, 0, sem_small).wait()
        return carry
    lax.fori_loop(0, nsmall, small, 0)


def _row_slab(ref, row, stride):
    return ref.at[pl.ds(row, PLANES, stride=stride), :]


def _dispatch_kernel(cnt_ref, ebase_ref, cbase_ref, etp_ref, nt_ref, pos_ref, h_ref, xs_hbm,
                     hpl, loc, locb, zpage, ztile, pend, sem, bsem, tsem, *, n_chunks, nt_max):
    c = pl.program_id(0)

    @pl.when(c == 0)
    def _():
        pend[0] = 0
        pend[1] = 0

    h = h_ref[...]
    for ch in range(PLANES):
        hpl[ch * HP_STRIDE:ch * HP_STRIDE + TC, :] = h[:, ch * LANE:(ch + 1) * LANE].astype(F32)
    zpage[...] = jnp.zeros(zpage.shape, BF16)
    ztile[...] = jnp.zeros(ztile.shape, BF16)
    zero_slab = jnp.zeros((PLANES, LANE), F32)

    def zero_pad_rows(n, lpage, gpage, npages):
        def per_row(r, carry):
            _row_slab(loc, r, LOC_STRIDE)[...] = zero_slab
            return carry
        lax.fori_loop(lpage * PAGE + n, (lpage + npages) * PAGE, per_row, 0)
    _for_each_run(c, cnt_ref, ebase_ref, cbase_ref, zero_pad_rows)

    def scatter(t8, carry):
        for u in range(8):
            t = t8 * 8 + u
            slab = _row_slab(hpl, t, HP_STRIDE)[...]
            for k in range(TOP_K):
                _row_slab(loc, pos_ref[0, 0, k * TC + t], LOC_STRIDE)[...] = slab
        return carry
    lax.fori_loop(0, TC // 8, scatter, 0)

    _wait_copies(locb, xs_hbm, pend[0], pend[1], bsem, sem)
    for ch in range(PLANES):
        locb[:, ch * LANE:(ch + 1) * LANE] = loc[ch * LOC_STRIDE:ch * LOC_STRIDE + LOCAL_ROWS, :].astype(BF16)

    def send_run(n, lpage, gpage, npages):
        _start_run_copies(locb, lpage, xs_hbm, gpage, npages, bsem, sem)
    _, n_big, n_small = _for_each_run(c, cnt_ref, ebase_ref, cbase_ref, send_run)
    pend[0] = n_big
    pend[1] = n_small

    @pl.when(c == n_chunks - 1)
    def _():
        def pad_expert(e, n):
            tp = etp_ref[e]
            full = lax.shift_left(lax.shift_right_logical(tp + (PAGES_PER_TILE - 1), TILE_SHIFT), TILE_SHIFT)

            def per_page(j, carry):
                _page_copy(zpage, 0, xs_hbm, ebase_ref[e] + j, sem).start()
                return carry
            lax.fori_loop(tp, full, per_page, 0)
            return n + full - tp
        n_pad = lax.fori_loop(0, N_EXPERTS, pad_expert, 0)

        def tile_copy(i):
            dst = xs_hbm.at[pl.ds(pl.multiple_of(i * TILE_M, TILE_M), TILE_M), :]
            return pltpu.make_async_copy(ztile, dst, tsem)

        def tail_start(i, carry):
            tile_copy(i).start()
            return carry
        lax.fori_loop(nt_ref[0], nt_max, tail_start, 0)
        _wait_copies(locb, xs_hbm, n_big, n_small + n_pad, bsem, sem)

        def tail_wait(i, carry):
            tile_copy(i).wait()
            return carry
        lax.fori_loop(nt_ref[0], nt_max, tail_wait, 0)


def _expert_kernel(te_ref, nt_ref, xs_ref, wg0_ref, wu0_ref, wd0_ref, wg1_ref, wu1_ref, wd1_ref, o_ref,
                   wgb, wub, wdb):
    i = pl.program_id(0)
    nt = nt_ref[0]
    slots = ((wg0_ref, wu0_ref, wd0_ref), (wg1_ref, wu1_ref, wd1_ref))

    @pl.when(2 * i < nt)
    def _():
        for slot in range(2):
            tile = jnp.minimum(2 * i + slot, nt - 1)
            wg_ref, wu_ref, wd_ref = slots[slot]

            @pl.when((i == 0) | (te_ref[tile] != te_ref[jnp.maximum(2 * i + slot - 2, 0)]))
            def _():
                wgb[slot] = wg_ref[0].astype(BF16)
                wub[slot] = wu_ref[0].astype(BF16)
                wdb[slot] = wd_ref[0].astype(BF16)
        for slot in range(2):
            rows = slice(slot * TILE_M, (slot + 1) * TILE_M)
            o_ref[rows, :] = _swiglu_rows(xs_ref[rows, :], wgb[slot], wub[slot], wdb[slot]).astype(BF16)


def _combine_kernel(cnt_ref, ebase_ref, cbase_ref, pos_ref, w_ref, xp_ref, xs_ref, gp_ref, gs_ref,
                    h_ref, gf_ref, wsg_ref, wsu_ref, wsd_ref, eo_hbm, yp_ref, ys_ref,
                    loc, locb, ypl, pend, sem, bsem, *, n_prompt_chunks, n_chunks):
    c = pl.program_id(0)
    ph = pl.program_id(1)

    def fetch(chunk):
        buf = locb.at[chunk % 2]

        def fetch_run(n, lpage, gpage, npages):
            _start_run_copies(eo_hbm, gpage, buf, lpage, npages, bsem, sem)
        _, n_big, n_small = _for_each_run(chunk, cnt_ref, ebase_ref, cbase_ref, fetch_run)
        pend[0] = n_big
        pend[1] = n_small

    @pl.when((c == 0) & (ph == 0))
    def _():
        locb[...] = jnp.zeros(locb.shape, BF16)
        fetch(c)

    @pl.when(ph == 0)
    def _():
        cur = locb.at[c % 2]
        _wait_copies(eo_hbm, cur, pend[0], pend[1], bsem, sem)

        @pl.when(c + 1 < n_chunks)
        def _():
            fetch(c + 1)
        for ch in range(PLANES):
            loc[ch * LOC_STRIDE:ch * LOC_STRIDE + LOCAL_ROWS, :] = cur[:, ch * LANE:(ch + 1) * LANE].astype(F32)

    shared = _swiglu_rows(h_ref[...], wsg_ref[...], wsu_ref[...], wsd_ref[...])
    t0 = ph * PHASE_ROWS

    def gather(t8, carry):
        for u in range(8):
            t = t8 * 8 + u
            acc = jnp.zeros((PLANES, LANE), F32)
            for k in range(TOP_K):
                idx = k * TC + t0 + t
                acc = acc + w_ref[0, 0, idx] * _row_slab(loc, pos_ref[0, 0, idx], LOC_STRIDE)[...]
            _row_slab(ypl, t, Y_STRIDE)[...] = acc
        return carry
    lax.fori_loop(0, PHASE_ROWS // 8, gather, 0)
    routed = jnp.concatenate([ypl[ch * Y_STRIDE:ch * Y_STRIDE + PHASE_ROWS, :] for ch in range(PLANES)], axis=1)
    f = shared + routed

    def finish(x_ref, gate_ref, y_ref):
        xo = x_ref[...] + gate_ref[...] * f.reshape(x_ref.shape)
        ms = jnp.mean(xo * xo, axis=-1, keepdims=True)
        y_ref[...] = xo * lax.rsqrt(ms + EPS) * gf_ref[...]

    @pl.when(c < n_prompt_chunks)
    def _():
        finish(xp_ref, gp_ref, yp_ref)

    @pl.when(c >= n_prompt_chunks)
    def _():
        finish(xs_ref, gs_ref, ys_ref)


def _smem_spec():
    return pl.BlockSpec(memory_space=pltpu.SMEM)


def _moe_sparse(x1p, x1s, modp, mods, g_ffn, g_final, w_r_t, rb, wsg, wsu, wsd, weg, weu, wed):
    b, s_len, d = x1p.shape
    db, ds, _ = x1s.shape
    r = GROUP_ROWS
    xp = x1p.reshape(b * s_len // r, r, d)
    xs = x1s.reshape(db * ds // r, r, d)
    mods_g = jnp.repeat(mods, ds // r, axis=0) if ds != r else mods
    npc = b * s_len // TC
    nsc = db * ds // TC
    n_chunks = npc + nsc
    n_tok = n_chunks * TC
    cpb = s_len // TC
    gpc = TC // r
    gpp = PHASE_ROWS // r
    max_pages = (TOP_K * n_tok) // PAGE + n_chunks * N_EXPERTS
    nt_max = -(-(max_pages // PAGES_PER_TILE + N_EXPERTS) // 8) * 8
    sorted_rows = nt_max * TILE_M
    arb1 = pltpu.CompilerParams(dimension_semantics=("arbitrary",), vmem_limit_bytes=VMEM_LIMIT)
    arb2 = pltpu.CompilerParams(dimension_semantics=("arbitrary", "arbitrary"), vmem_limit_bytes=VMEM_LIMIT)
    cst = lambda shape: pl.BlockSpec(shape, lambda *_: (0,) * len(shape))
    tri = jnp.triu(jnp.ones((TC, TC), BF16))
    pclamp = lambda i: jnp.minimum(i, npc - 1)
    sclamp = lambda i: jnp.maximum(i - npc, 0)

    h2, pos, wts, cnt = pl.pallas_call(
        functools.partial(_route_kernel, n_prompt_chunks=npc),
        grid=(n_chunks,),
        in_specs=[
            pl.BlockSpec((gpc, r, d), lambda i: (pclamp(i), 0, 0)),
            pl.BlockSpec((gpc, r, d), lambda i: (sclamp(i), 0, 0)),
            pl.BlockSpec((1, 1, 2 * d), lambda i: (pclamp(i) // cpb, 0, 0)),
            pl.BlockSpec((gpc, 1, 2 * d), lambda i: (sclamp(i), 0, 0)),
            cst((1, d)), cst((N_EXPERTS, d)), cst((N_EXPERTS, 1)), cst((TC, TC)),
        ],
        out_specs=[
            pl.BlockSpec((TC, d), lambda i: (i, 0)),
            pl.BlockSpec((1, TOP_K, TC), lambda i: (i, 0, 0)),
            pl.BlockSpec((1, TOP_K, TC), lambda i: (i, 0, 0)),
            pl.BlockSpec((1, N_EXPERTS, LANE), lambda i: (i, 0, 0)),
        ],
        out_shape=[
            jax.ShapeDtypeStruct((n_tok, d), BF16),
            jax.ShapeDtypeStruct((n_chunks, TOP_K, TC), jnp.int32),
            jax.ShapeDtypeStruct((n_chunks, TOP_K, TC), F32),
            jax.ShapeDtypeStruct((n_chunks, N_EXPERTS, LANE), jnp.int32),
        ],
        compiler_params=arb1,
        name="moe_route",
    )(xp, xs, modp[:, :, 0:2 * d], mods_g[:, :, 0:2 * d], g_ffn, w_r_t, rb, tri)
    counts = cnt[:, :, 0]
    pos = pos.reshape(n_chunks, 1, TOP_K * TC)
    wts = wts.reshape(n_chunks, 1, TOP_K * TC)

    te, nt, ebase, cbase, etp = pl.pallas_call(
        functools.partial(_plan_kernel, n_chunks=n_chunks, nt_max=nt_max),
        in_specs=[_smem_spec()],
        out_specs=[_smem_spec()] * 5,
        out_shape=[
            jax.ShapeDtypeStruct((nt_max,), jnp.int32),
            jax.ShapeDtypeStruct((1,), jnp.int32),
            jax.ShapeDtypeStruct((N_EXPERTS,), jnp.int32),
            jax.ShapeDtypeStruct((n_chunks * N_EXPERTS,), jnp.int32),
            jax.ShapeDtypeStruct((N_EXPERTS,), jnp.int32),
        ],
        name="moe_plan",
    )(counts)

    xsort = pl.pallas_call(
        functools.partial(_dispatch_kernel, n_chunks=n_chunks, nt_max=nt_max),
        grid=(n_chunks,),
        in_specs=[_smem_spec(), _smem_spec(), _smem_spec(), _smem_spec(), _smem_spec(),
                  pl.BlockSpec((1, 1, TOP_K * TC), lambda i: (i, 0, 0), memory_space=pltpu.SMEM),
                  pl.BlockSpec((TC, d), lambda i: (i, 0))],
        out_specs=pl.BlockSpec(memory_space=pl.ANY),
        out_shape=jax.ShapeDtypeStruct((sorted_rows, d), BF16),
        scratch_shapes=[pltpu.VMEM((PLANES * HP_STRIDE, LANE), F32),
                        pltpu.VMEM((PLANES * LOC_STRIDE, LANE), F32),
                        pltpu.VMEM((LOCAL_ROWS, d), BF16),
                        pltpu.VMEM((PAGE, d), BF16),
                        pltpu.VMEM((TILE_M, d), BF16),
                        pltpu.SMEM((2,), jnp.int32),
                        pltpu.SemaphoreType.DMA, pltpu.SemaphoreType.DMA, pltpu.SemaphoreType.DMA],
        compiler_params=arb1,
        name="moe_dispatch",
    )(counts, ebase, cbase, etp, nt, pos, h2)

    tile_map = lambda i, te_ref, nt_ref: (jnp.minimum(i, (nt_ref[0] - 1) // 2), 0)

    def w_map(slot):
        return lambda i, te_ref, nt_ref: (te_ref[jnp.minimum(2 * i + slot, nt_ref[0] - 1)], 0, 0)
    w_specs = [pl.BlockSpec(shape, w_map(slot)) for slot in range(2)
               for shape in ((1, d, D_EXPERT), (1, d, D_EXPERT), (1, D_EXPERT, d))]
    eo = pl.pallas_call(
        _expert_kernel,
        grid_spec=pltpu.PrefetchScalarGridSpec(
            num_scalar_prefetch=2,
            grid=(nt_max // 2,),
            in_specs=[pl.BlockSpec((2 * TILE_M, d), tile_map)] + w_specs,
            out_specs=pl.BlockSpec((2 * TILE_M, d), tile_map),
            scratch_shapes=[pltpu.VMEM((2, d, D_EXPERT), BF16), pltpu.VMEM((2, d, D_EXPERT), BF16),
                            pltpu.VMEM((2, D_EXPERT, d), BF16)],
        ),
        out_shape=jax.ShapeDtypeStruct((sorted_rows, d), BF16),
        input_output_aliases={2: 0},
        compiler_params=arb1,
        name="moe_experts",
    )(te, nt, xsort, weg, weu, wed, weg, weu, wed)

    nph = COMBINE_PHASES
    pstep = lambda i, j: jnp.minimum(i * nph + j, npc * nph - 1)
    sstep = lambda i, j: jnp.maximum((i - npc) * nph + j, 0)
    smem_chunk = pl.BlockSpec((1, 1, TOP_K * TC), lambda i, j: (i, 0, 0), memory_space=pltpu.SMEM)
    yp, ys = pl.pallas_call(
        functools.partial(_combine_kernel, n_prompt_chunks=npc, n_chunks=n_chunks),
        grid=(n_chunks, nph),
        in_specs=[_smem_spec(), _smem_spec(), _smem_spec(), smem_chunk, smem_chunk,
                  pl.BlockSpec((gpp, r, d), lambda i, j: (pstep(i, j), 0, 0)),
                  pl.BlockSpec((gpp, r, d), lambda i, j: (sstep(i, j), 0, 0)),
                  pl.BlockSpec((1, 1, d), lambda i, j: (pclamp(i) // cpb, 0, 0)),
                  pl.BlockSpec((gpp, 1, d), lambda i, j: (sstep(i, j), 0, 0)),
                  pl.BlockSpec((PHASE_ROWS, d), lambda i, j: (i * nph + j, 0)),
                  cst((1, d)), cst((d, D_EXPERT)), cst((d, D_EXPERT)), cst((D_EXPERT, d)),
                  pl.BlockSpec(memory_space=pl.ANY)],
        out_specs=[pl.BlockSpec((gpp, r, d), lambda i, j: (pstep(i, j), 0, 0)),
                   pl.BlockSpec((gpp, r, d), lambda i, j: (sstep(i, j), 0, 0))],
        out_shape=[jax.ShapeDtypeStruct(xp.shape, F32), jax.ShapeDtypeStruct(xs.shape, F32)],
        scratch_shapes=[pltpu.VMEM((PLANES * LOC_STRIDE, LANE), F32),
                        pltpu.VMEM((2, LOCAL_ROWS, d), BF16),
                        pltpu.VMEM((PLANES * Y_STRIDE, LANE), F32),
                        pltpu.SMEM((2,), jnp.int32),
                        pltpu.SemaphoreType.DMA, pltpu.SemaphoreType.DMA],
        compiler_params=arb2,
        name="moe_combine",
    )(counts, ebase, cbase, pos, wts, xp, xs, modp[:, :, 2 * d:3 * d], mods_g[:, :, 2 * d:3 * d],
      h2, g_final, wsg, wsu, wsd, eo)
    return yp.reshape(b, s_len, d), ys.reshape(db, ds, d)


def _t5_bucket(rel):
    nb = NUM_BUCKETS // 2
    max_exact = nb // 2
    ret = jnp.where(rel > 0, nb, 0)
    n = jnp.abs(rel)
    nf = jnp.maximum(n, 1).astype(F32)
    large = max_exact + (jnp.log(nf / max_exact) / math.log(REL_MAX_DIST / max_exact)
                         * (nb - max_exact)).astype(jnp.int32)
    large = jnp.minimum(large, nb - 1)
    return ret + jnp.where(n < max_exact, n, large)


def _pair_layout(per_head):
    groups = []
    for g in range(N_KV_HEADS):
        pars = []
        for par in range(2):
            pars.append(jnp.concatenate([per_head[4 * g + par], per_head[4 * g + par + 2]], axis=0))
        groups.append(jnp.stack(pars))
    return jnp.stack(groups)


def _pair_window_layout(bias):
    off = jnp.full((CHUNK, CHUNK), -1e30, F32)
    groups = []
    for g in range(N_KV_HEADS):
        pars = []
        for par in range(2):
            rows = []
            for hp in range(2):
                head = 4 * g + par + 2 * hp
                rows.append(jnp.concatenate([bias[head], off], axis=1))
                rows.append(jnp.concatenate([off, bias[head]], axis=1))
            pars.append(jnp.concatenate(rows, axis=0))
        groups.append(jnp.stack(pars))
    return jnp.stack(groups)


def kernel(x_prompt, x_sample, c_prompt, c_sample, cache_k, cache_v, state_pool, w_ada, b_ada, g_norm_mix, g_norm_ffn, w_in, w_pool_group, pool_scale, attn_sinks, w_pool_proj, w_attn_proj, w_out, rel_table, w_router, router_bias, w_exp_gate, w_exp_up, w_exp_down, w_sh_gate, w_sh_up, w_sh_down, g_final):
    b, s_len, d = x_prompt.shape
    db, ds, _ = x_sample.shape
    l = 0

    c_all = jnp.concatenate([c_prompt, c_sample, jnp.zeros((4, d), F32)], axis=0)
    mods = _ada(c_all, w_ada[l], b_ada[l])
    mod_p = mods[0:b].reshape(b, 1, 6 * d)
    mod_s = mods[b:b + db].reshape(db, 1, 6 * d)

    rel = jnp.arange(BAND)[None, :] - WINDOW - jnp.arange(CHUNK)[:, None]
    bias = jnp.transpose(rel_table[_t5_bucket(rel)], (2, 0, 1)).astype(F32)
    bias4 = _pair_layout(bias)
    sink4 = _pair_layout(jnp.broadcast_to(attn_sinks[l].astype(F32)[:, None, None], (N_HEADS, CHUNK, 128)))
    w_pg = jnp.zeros((POOL_WIDTH, POOL_WIDTH), F32)
    for g in range(len(POOL_WINDOWS)):
        w_pg = w_pg.at[g * POOL_GW:(g + 1) * POOL_GW, g * POOL_GW:(g + 1) * POOL_GW].set(w_pool_group[l, g])
    w_pg = w_pg.astype(BF16)
    w_in_b = w_in[l].astype(BF16)
    w_pp = w_pool_proj[l].astype(BF16)
    w_ap = w_attn_proj[l].astype(BF16)
    w_out_b = w_out[l].astype(BF16)
    g_mix = g_norm_mix[l].reshape(1, d)
    g_ffn = g_norm_ffn[l].reshape(1, d)
    ps = pool_scale[l].reshape(1, POOL_WIDTH)

    bias_pw = _pair_window_layout(bias)
    sk = attn_sinks[l].astype(F32)
    ones = jnp.ones((2 * CHUNK, LANE), F32)
    sink_pw = jnp.stack([jnp.stack([jnp.concatenate([ones * sk[4 * g + par], ones * sk[4 * g + par + 2]], axis=0)
                                    for par in range(2)]) for g in range(N_KV_HEADS)])
    x1p, nk_p, nv_p, np_p = _mixer_prompt(x_prompt, mod_p, g_mix, w_in_b, w_pg, ps, bias_pw, sink_pw,
                                          w_pp, w_ap, w_out_b)
    ck = cache_k[l].reshape(db, WINDOW, KV_WIDTH)
    cv = cache_v[l].reshape(db, WINDOW, KV_WIDTH)
    sp16 = jnp.pad(state_pool[l], ((0, 0), (HIST - POOL_PAD, 0), (0, 0)))
    x1s, nk_s, nv_s, np_s = _mixer_sample(x_sample, mod_s, ck, cv, sp16, g_mix, w_in_b, w_pg, ps,
                                          bias4, sink4, w_pp, w_ap, w_out_b)

    y_p, y_s = _moe_sparse(x1p, x1s, mod_p[:, :, 3 * d:], mod_s[:, :, 3 * d:], g_ffn, g_final.reshape(1, d),
                           w_router[l].T.astype(BF16), router_bias[l].astype(F32).reshape(N_EXPERTS, 1),
                           w_sh_gate[l].astype(BF16), w_sh_up[l].astype(BF16), w_sh_down[l].astype(BF16),
                           w_exp_gate[l], w_exp_up[l], w_exp_down[l])

    kv_shape = lambda n: (1, n, WINDOW, N_KV_HEADS, HEAD_DIM)
    return (y_p, y_s,
            nk_p.reshape(kv_shape(b)), nv_p.reshape(kv_shape(b)),
            np_p[:, HIST - POOL_PAD:, :][None],
            nk_s.reshape(kv_shape(db)), nv_s.reshape(kv_shape(db)),
            np_s[:, HIST - POOL_PAD:, :][None])
```

```python
import functools
import math

import jax
import jax.numpy as jnp
from jax import lax
from jax.experimental import pallas as pl
from jax.experimental.pallas import tpu as pltpu

F32 = jnp.float32
BF16 = jnp.bfloat16

D_MODEL = 1024
CHUNK = 64
EPS = 1e-6
POOL_WIDTH = 512
POOL_WINDOWS = (2, 4, 8, 16)
POOL_GW = 128
POOL_PAD = 15
HEAD_DIM = 64
N_HEADS = 8
N_KV_HEADS = 2
ATTN_WIDTH = 512
KV_WIDTH = 128
WINDOW = 128
BAND = WINDOW + CHUNK
PAIR_KEYS = BAND + CHUNK
NUM_BUCKETS = 32
REL_MAX_DIST = 128
PAST_LEN = 4096
N_EXPERTS = 32
TOP_K = 4
N_GROUPS = 4
TOPK_GROUPS = 2
EXPERTS_PER_GROUP = 8
D_EXPERT = 256
ROUTED_SCALE = 2.5

OFF_U = 0
OFF_Q = OFF_U + POOL_WIDTH
OFF_K = OFF_Q + ATTN_WIDTH
OFF_V = OFF_K + KV_WIDTH
OFF_GP = OFF_V + KV_WIDTH
OFF_GA = OFF_GP + D_MODEL
IN_WIDTH = OFF_GA + D_MODEL

HIST = 16
MIX_ROWS = 512
MERGE_PANEL = 256
VMEM_LIMIT = 56 * 1024 * 1024

GROUP_ROWS = 64
TC = 1024
PAGE = 16
PAGE_SHIFT = 4
TILE_M = 512
PAGES_PER_TILE = TILE_M // PAGE
TILE_SHIFT = 5
LOCAL_ROWS = TOP_K * TC + N_EXPERTS * PAGE
LANE = 128
PLANES = D_MODEL // LANE
PLANE_PAD = 8
HP_STRIDE = TC + PLANE_PAD
LOC_STRIDE = LOCAL_ROWS + PLANE_PAD
COMBINE_PHASES = 4
PHASE_ROWS = TC // COMBINE_PHASES
RUN_COPY = 4
RUN_COPY_SHIFT = 2
Y_STRIDE = PHASE_ROWS + PLANE_PAD


def _dot(a, b):
    return jnp.dot(a, b, preferred_element_type=F32)


def _dot_nt(a, b):
    return lax.dot_general(a, b, (((1,), (1,)), ((), ())), preferred_element_type=F32)


def _norm_mod(x, g, shift, scale):
    ms = jnp.mean(x * x, axis=-1, keepdims=True)
    y = x * lax.rsqrt(ms + EPS) * g
    return y * (1.0 + scale) + shift


def _silu(x):
    return x * jax.nn.sigmoid(x)


def _ada_kernel(c_ref, w_ref, b_ref, o_ref):
    a = _silu(c_ref[...]).astype(BF16)
    o_ref[...] = _dot(a, w_ref[...].astype(BF16)) + b_ref[...]


def _ada(c_all, w_ada, b_ada):
    rows = c_all.shape[0]
    n = w_ada.shape[1]
    tn = 768
    return pl.pallas_call(
        _ada_kernel,
        grid=(n // tn,),
        in_specs=[
            pl.BlockSpec((rows, D_MODEL), lambda j: (0, 0)),
            pl.BlockSpec((D_MODEL, tn), lambda j: (0, j)),
            pl.BlockSpec((1, tn), lambda j: (0, j)),
        ],
        out_specs=pl.BlockSpec((rows, tn), lambda j: (0, j)),
        out_shape=jax.ShapeDtypeStruct((rows, n), F32),
        compiler_params=pltpu.CompilerParams(dimension_semantics=("arbitrary",)),
        name="ada",
    )(c_all, w_ada, b_ada.reshape(1, n))


def _pool_feats(ue, pos0, rows):
    s2 = ue + pltpu.roll(ue, 1, 0)
    s4 = s2 + pltpu.roll(s2, 2, 0)
    s8 = s4 + pltpu.roll(s4, 4, 0)
    s16 = s8 + pltpu.roll(s8, 8, 0)
    pos = pos0 + lax.broadcasted_iota(jnp.int32, (rows, POOL_GW), 0)
    outs = []
    for g, (w, sw) in enumerate(zip(POOL_WINDOWS, (s2, s4, s8, s16))):
        sl = slice(g * POOL_GW, (g + 1) * POOL_GW)
        cnt = jnp.minimum(pos + 1, w).astype(F32)
        outs.append(sw[HIST:, sl] / cnt - ue[HIST:, sl])
    return jnp.concatenate(outs, axis=1)


def _kv_variants(t):
    lane = lax.broadcasted_iota(jnp.int32, t.shape, 1)
    low = lane < HEAD_DIM
    swapped = pltpu.roll(t, HEAD_DIM, 1)
    zero = jnp.zeros_like(t)
    return (jnp.where(low, t, zero).astype(BF16),
            jnp.where(low, zero, swapped).astype(BF16),
            jnp.where(low, swapped, zero).astype(BF16),
            jnp.where(low, zero, t).astype(BF16))


def _merge_out(h, pool_y, attn_y, win_ref, wpp_ref, wap_ref, wout_ref, m_scr):
    rows = h.shape[0]
    pool_b = pool_y.astype(BF16)
    attn_b = attn_y.astype(BF16)
    for n in range(D_MODEL // MERGE_PANEL):
        lo, hi = n * MERGE_PANEL, (n + 1) * MERGE_PANEL
        gp = _dot(h, win_ref[:, OFF_GP + lo:OFF_GP + hi])
        ga = _dot(h, win_ref[:, OFF_GA + lo:OFF_GA + hi])
        pp = _dot(pool_b, wpp_ref[:, lo:hi])
        ap = _dot(attn_b, wap_ref[:, lo:hi])
        m_scr[0:rows, lo:hi] = (jax.nn.sigmoid(gp) * pp + jax.nn.sigmoid(ga) * ap).astype(BF16)
    return _dot(m_scr[0:rows, :], wout_ref[...])


def _mixer_prompt_kernel(x_ref, mod_ref, g_ref, win_ref, wpg_ref, ps_ref, bias_ref, sink_ref,
                         wpp_ref, wap_ref, wout_ref,
                         x1_ref, nk_ref, nv_ref, npool_ref,
                         uext, k0, k1, k2, k3, v0, v1, v2, v3, m_scr, s_scr, p_scr, a_scr):
    ts = MIX_ROWS
    s = pl.program_id(1)
    kext = (k0, k1, k2, k3)
    vext = (v0, v1, v2, v3)

    @pl.when(s == 0)
    def _():
        uext[0:HIST, :] = jnp.zeros((HIST, POOL_WIDTH), F32)
        for r in kext + vext:
            r[0:WINDOW, :] = jnp.zeros((WINDOW, KV_WIDTH), BF16)

    x = x_ref[0]
    shift = mod_ref[0, :, 0:D_MODEL]
    scale = mod_ref[0, :, D_MODEL:2 * D_MODEL]
    gate = mod_ref[0, :, 2 * D_MODEL:3 * D_MODEL]
    h = _norm_mod(x, g_ref[...], shift, scale).astype(BF16)

    u = _dot(h, win_ref[:, OFF_U:OFF_U + POOL_WIDTH])
    q = (_dot(h, win_ref[:, OFF_Q:OFF_Q + ATTN_WIDTH]) * (HEAD_DIM ** -0.5)).astype(BF16)
    k = _dot(h, win_ref[:, OFF_K:OFF_K + KV_WIDTH])
    v = _dot(h, win_ref[:, OFF_V:OFF_V + KV_WIDTH])
    nk_ref[0] = k[ts - WINDOW:ts, :]
    nv_ref[0] = v[ts - WINDOW:ts, :]
    npool_ref[0] = u[ts - HIST:ts, :]

    pos0 = s * ts
    uext[HIST:HIST + ts, :] = u
    p = _pool_feats(uext[...], pos0, ts)
    uext[0:HIST, :] = u[ts - HIST:ts, :]
    pool_y = _dot(p.astype(BF16), wpg_ref[...]) * ps_ref[...]

    for r, t in zip(kext, _kv_variants(k)):
        r[WINDOW:WINDOW + ts, :] = t
    for r, t in zip(vext, _kv_variants(v)):
        r[WINDOW:WINDOW + ts, :] = t
    key_j = lax.broadcasted_iota(jnp.int32, (1, PAIR_KEYS), 1)
    pair_rows = 2 * CHUNK
    n_pairs = ts // pair_rows
    for pr in range(n_pairs):
        lo = pr * pair_rows
        valid = (pos0 + lo + key_j) >= WINDOW
        for g in range(N_KV_HEADS):
            qs = jnp.concatenate([q[lo:lo + pair_rows, (2 * g) * LANE:(2 * g + 1) * LANE],
                                  q[lo:lo + pair_rows, (2 * g + 1) * LANE:(2 * g + 2) * LANE]], axis=0)
            for par in range(2):
                s = _dot_nt(qs, kext[2 * g + par][lo:lo + PAIR_KEYS, :]) + bias_ref[g, par]
                s_scr[(pr * N_KV_HEADS + g) * 2 + par] = jnp.where(valid, s, -1e30)
    per_pair = N_KV_HEADS * 2
    blk_rows = 2 * pair_rows
    sink = sink_ref[...].reshape(per_pair * blk_rows, LANE)[:, 0:1]
    for pr in range(n_pairs):
        s = s_scr[pr * per_pair:(pr + 1) * per_pair].reshape(per_pair * blk_rows, PAIR_KEYS)
        m = jnp.maximum(jnp.max(s, axis=-1, keepdims=True), sink)
        e = jnp.exp(s - m)
        esum = _dot(e.astype(BF16), jnp.ones((PAIR_KEYS, LANE), BF16))
        denom = esum + jnp.exp(sink - m)
        p_scr[pr * per_pair:(pr + 1) * per_pair] = (
            (e / jnp.concatenate([denom, denom], axis=1)).astype(BF16).reshape(per_pair, blk_rows, PAIR_KEYS))
    for pr in range(n_pairs):
        lo = pr * pair_rows
        for g in range(N_KV_HEADS):
            blk = (pr * N_KV_HEADS + g) * 2
            o = (_dot(p_scr[blk], vext[2 * g][lo:lo + PAIR_KEYS, :])
                 + _dot(p_scr[blk + 1], vext[2 * g + 1][lo:lo + PAIR_KEYS, :]))
            a_scr[lo:lo + pair_rows, (2 * g) * LANE:(2 * g + 1) * LANE] = o[0:pair_rows].astype(BF16)
            a_scr[lo:lo + pair_rows, (2 * g + 1) * LANE:(2 * g + 2) * LANE] = o[pair_rows:].astype(BF16)
    for r in kext + vext:
        r[0:WINDOW, :] = r[ts:ts + WINDOW, :]

    y = _merge_out(h, pool_y, a_scr[...], win_ref, wpp_ref, wap_ref, wout_ref, m_scr)
    x1_ref[0] = x + gate * y


def _const_spec(shape):
    nd = len(shape)
    return pl.BlockSpec(shape, lambda *_: (0,) * nd)


def _mixer_prompt(x, mod, g_mix, w_in, w_pg, pool_scale, bias4, sink4, w_pp, w_ap, w_out):
    b, s_len, d = x.shape
    ts = MIX_ROWS
    pair_rows = 2 * CHUNK
    n_blocks = (ts // pair_rows) * N_KV_HEADS * 2
    kv_scratch = [pltpu.VMEM((WINDOW + ts, KV_WIDTH), BF16) for _ in range(8)]
    return pl.pallas_call(
        _mixer_prompt_kernel,
        grid=(b, s_len // ts),
        in_specs=[
            pl.BlockSpec((1, ts, d), lambda i, j: (i, j, 0)),
            pl.BlockSpec((1, 1, 6 * d), lambda i, j: (i, 0, 0)),
            _const_spec((1, d)),
            _const_spec((d, IN_WIDTH)),
            _const_spec((POOL_WIDTH, POOL_WIDTH)),
            _const_spec((1, POOL_WIDTH)),
            _const_spec(bias4.shape),
            _const_spec(sink4.shape),
            _const_spec((POOL_WIDTH, d)),
            _const_spec((ATTN_WIDTH, d)),
            _const_spec((d, d)),
        ],
        out_specs=[
            pl.BlockSpec((1, ts, d), lambda i, j: (i, j, 0)),
            pl.BlockSpec((1, WINDOW, KV_WIDTH), lambda i, j: (i, 0, 0)),
            pl.BlockSpec((1, WINDOW, KV_WIDTH), lambda i, j: (i, 0, 0)),
            pl.BlockSpec((1, HIST, POOL_WIDTH), lambda i, j: (i, 0, 0)),
        ],
        out_shape=[
            jax.ShapeDtypeStruct((b, s_len, d), F32),
            jax.ShapeDtypeStruct((b, WINDOW, KV_WIDTH), F32),
            jax.ShapeDtypeStruct((b, WINDOW, KV_WIDTH), F32),
            jax.ShapeDtypeStruct((b, HIST, POOL_WIDTH), F32),
        ],
        scratch_shapes=[pltpu.VMEM((HIST + ts, POOL_WIDTH), F32)] + kv_scratch
                       + [pltpu.VMEM((ts, d), BF16),
                          pltpu.VMEM((n_blocks, 2 * pair_rows, PAIR_KEYS), F32),
                          pltpu.VMEM((n_blocks, 2 * pair_rows, PAIR_KEYS), BF16),
                          pltpu.VMEM((ts, ATTN_WIDTH), BF16)],
        compiler_params=pltpu.CompilerParams(
            dimension_semantics=("arbitrary", "arbitrary"), vmem_limit_bytes=VMEM_LIMIT),
        name="mixer_prompt",
    )(x, mod, g_mix, w_in, w_pg, pool_scale, bias4, sink4, w_pp, w_ap, w_out)


def _mixer_sample_kernel(x_ref, mod_ref, ck_ref, cv_ref, sp_ref, g_ref, win_ref, wpg_ref, ps_ref,
                         bias_ref, sink_ref, wpp_ref, wap_ref, wout_ref,
                         x1_ref, nk_ref, nv_ref, npool_ref, m_scr, ks, vs, s_scr, p_scr, a_scr):
    nseq, ds, d = x_ref.shape
    rows = nseq * ds
    x = x_ref[...]
    shift = mod_ref[:, :, 0:D_MODEL]
    scale = mod_ref[:, :, D_MODEL:2 * D_MODEL]
    gate = mod_ref[:, :, 2 * D_MODEL:3 * D_MODEL]
    h = _norm_mod(x, g_ref[...], shift, scale).reshape(rows, d).astype(BF16)

    u = _dot(h, win_ref[:, OFF_U:OFF_U + POOL_WIDTH])
    q = (_dot(h, win_ref[:, OFF_Q:OFF_Q + ATTN_WIDTH]) * (HEAD_DIM ** -0.5)).astype(BF16)
    k = _dot(h, win_ref[:, OFF_K:OFF_K + KV_WIDTH])
    v = _dot(h, win_ref[:, OFF_V:OFF_V + KV_WIDTH])

    pool_parts = []
    key_pad = jnp.zeros((PAIR_KEYS - BAND, KV_WIDTH), F32)
    for b in range(nseq):
        lo = b * ds
        ub = u[lo:lo + ds, :]
        kb = k[lo:lo + ds, :]
        vb = v[lo:lo + ds, :]
        ck = ck_ref[b]
        cv = cv_ref[b]
        nk_ref[b] = jnp.concatenate([ck[ds:WINDOW, :], kb], axis=0)
        nv_ref[b] = jnp.concatenate([cv[ds:WINDOW, :], vb], axis=0)
        npool_ref[b] = ub[ds - HIST:ds, :]
        ue = jnp.concatenate([sp_ref[b], ub], axis=0)
        pool_parts.append(_pool_feats(ue, PAST_LEN, ds))
        for var, t in enumerate(_kv_variants(jnp.concatenate([ck, kb, key_pad], axis=0))):
            ks[var, b] = t
        for var, t in enumerate(_kv_variants(jnp.concatenate([cv, vb, key_pad], axis=0))):
            vs[var, b] = t
    p = jnp.concatenate(pool_parts, axis=0)
    pool_y = _dot(p.astype(BF16), wpg_ref[...]) * ps_ref[...]

    per_seq = N_KV_HEADS * 2
    for b in range(nseq):
        lo = b * ds
        for g in range(N_KV_HEADS):
            qs = jnp.concatenate([q[lo:lo + ds, (2 * g) * LANE:(2 * g + 1) * LANE],
                                  q[lo:lo + ds, (2 * g + 1) * LANE:(2 * g + 2) * LANE]], axis=0)
            for par in range(2):
                s_scr[b * per_seq + 2 * g + par] = _dot_nt(qs, ks[2 * g + par, b]) + bias_ref[g, par]
    group = 2
    grp_rows = group * per_seq * 2 * ds
    sink1 = sink_ref[...].reshape(per_seq * 2 * ds, LANE)[:, 0:1]
    sink = jnp.concatenate([sink1] * group, axis=0)
    for gi in range(nseq // group):
        blks = slice(gi * group * per_seq, (gi + 1) * group * per_seq)
        s = s_scr[blks].reshape(grp_rows, PAIR_KEYS)
        m = jnp.maximum(jnp.max(s, axis=-1, keepdims=True), sink)
        e = jnp.exp(s - m)
        esum = _dot(e.astype(BF16), jnp.ones((PAIR_KEYS, LANE), BF16))
        denom = esum + jnp.exp(sink - m)
        p_scr[blks] = ((e / jnp.concatenate([denom, denom], axis=1)).astype(BF16)
                       .reshape(group * per_seq, 2 * ds, PAIR_KEYS))
    for b in range(nseq):
        lo = b * ds
        for g in range(N_KV_HEADS):
            blk = b * per_seq + 2 * g
            o = _dot(p_scr[blk], vs[2 * g, b]) + _dot(p_scr[blk + 1], vs[2 * g + 1, b])
            a_scr[lo:lo + ds, (2 * g) * LANE:(2 * g + 1) * LANE] = o[0:ds].astype(BF16)
            a_scr[lo:lo + ds, (2 * g + 1) * LANE:(2 * g + 2) * LANE] = o[ds:].astype(BF16)

    y = _merge_out(h, pool_y, a_scr[...], win_ref, wpp_ref, wap_ref, wout_ref, m_scr)
    x1_ref[...] = x + gate * y.reshape(nseq, ds, d)


def _mixer_sample(x, mod, ck, cv, sp16, g_mix, w_in, w_pg, pool_scale, bias4, sink4, w_pp, w_ap, w_out):
    db, ds, d = x.shape
    nseq = MIX_ROWS // ds
    seq_spec = lambda shape: pl.BlockSpec((nseq,) + shape, lambda i: (i, 0, 0))
    return pl.pallas_call(
        _mixer_sample_kernel,
        grid=(db // nseq,),
        in_specs=[
            seq_spec((ds, d)),
            seq_spec((1, 6 * d)),
            seq_spec((WINDOW, KV_WIDTH)),
            seq_spec((WINDOW, KV_WIDTH)),
            seq_spec((HIST, POOL_WIDTH)),
            _const_spec((1, d)),
            _const_spec((d, IN_WIDTH)),
            _const_spec((POOL_WIDTH, POOL_WIDTH)),
            _const_spec((1, POOL_WIDTH)),
            _const_spec(bias4.shape),
            _const_spec(sink4.shape),
            _const_spec((POOL_WIDTH, d)),
            _const_spec((ATTN_WIDTH, d)),
            _const_spec((d, d)),
        ],
        out_specs=[
            seq_spec((ds, d)),
            seq_spec((WINDOW, KV_WIDTH)),
            seq_spec((WINDOW, KV_WIDTH)),
            seq_spec((HIST, POOL_WIDTH)),
        ],
        out_shape=[
            jax.ShapeDtypeStruct((db, ds, d), F32),
            jax.ShapeDtypeStruct((db, WINDOW, KV_WIDTH), F32),
            jax.ShapeDtypeStruct((db, WINDOW, KV_WIDTH), F32),
            jax.ShapeDtypeStruct((db, HIST, POOL_WIDTH), F32),
        ],
        scratch_shapes=[pltpu.VMEM((MIX_ROWS, d), BF16),
                        pltpu.VMEM((4, nseq, PAIR_KEYS, KV_WIDTH), BF16),
                        pltpu.VMEM((4, nseq, PAIR_KEYS, KV_WIDTH), BF16),
                        pltpu.VMEM((nseq * N_KV_HEADS * 2, 2 * ds, PAIR_KEYS), F32),
                        pltpu.VMEM((nseq * N_KV_HEADS * 2, 2 * ds, PAIR_KEYS), BF16),
                        pltpu.VMEM((MIX_ROWS, ATTN_WIDTH), BF16)],
        compiler_params=pltpu.CompilerParams(
            dimension_semantics=("arbitrary",), vmem_limit_bytes=VMEM_LIMIT),
        name="mixer_sample",
    )(x, mod, ck, cv, sp16, g_mix, w_in, w_pg, pool_scale, bias4, sink4, w_pp, w_ap, w_out)


def _first_argmax(vals, idx, size):
    m = jnp.max(vals, axis=0, keepdims=True)
    first = jnp.min(jnp.where(vals == m, idx, size), axis=0, keepdims=True)
    return m, first


def _router_picks(h, wr_ref, rb_ref):
    t = h.shape[0]
    logits = _dot_nt(wr_ref[...], h)
    scores = jax.nn.sigmoid(logits)
    sel = scores + rb_ref[...]
    neg = jnp.float32(-jnp.inf)
    eidx = lax.broadcasted_iota(jnp.int32, (N_EXPERTS, t), 0)

    gscores = []
    bidx = lax.broadcasted_iota(jnp.int32, (EXPERTS_PER_GROUP, t), 0)
    for g in range(N_GROUPS):
        blk = sel[g * EXPERTS_PER_GROUP:(g + 1) * EXPERTS_PER_GROUP, :]
        m1, i1 = _first_argmax(blk, bidx, EXPERTS_PER_GROUP)
        m2 = jnp.max(jnp.where(bidx == i1, neg, blk), axis=0, keepdims=True)
        gscores.append(m1 + m2)
    gs = jnp.concatenate(gscores, axis=0)
    gidx = lax.broadcasted_iota(jnp.int32, (N_GROUPS, t), 0)
    _, g1 = _first_argmax(gs, gidx, N_GROUPS)
    _, g2 = _first_argmax(jnp.where(gidx == g1, neg, gs), gidx, N_GROUPS)
    egrp = eidx // EXPERTS_PER_GROUP
    cand = jnp.where((egrp == g1) | (egrp == g2), sel, neg)

    picks = []
    for _ in range(TOP_K):
        _, i = _first_argmax(cand, eidx, N_EXPERTS)
        hit = eidx == i
        picks.append(hit)
        cand = jnp.where(hit, neg, cand)
    return picks, scores


def _swiglu_rows(h, wg, wu, wd):
    act = _silu(_dot(h, wg)) * _dot(h, wu)
    return _dot(act.astype(BF16), wd)


def _route_kernel(xp_ref, xs_ref, modp_ref, mods_ref, g_ref, wr_ref, rb_ref, tri_ref,
                  h_ref, pos_ref, w_ref, cnt_ref, *, n_prompt_chunks):
    is_prompt = pl.program_id(0) < n_prompt_chunks
    x = jnp.where(is_prompt, xp_ref[...], xs_ref[...])
    shift = jnp.where(is_prompt, modp_ref[:, :, 0:D_MODEL], mods_ref[:, :, 0:D_MODEL])
    scale = jnp.where(is_prompt, modp_ref[:, :, D_MODEL:2 * D_MODEL], mods_ref[:, :, D_MODEL:2 * D_MODEL])
    h = _norm_mod(x, g_ref[...], shift, scale).reshape(TC, D_MODEL).astype(BF16)
    h_ref[...] = h
    picks, scores = _router_picks(h, wr_ref, rb_ref)
    picked = picks[0] | picks[1] | picks[2] | picks[3]
    cnt = _dot(jnp.where(picked, 1.0, 0.0).astype(BF16), tri_ref[...])
    n_e = jnp.max(cnt, axis=1, keepdims=True).astype(jnp.int32)
    n_b = jnp.broadcast_to(n_e, (N_EXPERTS, LANE))
    cnt_ref[0] = n_b
    padded = lax.shift_left(lax.shift_right_logical(n_b + (PAGE - 1), PAGE_SHIFT), PAGE_SHIFT)
    row = lax.broadcasted_iota(jnp.int32, (N_EXPERTS, LANE), 0)
    scan = padded
    for s in (1, 2, 4, 8, 16):
        scan = scan + jnp.where(row >= s, pltpu.roll(scan, s, 0), 0)
    base = (scan - padded)[:, 0:1].astype(F32)
    slot = base + cnt - 1.0
    pos = [jnp.sum(jnp.where(p, slot, 0.0), axis=0, keepdims=True) for p in picks]
    wk = [jnp.sum(jnp.where(p, scores, 0.0), axis=0, keepdims=True) for p in picks]
    wsum = wk[0] + wk[1] + wk[2] + wk[3]
    pos_ref[0] = jnp.concatenate(pos, axis=0).astype(jnp.int32)
    w_ref[0] = jnp.concatenate(wk, axis=0) / wsum * ROUTED_SCALE


def _plan_kernel(cnt_ref, te_ref, nt_ref, ebase_ref, cbase_ref, etp_ref, *, n_chunks, nt_max):
    def per_expert(e, tile0):
        def per_chunk(c, acc):
            cbase_ref[c * N_EXPERTS + e] = acc
            return acc + lax.shift_right_logical(cnt_ref[c, e] + (PAGE - 1), PAGE_SHIFT)
        tp = lax.fori_loop(0, n_chunks, per_chunk, 0)
        etp_ref[e] = tp
        ebase_ref[e] = tile0 * PAGES_PER_TILE
        nt = lax.shift_right_logical(tp + (PAGES_PER_TILE - 1), TILE_SHIFT)

        def fill(j, carry):
            te_ref[tile0 + j] = e
            return carry
        lax.fori_loop(0, nt, fill, 0)
        return tile0 + nt
    total = lax.fori_loop(0, N_EXPERTS, per_expert, 0)
    nt_ref[0] = total

    def tail(j, carry):
        te_ref[j] = N_EXPERTS - 1
        return carry
    lax.fori_loop(total, nt_max, tail, 0)


def _page_copy(src, src_page, dst, dst_page, sem, pages=1):
    rows = pages * PAGE
    s = src.at[pl.ds(pl.multiple_of(src_page * PAGE, PAGE), rows), :]
    t = dst.at[pl.ds(pl.multiple_of(dst_page * PAGE, PAGE), rows), :]
    return pltpu.make_async_copy(s, t, sem)


def _for_each_run(c, cnt_ref, ebase_ref, cbase_ref, fn):
    def per_expert(e, carry):
        lpage, nbig, nsmall = carry
        n = cnt_ref[c, e]
        npages = lax.shift_right_logical(n + (PAGE - 1), PAGE_SHIFT)
        fn(n, lpage, ebase_ref[e] + cbase_ref[c * N_EXPERTS + e], npages)
        return (lpage + npages, nbig + lax.shift_right_logical(npages, RUN_COPY_SHIFT),
                nsmall + (npages & (RUN_COPY - 1)))
    return lax.fori_loop(0, N_EXPERTS, per_expert, (0, 0, 0))


def _start_run_copies(src, src_page, dst, dst_page, npages, sem_big, sem_small):
    nbig = lax.shift_right_logical(npages, RUN_COPY_SHIFT)

    def big(j, carry):
        _page_copy(src, src_page + j * RUN_COPY, dst, dst_page + j * RUN_COPY, sem_big, RUN_COPY).start()
        return carry
    lax.fori_loop(0, nbig, big, 0)

    def small(j, carry):
        _page_copy(src, src_page + j, dst, dst_page + j, sem_small).start()
        return carry
    lax.fori_loop(nbig * RUN_COPY, npages, small, 0)


def _wait_copies(src, dst, nbig, nsmall, sem_big, sem_small):
    def big(j, carry):
        _page_copy(src, 0, dst, 0, sem_big, RUN_COPY).wait()
        return carry
    lax.fori_loop(0, nbig, big, 0)

    def small(j, carry):
        _page_copy(src, 0, dst, 0, sem_small).wait()
        return carry
    lax.fori_loop(0, nsmall, small, 0)


def _row_slab(ref, row, stride):
    return ref.at[pl.ds(row, PLANES, stride=stride), :]


def _dispatch_kernel(cnt_ref, ebase_ref, cbase_ref, etp_ref, nt_ref, pos_ref, h_ref, xs_hbm,
                     hpl, loc, locb, zpage, ztile, pend, sem, bsem, tsem, *, n_chunks, nt_max):
    c = pl.program_id(0)

    @pl.when(c == 0)
    def _():
        pend[0] = 0
        pend[1] = 0

    h = h_ref[...]
    for ch in range(PLANES):
        hpl[ch * HP_STRIDE:ch * HP_STRIDE + TC, :] = h[:, ch * LANE:(ch + 1) * LANE].astype(F32)
    zpage[...] = jnp.zeros(zpage.shape, BF16)
    ztile[...] = jnp.zeros(ztile.shape, BF16)
    zero_slab = jnp.zeros((PLANES, LANE), F32)

    def zero_pad_rows(n, lpage, gpage, npages):
        def per_row(r, carry):
            _row_slab(loc, r, LOC_STRIDE)[...] = zero_slab
            return carry
        lax.fori_loop(lpage * PAGE + n, (lpage + npages) * PAGE, per_row, 0)
    _for_each_run(c, cnt_ref, ebase_ref, cbase_ref, zero_pad_rows)

    def scatter(t8, carry):
        for u in range(8):
            t = t8 * 8 + u
            slab = _row_slab(hpl, t, HP_STRIDE)[...]
            for k in range(TOP_K):
                _row_slab(loc, pos_ref[0, 0, k * TC + t], LOC_STRIDE)[...] = slab
        return carry
    lax.fori_loop(0, TC // 8, scatter, 0)

    _wait_copies(locb, xs_hbm, pend[0], pend[1], bsem, sem)
    for ch in range(PLANES):
        locb[:, ch * LANE:(ch + 1) * LANE] = loc[ch * LOC_STRIDE:ch * LOC_STRIDE + LOCAL_ROWS, :].astype(BF16)

    def send_run(n, lpage, gpage, npages):
        _start_run_copies(locb, lpage, xs_hbm, gpage, npages, bsem, sem)
    _, n_big, n_small = _for_each_run(c, cnt_ref, ebase_ref, cbase_ref, send_run)
    pend[0] = n_big
    pend[1] = n_small

    @pl.when(c == n_chunks - 1)
    def _():
        def pad_expert(e, n):
            tp = etp_ref[e]
            full = lax.shift_left(lax.shift_right_logical(tp + (PAGES_PER_TILE - 1), TILE_SHIFT), TILE_SHIFT)

            def per_page(j, carry):
                _page_copy(zpage, 0, xs_hbm, ebase_ref[e] + j, sem).start()
                return carry
            lax.fori_loop(tp, full, per_page, 0)
            return n + full - tp
        n_pad = lax.fori_loop(0, N_EXPERTS, pad_expert, 0)

        def tile_copy(i):
            dst = xs_hbm.at[pl.ds(pl.multiple_of(i * TILE_M, TILE_M), TILE_M), :]
            return pltpu.make_async_copy(ztile, dst, tsem)

        def tail_start(i, carry):
            tile_copy(i).start()
            return carry
        lax.fori_loop(nt_ref[0], nt_max, tail_start, 0)
        _wait_copies(locb, xs_hbm, n_big, n_small + n_pad, bsem, sem)

        def tail_wait(i, carry):
            tile_copy(i).wait()
            return carry
        lax.fori_loop(nt_ref[0], nt_max, tail_wait, 0)


def _expert_kernel(te_ref, nt_ref, xs_ref, wg0_ref, wu0_ref, wd0_ref, wg1_ref, wu1_ref, wd1_ref, o_ref,
                   wgb, wub, wdb):
    i = pl.program_id(0)
    nt = nt_ref[0]
    slots = ((wg0_ref, wu0_ref, wd0_ref), (wg1_ref, wu1_ref, wd1_ref))

    @pl.when(2 * i < nt)
    def _():
        for slot in range(2):
            tile = jnp.minimum(2 * i + slot, nt - 1)
            wg_ref, wu_ref, wd_ref = slots[slot]

            @pl.when((i == 0) | (te_ref[tile] != te_ref[jnp.maximum(2 * i + slot - 2, 0)]))
            def _():
                wgb[slot] = wg_ref[0].astype(BF16)
                wub[slot] = wu_ref[0].astype(BF16)
                wdb[slot] = wd_ref[0].astype(BF16)
        for slot in range(2):
            rows = slice(slot * TILE_M, (slot + 1) * TILE_M)
            o_ref[rows, :] = _swiglu_rows(xs_ref[rows, :], wgb[slot], wub[slot], wdb[slot]).astype(BF16)


def _combine_kernel(cnt_ref, ebase_ref, cbase_ref, pos_ref, w_ref, xp_ref, xs_ref, gp_ref, gs_ref,
                    h_ref, gf_ref, wsg_ref, wsu_ref, wsd_ref, eo_hbm, yp_ref, ys_ref,
                    loc, locb, ypl, pend, sem, bsem, *, n_prompt_chunks, n_chunks):
    c = pl.program_id(0)
    ph = pl.program_id(1)

    def fetch(chunk):
        buf = locb.at[chunk % 2]

        def fetch_run(n, lpage, gpage, npages):
            _start_run_copies(eo_hbm, gpage, buf, lpage, npages, bsem, sem)
        _, n_big, n_small = _for_each_run(chunk, cnt_ref, ebase_ref, cbase_ref, fetch_run)
        pend[0] = n_big
        pend[1] = n_small

    @pl.when((c == 0) & (ph == 0))
    def _():
        locb[...] = jnp.zeros(locb.shape, BF16)
        fetch(c)

    @pl.when(ph == 0)
    def _():
        cur = locb.at[c % 2]
        _wait_copies(eo_hbm, cur, pend[0], pend[1], bsem, sem)

        @pl.when(c + 1 < n_chunks)
        def _():
            fetch(c + 1)
        for ch in range(PLANES):
            loc[ch * LOC_STRIDE:ch * LOC_STRIDE + LOCAL_ROWS, :] = cur[:, ch * LANE:(ch + 1) * LANE].astype(F32)

    shared = _swiglu_rows(h_ref[...], wsg_ref[...], wsu_ref[...], wsd_ref[...])
    t0 = ph * PHASE_ROWS

    def gather(t8, carry):
        for u in range(8):
            t = t8 * 8 + u
            acc = jnp.zeros((PLANES, LANE), F32)
            for k in range(TOP_K):
                idx = k * TC + t0 + t
                acc = acc + w_ref[0, 0, idx] * _row_slab(loc, pos_ref[0, 0, idx], LOC_STRIDE)[...]
            _row_slab(ypl, t, Y_STRIDE)[...] = acc
        return carry
    lax.fori_loop(0, PHASE_ROWS // 8, gather, 0)
    routed = jnp.concatenate([ypl[ch * Y_STRIDE:ch * Y_STRIDE + PHASE_ROWS, :] for ch in range(PLANES)], axis=1)
    f = shared + routed

    def finish(x_ref, gate_ref, y_ref):
        xo = x_ref[...] + gate_ref[...] * f.reshape(x_ref.shape)
        ms = jnp.mean(xo * xo, axis=-1, keepdims=True)
        y_ref[...] = xo * lax.rsqrt(ms + EPS) * gf_ref[...]

    @pl.when(c < n_prompt_chunks)
    def _():
        finish(xp_ref, gp_ref, yp_ref)

    @pl.when(c >= n_prompt_chunks)
    def _():
        finish(xs_ref, gs_ref, ys_ref)


def _smem_spec():
    return pl.BlockSpec(memory_space=pltpu.SMEM)


def _moe_sparse(x1p, x1s, modp, mods, g_ffn, g_final, w_r_t, rb, wsg, wsu, wsd, weg, weu, wed):
    b, s_len, d = x1p.shape
    db, ds, _ = x1s.shape
    r = GROUP_ROWS
    xp = x1p.reshape(b * s_len // r, r, d)
    xs = x1s.reshape(db * ds // r, r, d)
    mods_g = jnp.repeat(mods, ds // r, axis=0) if ds != r else mods
    npc = b * s_len // TC
    nsc = db * ds // TC
    n_chunks = npc + nsc
    n_tok = n_chunks * TC
    cpb = s_len // TC
    gpc = TC // r
    gpp = PHASE_ROWS // r
    max_pages = (TOP_K * n_tok) // PAGE + n_chunks * N_EXPERTS
    nt_max = -(-(max_pages // PAGES_PER_TILE + N_EXPERTS) // 8) * 8
    sorted_rows = nt_max * TILE_M
    arb1 = pltpu.CompilerParams(dimension_semantics=("arbitrary",), vmem_limit_bytes=VMEM_LIMIT)
    arb2 = pltpu.CompilerParams(dimension_semantics=("arbitrary", "arbitrary"), vmem_limit_bytes=VMEM_LIMIT)
    cst = lambda shape: pl.BlockSpec(shape, lambda *_: (0,) * len(shape))
    tri = jnp.triu(jnp.ones((TC, TC), BF16))
    pclamp = lambda i: jnp.minimum(i, npc - 1)
    sclamp = lambda i: jnp.maximum(i - npc, 0)

    h2, pos, wts, cnt = pl.pallas_call(
        functools.partial(_route_kernel, n_prompt_chunks=npc),
        grid=(n_chunks,),
        in_specs=[
            pl.BlockSpec((gpc, r, d), lambda i: (pclamp(i), 0, 0)),
            pl.BlockSpec((gpc, r, d), lambda i: (sclamp(i), 0, 0)),
            pl.BlockSpec((1, 1, 2 * d), lambda i: (pclamp(i) // cpb, 0, 0)),
            pl.BlockSpec((gpc, 1, 2 * d), lambda i: (sclamp(i), 0, 0)),
            cst((1, d)), cst((N_EXPERTS, d)), cst((N_EXPERTS, 1)), cst((TC, TC)),
        ],
        out_specs=[
            pl.BlockSpec((TC, d), lambda i: (i, 0)),
            pl.BlockSpec((1, TOP_K, TC), lambda i: (i, 0, 0)),
            pl.BlockSpec((1, TOP_K, TC), lambda i: (i, 0, 0)),
            pl.BlockSpec((1, N_EXPERTS, LANE), lambda i: (i, 0, 0)),
        ],
        out_shape=[
            jax.ShapeDtypeStruct((n_tok, d), BF16),
            jax.ShapeDtypeStruct((n_chunks, TOP_K, TC), jnp.int32),
            jax.ShapeDtypeStruct((n_chunks, TOP_K, TC), F32),
            jax.ShapeDtypeStruct((n_chunks, N_EXPERTS, LANE), jnp.int32),
        ],
        compiler_params=arb1,
        name="moe_route",
    )(xp, xs, modp[:, :, 0:2 * d], mods_g[:, :, 0:2 * d], g_ffn, w_r_t, rb, tri)
    counts = cnt[:, :, 0]
    pos = pos.reshape(n_chunks, 1, TOP_K * TC)
    wts = wts.reshape(n_chunks, 1, TOP_K * TC)

    te, nt, ebase, cbase, etp = pl.pallas_call(
        functools.partial(_plan_kernel, n_chunks=n_chunks, nt_max=nt_max),
        in_specs=[_smem_spec()],
        out_specs=[_smem_spec()] * 5,
        out_shape=[
            jax.ShapeDtypeStruct((nt_max,), jnp.int32),
            jax.ShapeDtypeStruct((1,), jnp.int32),
            jax.ShapeDtypeStruct((N_EXPERTS,), jnp.int32),
            jax.ShapeDtypeStruct((n_chunks * N_EXPERTS,), jnp.int32),
            jax.ShapeDtypeStruct((N_EXPERTS,), jnp.int32),
        ],
        name="moe_plan",
    )(counts)

    xsort = pl.pallas_call(
        functools.partial(_dispatch_kernel, n_chunks=n_chunks, nt_max=nt_max),
        grid=(n_chunks,),
        in_specs=[_smem_spec(), _smem_spec(), _smem_spec(), _smem_spec(), _smem_spec(),
                  pl.BlockSpec((1, 1, TOP_K * TC), lambda i: (i, 0, 0), memory_space=pltpu.SMEM),
                  pl.BlockSpec((TC, d), lambda i: (i, 0))],
        out_specs=pl.BlockSpec(memory_space=pl.ANY),
        out_shape=jax.ShapeDtypeStruct((sorted_rows, d), BF16),
        scratch_shapes=[pltpu.VMEM((PLANES * HP_STRIDE, LANE), F32),
                        pltpu.VMEM((PLANES * LOC_STRIDE, LANE), F32),
                        pltpu.VMEM((LOCAL_ROWS, d), BF16),
                        pltpu.VMEM((PAGE, d), BF16),
                        pltpu.VMEM((TILE_M, d), BF16),
                        pltpu.SMEM((2,), jnp.int32),
                        pltpu.SemaphoreType.DMA, pltpu.SemaphoreType.DMA, pltpu.SemaphoreType.DMA],
        compiler_params=arb1,
        name="moe_dispatch",
    )(counts, ebase, cbase, etp, nt, pos, h2)

    tile_map = lambda i, te_ref, nt_ref: (jnp.minimum(i, (nt_ref[0] - 1) // 2), 0)

    def w_map(slot):
        return lambda i, te_ref, nt_ref: (te_ref[jnp.minimum(2 * i + slot, nt_ref[0] - 1)], 0, 0)
    w_specs = [pl.BlockSpec(shape, w_map(slot)) for slot in range(2)
               for shape in ((1, d, D_EXPERT), (1, d, D_EXPERT), (1, D_EXPERT, d))]
    eo = pl.pallas_call(
        _expert_kernel,
        grid_spec=pltpu.PrefetchScalarGridSpec(
            num_scalar_prefetch=2,
            grid=(nt_max // 2,),
            in_specs=[pl.BlockSpec((2 * TILE_M, d), tile_map)] + w_specs,
            out_specs=pl.BlockSpec((2 * TILE_M, d), tile_map),
            scratch_shapes=[pltpu.VMEM((2, d, D_EXPERT), BF16), pltpu.VMEM((2, d, D_EXPERT), BF16),
                            pltpu.VMEM((2, D_EXPERT, d), BF16)],
        ),
        out_shape=jax.ShapeDtypeStruct((sorted_rows, d), BF16),
        input_output_aliases={2: 0},
        compiler_params=arb1,
        name="moe_experts",
    )(te, nt, xsort, weg, weu, wed, weg, weu, wed)

    nph = COMBINE_PHASES
    pstep = lambda i, j: jnp.minimum(i * nph + j, npc * nph - 1)
    sstep = lambda i, j: jnp.maximum((i - npc) * nph + j, 0)
    smem_chunk = pl.BlockSpec((1, 1, TOP_K * TC), lambda i, j: (i, 0, 0), memory_space=pltpu.SMEM)
    yp, ys = pl.pallas_call(
        functools.partial(_combine_kernel, n_prompt_chunks=npc, n_chunks=n_chunks),
        grid=(n_chunks, nph),
        in_specs=[_smem_spec(), _smem_spec(), _smem_spec(), smem_chunk, smem_chunk,
                  pl.BlockSpec((gpp, r, d), lambda i, j: (pstep(i, j), 0, 0)),
                  pl.BlockSpec((gpp, r, d), lambda i, j: (sstep(i, j), 0, 0)),
                  pl.BlockSpec((1, 1, d), lambda i, j: (pclamp(i) // cpb, 0, 0)),
                  pl.BlockSpec((gpp, 1, d), lambda i, j: (sstep(i, j), 0, 0)),
                  pl.BlockSpec((PHASE_ROWS, d), lambda i, j: (i * nph + j, 0)),
                  cst((1, d)), cst((d, D_EXPERT)), cst((d, D_EXPERT)), cst((D_EXPERT, d)),
                  pl.BlockSpec(memory_space=pl.ANY)],
        out_specs=[pl.BlockSpec((gpp, r, d), lambda i, j: (pstep(i, j), 0, 0)),
                   pl.BlockSpec((gpp, r, d), lambda i, j: (sstep(i, j), 0, 0))],
        out_shape=[jax.ShapeDtypeStruct(xp.shape, F32), jax.ShapeDtypeStruct(xs.shape, F32)],
        scratch_shapes=[pltpu.VMEM((PLANES * LOC_STRIDE, LANE), F32),
                        pltpu.VMEM((2, LOCAL_ROWS, d), BF16),
                        pltpu.VMEM((PLANES * Y_STRIDE, LANE), F32),
                        pltpu.SMEM((2,), jnp.int32),
                        pltpu.SemaphoreType.DMA, pltpu.SemaphoreType.DMA],
        compiler_params=arb2,
        name="moe_combine",
    )(counts, ebase, cbase, pos, wts, xp, xs, modp[:, :, 2 * d:3 * d], mods_g[:, :, 2 * d:3 * d],
      h2, g_final, wsg, wsu, wsd, eo)
    return yp.reshape(b, s_len, d), ys.reshape(db, ds, d)


def _t5_bucket(rel):
    nb = NUM_BUCKETS // 2
    max_exact = nb // 2
    ret = jnp.where(rel > 0, nb, 0)
    n = jnp.abs(rel)
    nf = jnp.maximum(n, 1).astype(F32)
    large = max_exact + (jnp.log(nf / max_exact) / math.log(REL_MAX_DIST / max_exact)
                         * (nb - max_exact)).astype(jnp.int32)
    large = jnp.minimum(large, nb - 1)
    return ret + jnp.where(n < max_exact, n, large)


def _pair_layout(per_head):
    groups = []
    for g in range(N_KV_HEADS):
        pars = []
        for par in range(2):
            pars.append(jnp.concatenate([per_head[4 * g + par], per_head[4 * g + par + 2]], axis=0))
        groups.append(jnp.stack(pars))
    return jnp.stack(groups)


REL_SPAN = 2 * LANE


def _rel_bias_kernel(tab_ref, o_ref):
    tab = tab_ref[...]
    for i in range(CHUNK):
        shift = CHUNK - 1 - i
        o_ref[i] = pltpu.roll(tab, (REL_SPAN - shift) % REL_SPAN, 1)[:, 0:BAND]


def _rel_bias(rel_table):
    rel = jnp.arange(REL_SPAN) - (WINDOW + CHUNK - 1)
    tab = rel_table[_t5_bucket(rel)].astype(F32).T
    out = pl.pallas_call(
        _rel_bias_kernel,
        in_specs=[pl.BlockSpec((N_HEADS, REL_SPAN), lambda: (0, 0))],
        out_specs=pl.BlockSpec((CHUNK, N_HEADS, BAND), lambda: (0, 0, 0)),
        out_shape=jax.ShapeDtypeStruct((CHUNK, N_HEADS, BAND), F32),
        name="rel_bias",
    )(tab)
    return jnp.transpose(out, (1, 0, 2))


def _pair_window_layout(bias):
    off = jnp.full((CHUNK, CHUNK), -1e30, F32)
    groups = []
    for g in range(N_KV_HEADS):
        pars = []
        for par in range(2):
            rows = []
            for hp in range(2):
                head = 4 * g + par + 2 * hp
                rows.append(jnp.concatenate([bias[head], off], axis=1))
                rows.append(jnp.concatenate([off, bias[head]], axis=1))
            pars.append(jnp.concatenate(rows, axis=0))
        groups.append(jnp.stack(pars))
    return jnp.stack(groups)


def kernel(x_prompt, x_sample, c_prompt, c_sample, cache_k, cache_v, state_pool, w_ada, b_ada, g_norm_mix, g_norm_ffn, w_in, w_pool_group, pool_scale, attn_sinks, w_pool_proj, w_attn_proj, w_out, rel_table, w_router, router_bias, w_exp_gate, w_exp_up, w_exp_down, w_sh_gate, w_sh_up, w_sh_down, g_final):
    b, s_len, d = x_prompt.shape
    db, ds, _ = x_sample.shape
    l = 0

    c_all = jnp.concatenate([c_prompt, c_sample, jnp.zeros((4, d), F32)], axis=0)
    mods = _ada(c_all, w_ada[l], b_ada[l])
    mod_p = mods[0:b].reshape(b, 1, 6 * d)
    mod_s = mods[b:b + db].reshape(db, 1, 6 * d)

    bias = _rel_bias(rel_table)
    bias4 = jnp.pad(_pair_layout(bias), ((0, 0), (0, 0), (0, 0), (0, PAIR_KEYS - BAND)), constant_values=-1e30)
    sink4 = _pair_layout(jnp.broadcast_to(attn_sinks[l].astype(F32)[:, None, None], (N_HEADS, CHUNK, 128)))
    w_pg = jnp.zeros((POOL_WIDTH, POOL_WIDTH), F32)
    for g in range(len(POOL_WINDOWS)):
        w_pg = w_pg.at[g * POOL_GW:(g + 1) * POOL_GW, g * POOL_GW:(g + 1) * POOL_GW].set(w_pool_group[l, g])
    w_pg = w_pg.astype(BF16)
    w_in_b = w_in[l].astype(BF16)
    w_pp = w_pool_proj[l].astype(BF16)
    w_ap = w_attn_proj[l].astype(BF16)
    w_out_b = w_out[l].astype(BF16)
    g_mix = g_norm_mix[l].reshape(1, d)
    g_ffn = g_norm_ffn[l].reshape(1, d)
    ps = pool_scale[l].reshape(1, POOL_WIDTH)

    bias_pw = _pair_window_layout(bias)
    sk = attn_sinks[l].astype(F32)
    ones = jnp.ones((2 * CHUNK, LANE), F32)
    sink_pw = jnp.stack([jnp.stack([jnp.concatenate([ones * sk[4 * g + par], ones * sk[4 * g + par + 2]], axis=0)
                                    for par in range(2)]) for g in range(N_KV_HEADS)])
    x1p, nk_p, nv_p, np_p = _mixer_prompt(x_prompt, mod_p, g_mix, w_in_b, w_pg, ps, bias_pw, sink_pw,
                                          w_pp, w_ap, w_out_b)
    ck = cache_k[l].reshape(db, WINDOW, KV_WIDTH)
    cv = cache_v[l].reshape(db, WINDOW, KV_WIDTH)
    sp16 = jnp.pad(state_pool[l], ((0, 0), (HIST - POOL_PAD, 0), (0, 0)))
    x1s, nk_s, nv_s, np_s = _mixer_sample(x_sample, mod_s, ck, cv, sp16, g_mix, w_in_b, w_pg, ps,
                                          bias4, sink4, w_pp, w_ap, w_out_b)

    y_p, y_s = _moe_sparse(x1p, x1s, mod_p[:, :, 3 * d:], mod_s[:, :, 3 * d:], g_ffn, g_final.reshape(1, d),
                           w_router[l].T.astype(BF16), router_bias[l].astype(F32).reshape(N_EXPERTS, 1),
                           w_sh_gate[l].astype(BF16), w_sh_up[l].astype(BF16), w_sh_down[l].astype(BF16),
                           w_exp_gate[l], w_exp_up[l], w_exp_down[l])

    kv_shape = lambda n: (1, n, WINDOW, N_KV_HEADS, HEAD_DIM)
    return (y_p, y_s,
            nk_p.reshape(kv_shape(b)), nv_p.reshape(kv_shape(b)),
            np_p[:, HIST - POOL_PAD:, :][None],
            nk_s.reshape(kv_shape(db)), nv_s.reshape(kv_shape(db)),
            np_s[:, HIST - POOL_PAD:, :][None])
```

```python
import functools
import math

import jax
import jax.numpy as jnp
from jax import lax
from jax.experimental import pallas as pl
from jax.experimental.pallas import tpu as pltpu

F32 = jnp.float32
BF16 = jnp.bfloat16

D_MODEL = 1024
CHUNK = 64
EPS = 1e-6
POOL_WIDTH = 512
POOL_WINDOWS = (2, 4, 8, 16)
POOL_GW = 128
POOL_PAD = 15
HEAD_DIM = 64
N_HEADS = 8
N_KV_HEADS = 2
ATTN_WIDTH = 512
KV_WIDTH = 128
WINDOW = 128
BAND = WINDOW + CHUNK
PAIR_KEYS = BAND + CHUNK
NUM_BUCKETS = 32
REL_MAX_DIST = 128
PAST_LEN = 4096
N_EXPERTS = 32
TOP_K = 4
N_GROUPS = 4
TOPK_GROUPS = 2
EXPERTS_PER_GROUP = 8
D_EXPERT = 256
ROUTED_SCALE = 2.5

OFF_U = 0
OFF_Q = OFF_U + POOL_WIDTH
OFF_K = OFF_Q + ATTN_WIDTH
OFF_V = OFF_K + KV_WIDTH
OFF_GP = OFF_V + KV_WIDTH
OFF_GA = OFF_GP + D_MODEL
IN_WIDTH = OFF_GA + D_MODEL

HIST = 16
MIX_ROWS = 512
MERGE_PANEL = 256
VMEM_LIMIT = 56 * 1024 * 1024

GROUP_ROWS = 64
TC = 1024
PAGE = 16
PAGE_SHIFT = 4
TILE_M = 512
PAGES_PER_TILE = TILE_M // PAGE
TILE_SHIFT = 5
LOCAL_ROWS = TOP_K * TC + N_EXPERTS * PAGE
LANE = 128
PLANES = D_MODEL // LANE
PLANE_PAD = 8
HP_STRIDE = TC + PLANE_PAD
LOC_STRIDE = LOCAL_ROWS + PLANE_PAD
COMBINE_PHASES = 4
PHASE_ROWS = TC // COMBINE_PHASES
RUN_COPY = 4
RUN_COPY_SHIFT = 2
Y_STRIDE = PHASE_ROWS + PLANE_PAD


def _dot(a, b):
    return jnp.dot(a, b, preferred_element_type=F32)


def _dot_nt(a, b):
    return lax.dot_general(a, b, (((1,), (1,)), ((), ())), preferred_element_type=F32)


def _norm_mod(x, g, shift, scale):
    ms = jnp.mean(x * x, axis=-1, keepdims=True)
    y = x * lax.rsqrt(ms + EPS) * g
    return y * (1.0 + scale) + shift


def _silu(x):
    return x * jax.nn.sigmoid(x)


def _ada_kernel(c_ref, w_ref, b_ref, o_ref):
    a = _silu(c_ref[...]).astype(BF16)
    o_ref[...] = _dot(a, w_ref[...].astype(BF16)) + b_ref[...]


def _ada(c_all, w_ada, b_ada):
    rows = c_all.shape[0]
    n = w_ada.shape[1]
    tn = 768
    return pl.pallas_call(
        _ada_kernel,
        grid=(n // tn,),
        in_specs=[
            pl.BlockSpec((rows, D_MODEL), lambda j: (0, 0)),
            pl.BlockSpec((D_MODEL, tn), lambda j: (0, j)),
            pl.BlockSpec((1, tn), lambda j: (0, j)),
        ],
        out_specs=pl.BlockSpec((rows, tn), lambda j: (0, j)),
        out_shape=jax.ShapeDtypeStruct((rows, n), F32),
        compiler_params=pltpu.CompilerParams(dimension_semantics=("arbitrary",)),
        name="ada",
    )(c_all, w_ada, b_ada.reshape(1, n))


def _pool_feats(ue, pos0, rows):
    s2 = ue + pltpu.roll(ue, 1, 0)
    s4 = s2 + pltpu.roll(s2, 2, 0)
    s8 = s4 + pltpu.roll(s4, 4, 0)
    s16 = s8 + pltpu.roll(s8, 8, 0)
    pos = pos0 + lax.broadcasted_iota(jnp.int32, (rows, POOL_GW), 0)
    outs = []
    for g, (w, sw) in enumerate(zip(POOL_WINDOWS, (s2, s4, s8, s16))):
        sl = slice(g * POOL_GW, (g + 1) * POOL_GW)
        cnt = jnp.minimum(pos + 1, w).astype(F32)
        outs.append(sw[HIST:, sl] / cnt - ue[HIST:, sl])
    return jnp.concatenate(outs, axis=1)


def _kv_variants(t):
    lane = lax.broadcasted_iota(jnp.int32, t.shape, 1)
    low = lane < HEAD_DIM
    swapped = pltpu.roll(t, HEAD_DIM, 1)
    zero = jnp.zeros_like(t)
    return (jnp.where(low, t, zero).astype(BF16),
            jnp.where(low, zero, swapped).astype(BF16),
            jnp.where(low, swapped, zero).astype(BF16),
            jnp.where(low, zero, t).astype(BF16))


def _merge_out(h, pool_y, attn_y, win_ref, wpp_ref, wap_ref, wout_ref, m_scr):
    rows = h.shape[0]
    pool_b = pool_y.astype(BF16)
    attn_b = attn_y.astype(BF16)
    for n in range(D_MODEL // MERGE_PANEL):
        lo, hi = n * MERGE_PANEL, (n + 1) * MERGE_PANEL
        gp = _dot(h, win_ref[:, OFF_GP + lo:OFF_GP + hi])
        ga = _dot(h, win_ref[:, OFF_GA + lo:OFF_GA + hi])
        pp = _dot(pool_b, wpp_ref[:, lo:hi])
        ap = _dot(attn_b, wap_ref[:, lo:hi])
        m_scr[0:rows, lo:hi] = (jax.nn.sigmoid(gp) * pp + jax.nn.sigmoid(ga) * ap).astype(BF16)
    return _dot(m_scr[0:rows, :], wout_ref[...])


def _mixer_prompt_kernel(x_ref, mod_ref, g_ref, win_ref, wpg_ref, ps_ref, bias_ref, sink_ref,
                         wpp_ref, wap_ref, wout_ref,
                         x1_ref, nk_ref, nv_ref, npool_ref,
                         uext, k0, k1, k2, k3, v0, v1, v2, v3, m_scr, s_scr, p_scr, a_scr):
    ts = MIX_ROWS
    s = pl.program_id(1)
    kext = (k0, k1, k2, k3)
    vext = (v0, v1, v2, v3)

    @pl.when(s == 0)
    def _():
        uext[0:HIST, :] = jnp.zeros((HIST, POOL_WIDTH), F32)
        for r in kext + vext:
            r[0:WINDOW, :] = jnp.zeros((WINDOW, KV_WIDTH), BF16)

    x = x_ref[0]
    shift = mod_ref[0, :, 0:D_MODEL]
    scale = mod_ref[0, :, D_MODEL:2 * D_MODEL]
    gate = mod_ref[0, :, 2 * D_MODEL:3 * D_MODEL]
    h = _norm_mod(x, g_ref[...], shift, scale).astype(BF16)

    u = _dot(h, win_ref[:, OFF_U:OFF_U + POOL_WIDTH])
    q = (_dot(h, win_ref[:, OFF_Q:OFF_Q + ATTN_WIDTH]) * (HEAD_DIM ** -0.5)).astype(BF16)
    k = _dot(h, win_ref[:, OFF_K:OFF_K + KV_WIDTH])
    v = _dot(h, win_ref[:, OFF_V:OFF_V + KV_WIDTH])
    nk_ref[0] = k[ts - WINDOW:ts, :]
    nv_ref[0] = v[ts - WINDOW:ts, :]
    npool_ref[0] = u[ts - HIST:ts, :]

    pos0 = s * ts
    uext[HIST:HIST + ts, :] = u
    p = _pool_feats(uext[...], pos0, ts)
    uext[0:HIST, :] = u[ts - HIST:ts, :]
    pool_y = _dot(p.astype(BF16), wpg_ref[...]) * ps_ref[...]

    for r, t in zip(kext, _kv_variants(k)):
        r[WINDOW:WINDOW + ts, :] = t
    for r, t in zip(vext, _kv_variants(v)):
        r[WINDOW:WINDOW + ts, :] = t
    key_j = lax.broadcasted_iota(jnp.int32, (1, PAIR_KEYS), 1)
    pair_rows = 2 * CHUNK
    n_pairs = ts // pair_rows
    for pr in range(n_pairs):
        lo = pr * pair_rows
        valid = (pos0 + lo + key_j) >= WINDOW
        for g in range(N_KV_HEADS):
            qs = jnp.concatenate([q[lo:lo + pair_rows, (2 * g) * LANE:(2 * g + 1) * LANE],
                                  q[lo:lo + pair_rows, (2 * g + 1) * LANE:(2 * g + 2) * LANE]], axis=0)
            for par in range(2):
                s = _dot_nt(qs, kext[2 * g + par][lo:lo + PAIR_KEYS, :]) + bias_ref[g, par]
                s_scr[(pr * N_KV_HEADS + g) * 2 + par] = jnp.where(valid, s, -1e30)
    per_pair = N_KV_HEADS * 2
    blk_rows = 2 * pair_rows
    sink = sink_ref[...].reshape(per_pair * blk_rows, LANE)[:, 0:1]
    for pr in range(n_pairs):
        s = s_scr[pr * per_pair:(pr + 1) * per_pair].reshape(per_pair * blk_rows, PAIR_KEYS)
        m = jnp.maximum(jnp.max(s, axis=-1, keepdims=True), sink)
        e = jnp.exp(s - m)
        esum = _dot(e.astype(BF16), jnp.ones((PAIR_KEYS, LANE), BF16))
        denom = esum + jnp.exp(sink - m)
        p_scr[pr * per_pair:(pr + 1) * per_pair] = (
            (e / jnp.concatenate([denom, denom], axis=1)).astype(BF16).reshape(per_pair, blk_rows, PAIR_KEYS))
    for pr in range(n_pairs):
        lo = pr * pair_rows
        for g in range(N_KV_HEADS):
            blk = (pr * N_KV_HEADS + g) * 2
            o = (_dot(p_scr[blk], vext[2 * g][lo:lo + PAIR_KEYS, :])
                 + _dot(p_scr[blk + 1], vext[2 * g + 1][lo:lo + PAIR_KEYS, :]))
            a_scr[lo:lo + pair_rows, (2 * g) * LANE:(2 * g + 1) * LANE] = o[0:pair_rows].astype(BF16)
            a_scr[lo:lo + pair_rows, (2 * g + 1) * LANE:(2 * g + 2) * LANE] = o[pair_rows:].astype(BF16)
    for r in kext + vext:
        r[0:WINDOW, :] = r[ts:ts + WINDOW, :]

    y = _merge_out(h, pool_y, a_scr[...], win_ref, wpp_ref, wap_ref, wout_ref, m_scr)
    x1_ref[0] = x + gate * y


def _const_spec(shape):
    nd = len(shape)
    return pl.BlockSpec(shape, lambda *_: (0,) * nd)


def _mixer_prompt(x, mod, g_mix, w_in, w_pg, pool_scale, bias4, sink4, w_pp, w_ap, w_out):
    b, s_len, d = x.shape
    ts = MIX_ROWS
    pair_rows = 2 * CHUNK
    n_blocks = (ts // pair_rows) * N_KV_HEADS * 2
    kv_scratch = [pltpu.VMEM((WINDOW + ts, KV_WIDTH), BF16) for _ in range(8)]
    return pl.pallas_call(
        _mixer_prompt_kernel,
        grid=(b, s_len // ts),
        in_specs=[
            pl.BlockSpec((1, ts, d), lambda i, j: (i, j, 0)),
            pl.BlockSpec((1, 1, 6 * d), lambda i, j: (i, 0, 0)),
            _const_spec((1, d)),
            _const_spec((d, IN_WIDTH)),
            _const_spec((POOL_WIDTH, POOL_WIDTH)),
            _const_spec((1, POOL_WIDTH)),
            _const_spec(bias4.shape),
            _const_spec(sink4.shape),
            _const_spec((POOL_WIDTH, d)),
            _const_spec((ATTN_WIDTH, d)),
            _const_spec((d, d)),
        ],
        out_specs=[
            pl.BlockSpec((1, ts, d), lambda i, j: (i, j, 0)),
            pl.BlockSpec((1, WINDOW, KV_WIDTH), lambda i, j: (i, 0, 0)),
            pl.BlockSpec((1, WINDOW, KV_WIDTH), lambda i, j: (i, 0, 0)),
            pl.BlockSpec((1, HIST, POOL_WIDTH), lambda i, j: (i, 0, 0)),
        ],
        out_shape=[
            jax.ShapeDtypeStruct((b, s_len, d), F32),
            jax.ShapeDtypeStruct((b, WINDOW, KV_WIDTH), F32),
            jax.ShapeDtypeStruct((b, WINDOW, KV_WIDTH), F32),
            jax.ShapeDtypeStruct((b, HIST, POOL_WIDTH), F32),
        ],
        scratch_shapes=[pltpu.VMEM((HIST + ts, POOL_WIDTH), F32)] + kv_scratch
                       + [pltpu.VMEM((ts, d), BF16),
                          pltpu.VMEM((n_blocks, 2 * pair_rows, PAIR_KEYS), F32),
                          pltpu.VMEM((n_blocks, 2 * pair_rows, PAIR_KEYS), BF16),
                          pltpu.VMEM((ts, ATTN_WIDTH), BF16)],
        compiler_params=pltpu.CompilerParams(
            dimension_semantics=("arbitrary", "arbitrary"), vmem_limit_bytes=VMEM_LIMIT),
        name="mixer_prompt",
    )(x, mod, g_mix, w_in, w_pg, pool_scale, bias4, sink4, w_pp, w_ap, w_out)


def _mixer_sample_kernel(x_ref, mod_ref, ck_ref, cv_ref, sp_ref, g_ref, win_ref, wpg_ref, ps_ref,
                         bias_ref, sink_ref, wpp_ref, wap_ref, wout_ref,
                         x1_ref, nk_ref, nv_ref, npool_ref, m_scr, ks, vs, s_scr, p_scr, a_scr):
    nseq, ds, d = x_ref.shape
    rows = nseq * ds
    x = x_ref[...]
    shift = mod_ref[:, :, 0:D_MODEL]
    scale = mod_ref[:, :, D_MODEL:2 * D_MODEL]
    gate = mod_ref[:, :, 2 * D_MODEL:3 * D_MODEL]
    h = _norm_mod(x, g_ref[...], shift, scale).reshape(rows, d).astype(BF16)

    u = _dot(h, win_ref[:, OFF_U:OFF_U + POOL_WIDTH])
    q = (_dot(h, win_ref[:, OFF_Q:OFF_Q + ATTN_WIDTH]) * (HEAD_DIM ** -0.5)).astype(BF16)
    k = _dot(h, win_ref[:, OFF_K:OFF_K + KV_WIDTH])
    v = _dot(h, win_ref[:, OFF_V:OFF_V + KV_WIDTH])

    pool_parts = []
    key_pad = jnp.zeros((PAIR_KEYS - BAND, KV_WIDTH), F32)
    for b in range(nseq):
        lo = b * ds
        ub = u[lo:lo + ds, :]
        kb = k[lo:lo + ds, :]
        vb = v[lo:lo + ds, :]
        ck = ck_ref[b]
        cv = cv_ref[b]
        nk_ref[b] = jnp.concatenate([ck[ds:WINDOW, :], kb], axis=0)
        nv_ref[b] = jnp.concatenate([cv[ds:WINDOW, :], vb], axis=0)
        npool_ref[b] = ub[ds - HIST:ds, :]
        ue = jnp.concatenate([sp_ref[b], ub], axis=0)
        pool_parts.append(_pool_feats(ue, PAST_LEN, ds))
        for var, t in enumerate(_kv_variants(jnp.concatenate([ck, kb, key_pad], axis=0))):
            ks[var, b] = t
        for var, t in enumerate(_kv_variants(jnp.concatenate([cv, vb, key_pad], axis=0))):
            vs[var, b] = t
    p = jnp.concatenate(pool_parts, axis=0)
    pool_y = _dot(p.astype(BF16), wpg_ref[...]) * ps_ref[...]

    per_seq = N_KV_HEADS * 2
    for b in range(nseq):
        lo = b * ds
        for g in range(N_KV_HEADS):
            qs = jnp.concatenate([q[lo:lo + ds, (2 * g) * LANE:(2 * g + 1) * LANE],
                                  q[lo:lo + ds, (2 * g + 1) * LANE:(2 * g + 2) * LANE]], axis=0)
            for par in range(2):
                s_scr[b * per_seq + 2 * g + par] = _dot_nt(qs, ks[2 * g + par, b]) + bias_ref[g, par]
    group = 2
    grp_rows = group * per_seq * 2 * ds
    sink1 = sink_ref[...].reshape(per_seq * 2 * ds, LANE)[:, 0:1]
    sink = jnp.concatenate([sink1] * group, axis=0)
    for gi in range(nseq // group):
        blks = slice(gi * group * per_seq, (gi + 1) * group * per_seq)
        s = s_scr[blks].reshape(grp_rows, PAIR_KEYS)
        m = jnp.maximum(jnp.max(s, axis=-1, keepdims=True), sink)
        e = jnp.exp(s - m)
        esum = _dot(e.astype(BF16), jnp.ones((PAIR_KEYS, LANE), BF16))
        denom = esum + jnp.exp(sink - m)
        p_scr[blks] = ((e / jnp.concatenate([denom, denom], axis=1)).astype(BF16)
                       .reshape(group * per_seq, 2 * ds, PAIR_KEYS))
    for b in range(nseq):
        lo = b * ds
        for g in range(N_KV_HEADS):
            blk = b * per_seq + 2 * g
            o = _dot(p_scr[blk], vs[2 * g, b]) + _dot(p_scr[blk + 1], vs[2 * g + 1, b])
            a_scr[lo:lo + ds, (2 * g) * LANE:(2 * g + 1) * LANE] = o[0:ds].astype(BF16)
            a_scr[lo:lo + ds, (2 * g + 1) * LANE:(2 * g + 2) * LANE] = o[ds:].astype(BF16)

    y = _merge_out(h, pool_y, a_scr[...], win_ref, wpp_ref, wap_ref, wout_ref, m_scr)
    x1_ref[...] = x + gate * y.reshape(nseq, ds, d)


def _mixer_sample(x, mod, ck, cv, sp16, g_mix, w_in, w_pg, pool_scale, bias4, sink4, w_pp, w_ap, w_out):
    db, ds, d = x.shape
    nseq = MIX_ROWS // ds
    seq_spec = lambda shape: pl.BlockSpec((nseq,) + shape, lambda i: (i, 0, 0))
    return pl.pallas_call(
        _mixer_sample_kernel,
        grid=(db // nseq,),
        in_specs=[
            seq_spec((ds, d)),
            seq_spec((1, 6 * d)),
            seq_spec((WINDOW, KV_WIDTH)),
            seq_spec((WINDOW, KV_WIDTH)),
            seq_spec((HIST, POOL_WIDTH)),
            _const_spec((1, d)),
            _const_spec((d, IN_WIDTH)),
            _const_spec((POOL_WIDTH, POOL_WIDTH)),
            _const_spec((1, POOL_WIDTH)),
            _const_spec(bias4.shape),
            _const_spec(sink4.shape),
            _const_spec((POOL_WIDTH, d)),
            _const_spec((ATTN_WIDTH, d)),
            _const_spec((d, d)),
        ],
        out_specs=[
            seq_spec((ds, d)),
            seq_spec((WINDOW, KV_WIDTH)),
            seq_spec((WINDOW, KV_WIDTH)),
            seq_spec((HIST, POOL_WIDTH)),
        ],
        out_shape=[
            jax.ShapeDtypeStruct((db, ds, d), F32),
            jax.ShapeDtypeStruct((db, WINDOW, KV_WIDTH), F32),
            jax.ShapeDtypeStruct((db, WINDOW, KV_WIDTH), F32),
            jax.ShapeDtypeStruct((db, HIST, POOL_WIDTH), F32),
        ],
        scratch_shapes=[pltpu.VMEM((MIX_ROWS, d), BF16),
                        pltpu.VMEM((4, nseq, PAIR_KEYS, KV_WIDTH), BF16),
                        pltpu.VMEM((4, nseq, PAIR_KEYS, KV_WIDTH), BF16),
                        pltpu.VMEM((nseq * N_KV_HEADS * 2, 2 * ds, PAIR_KEYS), F32),
                        pltpu.VMEM((nseq * N_KV_HEADS * 2, 2 * ds, PAIR_KEYS), BF16),
                        pltpu.VMEM((MIX_ROWS, ATTN_WIDTH), BF16)],
        compiler_params=pltpu.CompilerParams(
            dimension_semantics=("arbitrary",), vmem_limit_bytes=VMEM_LIMIT),
        name="mixer_sample",
    )(x, mod, ck, cv, sp16, g_mix, w_in, w_pg, pool_scale, bias4, sink4, w_pp, w_ap, w_out)


def _first_argmax(vals, idx, size):
    m = jnp.max(vals, axis=0, keepdims=True)
    first = jnp.min(jnp.where(vals == m, idx, size), axis=0, keepdims=True)
    return m, first


def _router_picks(h, wr_ref, rb_ref):
    t = h.shape[0]
    logits = _dot_nt(wr_ref[...], h)
    scores = jax.nn.sigmoid(logits)
    sel = scores + rb_ref[...]
    neg = jnp.float32(-jnp.inf)
    eidx = lax.broadcasted_iota(jnp.int32, (N_EXPERTS, t), 0)

    gscores = []
    bidx = lax.broadcasted_iota(jnp.int32, (EXPERTS_PER_GROUP, t), 0)
    for g in range(N_GROUPS):
        blk = sel[g * EXPERTS_PER_GROUP:(g + 1) * EXPERTS_PER_GROUP, :]
        m1, i1 = _first_argmax(blk, bidx, EXPERTS_PER_GROUP)
        m2 = jnp.max(jnp.where(bidx == i1, neg, blk), axis=0, keepdims=True)
        gscores.append(m1 + m2)
    gs = jnp.concatenate(gscores, axis=0)
    gidx = lax.broadcasted_iota(jnp.int32, (N_GROUPS, t), 0)
    _, g1 = _first_argmax(gs, gidx, N_GROUPS)
    _, g2 = _first_argmax(jnp.where(gidx == g1, neg, gs), gidx, N_GROUPS)
    egrp = eidx // EXPERTS_PER_GROUP
    cand = jnp.where((egrp == g1) | (egrp == g2), sel, neg)

    picks = []
    for _ in range(TOP_K):
        _, i = _first_argmax(cand, eidx, N_EXPERTS)
        hit = eidx == i
        picks.append(hit)
        cand = jnp.where(hit, neg, cand)
    return picks, scores


def _swiglu_rows(h, wg, wu, wd):
    act = _silu(_dot(h, wg)) * _dot(h, wu)
    return _dot(act.astype(BF16), wd)


def _route_kernel(xp_ref, xs_ref, modp_ref, mods_ref, g_ref, wr_ref, rb_ref, tri_ref,
                  h_ref, pos_ref, w_ref, cnt_ref, *, n_prompt_chunks):
    is_prompt = pl.program_id(0) < n_prompt_chunks
    x = jnp.where(is_prompt, xp_ref[...], xs_ref[...])
    shift = jnp.where(is_prompt, modp_ref[:, :, 0:D_MODEL], mods_ref[:, :, 0:D_MODEL])
    scale = jnp.where(is_prompt, modp_ref[:, :, D_MODEL:2 * D_MODEL], mods_ref[:, :, D_MODEL:2 * D_MODEL])
    h = _norm_mod(x, g_ref[...], shift, scale).reshape(TC, D_MODEL).astype(BF16)
    h_ref[...] = h
    picks, scores = _router_picks(h, wr_ref, rb_ref)
    picked = picks[0] | picks[1] | picks[2] | picks[3]
    cnt = _dot(jnp.where(picked, 1.0, 0.0).astype(BF16), tri_ref[...])
    n_e = jnp.max(cnt, axis=1, keepdims=True).astype(jnp.int32)
    n_b = jnp.broadcast_to(n_e, (N_EXPERTS, LANE))
    cnt_ref[0] = n_b
    padded = lax.shift_left(lax.shift_right_logical(n_b + (PAGE - 1), PAGE_SHIFT), PAGE_SHIFT)
    row = lax.broadcasted_iota(jnp.int32, (N_EXPERTS, LANE), 0)
    scan = padded
    for s in (1, 2, 4, 8, 16):
        scan = scan + jnp.where(row >= s, pltpu.roll(scan, s, 0), 0)
    base = (scan - padded)[:, 0:1].astype(F32)
    slot = base + cnt - 1.0
    pos = [jnp.sum(jnp.where(p, slot, 0.0), axis=0, keepdims=True) for p in picks]
    wk = [jnp.sum(jnp.where(p, scores, 0.0), axis=0, keepdims=True) for p in picks]
    wsum = wk[0] + wk[1] + wk[2] + wk[3]
    pos_ref[0] = jnp.concatenate(pos, axis=0).astype(jnp.int32)
    w_ref[0] = jnp.concatenate(wk, axis=0) / wsum * ROUTED_SCALE


def _plan_kernel(cnt_ref, te_ref, nt_ref, ebase_ref, cbase_ref, etp_ref, slot0_ref, slot1_ref, sel_ref,
                 *, n_chunks, nt_max):
    def per_expert(e, tile0):
        def per_chunk(c, acc):
            cbase_ref[c * N_EXPERTS + e] = acc
            return acc + lax.shift_right_logical(cnt_ref[c, e] + (PAGE - 1), PAGE_SHIFT)
        tp = lax.fori_loop(0, n_chunks, per_chunk, 0)
        etp_ref[e] = tp
        ebase_ref[e] = tile0 * PAGES_PER_TILE
        nt = lax.shift_right_logical(tp + (PAGES_PER_TILE - 1), TILE_SHIFT)

        def fill(j, carry):
            te_ref[tile0 + j] = e
            return carry
        lax.fori_loop(0, nt, fill, 0)
        return tile0 + nt
    total = lax.fori_loop(0, N_EXPERTS, per_expert, 0)
    nt_ref[0] = total

    def tail(j, carry):
        te_ref[j] = N_EXPERTS - 1
        return carry
    lax.fori_loop(total, nt_max, tail, 0)

    def per_step(i, carry):
        c0, c1 = carry
        used = 2 * i < total
        a = te_ref[jnp.minimum(2 * i, total - 1)]
        b = te_ref[jnp.minimum(2 * i + 1, total - 1)]
        hit_a = (a == c0) | (a == c1)
        evict0 = hit_a | (c0 != b)
        n0 = jnp.where(used & ~hit_a & evict0, a, c0)
        n1 = jnp.where(used & ~hit_a & ~evict0, a, c1)
        sa = jnp.where(a == n0, 0, 1)
        hit_b = (b == n0) | (b == n1)
        m0 = jnp.where(used & ~hit_b & (sa == 1), b, n0)
        m1 = jnp.where(used & ~hit_b & (sa == 0), b, n1)
        slot0_ref[i] = jnp.maximum(m0, 0)
        slot1_ref[i] = jnp.maximum(m1, 0)
        sel_ref[2 * i] = sa
        sel_ref[2 * i + 1] = jnp.where(b == m0, 0, 1)
        return (m0, m1)
    lax.fori_loop(0, nt_max // 2, per_step, (-1, -1))


def _page_copy(src, src_page, dst, dst_page, sem, pages=1):
    rows = pages * PAGE
    s = src.at[pl.ds(pl.multiple_of(src_page * PAGE, PAGE), rows), :]
    t = dst.at[pl.ds(pl.multiple_of(dst_page * PAGE, PAGE), rows), :]
    return pltpu.make_async_copy(s, t, sem)


def _for_each_run(c, cnt_ref, ebase_ref, cbase_ref, fn):
    def per_expert(e, carry):
        lpage, nbig, nsmall = carry
        n = cnt_ref[c, e]
        npages = lax.shift_right_logical(n + (PAGE - 1), PAGE_SHIFT)
        fn(n, lpage, ebase_ref[e] + cbase_ref[c * N_EXPERTS + e], npages)
        return (lpage + npages, nbig + lax.shift_right_logical(npages, RUN_COPY_SHIFT),
                nsmall + (npages & (RUN_COPY - 1)))
    return lax.fori_loop(0, N_EXPERTS, per_expert, (0, 0, 0))


def _start_run_copies(src, src_page, dst, dst_page, npages, sem_big, sem_small):
    nbig = lax.shift_right_logical(npages, RUN_COPY_SHIFT)

    def big(j, carry):
        _page_copy(src, src_page + j * RUN_COPY, dst, dst_page + j * RUN_COPY, sem_big, RUN_COPY).start()
        return carry
    lax.fori_loop(0, nbig, big, 0)

    def small(j, carry):
        _page_copy(src, src_page + j, dst, dst_page + j, sem_small).start()
        return carry
    lax.fori_loop(nbig * RUN_COPY, npages, small, 0)


def _wait_copies(src, dst, nbig, nsmall, sem_big, sem_small):
    def big(j, carry):
        _page_copy(src, 0, dst, 0, sem_big, RUN_COPY).wait()
        return carry
    lax.fori_loop(0, nbig, big, 0)

    def small(j, carry):
        _page_copy(src, 0, dst, 0, sem_small).wait()
        return carry
    lax.fori_loop(0, nsmall, small, 0)


def _row_slab(ref, row, stride):
    return ref.at[pl.ds(row, PLANES, stride=stride), :]


def _dispatch_kernel(cnt_ref, ebase_ref, cbase_ref, etp_ref, nt_ref, pos_ref, h_ref, xs_hbm,
                     hpl, loc, locb, zpage, ztile, pend, sem, bsem, tsem, *, n_chunks, nt_max):
    c = pl.program_id(0)

    @pl.when(c == 0)
    def _():
        pend[0] = 0
        pend[1] = 0
        loc[...] = jnp.zeros(loc.shape, F32)

    h = h_ref[...]
    for ch in range(PLANES):
        hpl[ch * HP_STRIDE:ch * HP_STRIDE + TC, :] = h[:, ch * LANE:(ch + 1) * LANE].astype(F32)
    zpage[...] = jnp.zeros(zpage.shape, BF16)
    ztile[...] = jnp.zeros(ztile.shape, BF16)

    def scatter(t8, carry):
        for u in range(8):
            t = t8 * 8 + u
            slab = _row_slab(hpl, t, HP_STRIDE)[...]
            for k in range(TOP_K):
                _row_slab(loc, pos_ref[0, 0, k * TC + t], LOC_STRIDE)[...] = slab
        return carry
    lax.fori_loop(0, TC // 8, scatter, 0)

    _wait_copies(locb, xs_hbm, pend[0], pend[1], bsem, sem)
    for ch in range(PLANES):
        locb[:, ch * LANE:(ch + 1) * LANE] = loc[ch * LOC_STRIDE:ch * LOC_STRIDE + LOCAL_ROWS, :].astype(BF16)

    def send_run(n, lpage, gpage, npages):
        _start_run_copies(locb, lpage, xs_hbm, gpage, npages, bsem, sem)
    _, n_big, n_small = _for_each_run(c, cnt_ref, ebase_ref, cbase_ref, send_run)
    pend[0] = n_big
    pend[1] = n_small

    @pl.when(c == n_chunks - 1)
    def _():
        def pad_expert(e, n):
            tp = etp_ref[e]
            full = lax.shift_left(lax.shift_right_logical(tp + (PAGES_PER_TILE - 1), TILE_SHIFT), TILE_SHIFT)

            def per_page(j, carry):
                _page_copy(zpage, 0, xs_hbm, ebase_ref[e] + j, sem).start()
                return carry
            lax.fori_loop(tp, full, per_page, 0)
            return n + full - tp
        n_pad = lax.fori_loop(0, N_EXPERTS, pad_expert, 0)

        def tile_copy(i):
            dst = xs_hbm.at[pl.ds(pl.multiple_of(i * TILE_M, TILE_M), TILE_M), :]
            return pltpu.make_async_copy(ztile, dst, tsem)

        def tail_start(i, carry):
            tile_copy(i).start()
            return carry
        lax.fori_loop(nt_ref[0], nt_max, tail_start, 0)
        _wait_copies(locb, xs_hbm, n_big, n_small + n_pad, bsem, sem)

        def tail_wait(i, carry):
            tile_copy(i).wait()
            return carry
        lax.fori_loop(nt_ref[0], nt_max, tail_wait, 0)


def _expert_kernel(slot0_ref, slot1_ref, sel_ref, nt_ref, xs_ref, wg0_ref, wu0_ref, wd0_ref,
                   wg1_ref, wu1_ref, wd1_ref, o_ref, wgb, wub, wdb):
    i = pl.program_id(0)
    slots = ((slot0_ref, wg0_ref, wu0_ref, wd0_ref), (slot1_ref, wg1_ref, wu1_ref, wd1_ref))

    @pl.when(2 * i < nt_ref[0])
    def _():
        for slot, (idx_ref, wg_ref, wu_ref, wd_ref) in enumerate(slots):
            @pl.when((i == 0) | (idx_ref[i] != idx_ref[jnp.maximum(i - 1, 0)]))
            def _():
                wgb[slot] = wg_ref[0].astype(BF16)
                wub[slot] = wu_ref[0].astype(BF16)
                wdb[slot] = wd_ref[0].astype(BF16)
        for t in range(2):
            rows = slice(t * TILE_M, (t + 1) * TILE_M)
            sl = sel_ref[2 * i + t]
            o_ref[rows, :] = _swiglu_rows(xs_ref[rows, :], wgb[sl], wub[sl], wdb[sl]).astype(BF16)


def _combine_kernel(cnt_ref, ebase_ref, cbase_ref, pos_ref, w_ref, xp_ref, xs_ref, gp_ref, gs_ref,
                    h_ref, gf_ref, wsg_ref, wsu_ref, wsd_ref, eo_hbm, yp_ref, ys_ref,
                    loc, locb, ypl, pend, sem, bsem, *, n_prompt_chunks, n_chunks):
    c = pl.program_id(0)
    ph = pl.program_id(1)

    def fetch(chunk):
        buf = locb.at[chunk % 2]

        def fetch_run(n, lpage, gpage, npages):
            _start_run_copies(eo_hbm, gpage, buf, lpage, npages, bsem, sem)
        _, n_big, n_small = _for_each_run(chunk, cnt_ref, ebase_ref, cbase_ref, fetch_run)
        pend[0] = n_big
        pend[1] = n_small

    @pl.when((c == 0) & (ph == 0))
    def _():
        locb[...] = jnp.zeros(locb.shape, BF16)
        fetch(c)

    @pl.when(ph == 0)
    def _():
        cur = locb.at[c % 2]
        _wait_copies(eo_hbm, cur, pend[0], pend[1], bsem, sem)

        @pl.when(c + 1 < n_chunks)
        def _():
            fetch(c + 1)
        for ch in range(PLANES):
            loc[ch * LOC_STRIDE:ch * LOC_STRIDE + LOCAL_ROWS, :] = cur[:, ch * LANE:(ch + 1) * LANE].astype(F32)

    shared = _swiglu_rows(h_ref[...], wsg_ref[...], wsu_ref[...], wsd_ref[...])
    t0 = ph * PHASE_ROWS

    def gather(t8, carry):
        for u in range(8):
            t = t8 * 8 + u
            acc = jnp.zeros((PLANES, LANE), F32)
            for k in range(TOP_K):
                idx = k * TC + t0 + t
                acc = acc + w_ref[0, 0, idx] * _row_slab(loc, pos_ref[0, 0, idx], LOC_STRIDE)[...]
            _row_slab(ypl, t, Y_STRIDE)[...] = acc
        return carry
    lax.fori_loop(0, PHASE_ROWS // 8, gather, 0)
    routed = jnp.concatenate([ypl[ch * Y_STRIDE:ch * Y_STRIDE + PHASE_ROWS, :] for ch in range(PLANES)], axis=1)
    f = shared + routed

    def finish(x_ref, gate_ref, y_ref):
        xo = x_ref[...] + gate_ref[...] * f.reshape(x_ref.shape)
        ms = jnp.mean(xo * xo, axis=-1, keepdims=True)
        y_ref[...] = xo * lax.rsqrt(ms + EPS) * gf_ref[...]

    @pl.when(c < n_prompt_chunks)
    def _():
        finish(xp_ref, gp_ref, yp_ref)

    @pl.when(c >= n_prompt_chunks)
    def _():
        finish(xs_ref, gs_ref, ys_ref)


def _smem_spec():
    return pl.BlockSpec(memory_space=pltpu.SMEM)


def _moe_sparse(x1p, x1s, modp, mods, g_ffn, g_final, w_r_t, rb, wsg, wsu, wsd, weg, weu, wed):
    b, s_len, d = x1p.shape
    db, ds, _ = x1s.shape
    r = GROUP_ROWS
    xp = x1p.reshape(b * s_len // r, r, d)
    xs = x1s.reshape(db * ds // r, r, d)
    mods_g = jnp.repeat(mods, ds // r, axis=0) if ds != r else mods
    npc = b * s_len // TC
    nsc = db * ds // TC
    n_chunks = npc + nsc
    n_tok = n_chunks * TC
    cpb = s_len // TC
    gpc = TC // r
    gpp = PHASE_ROWS // r
    max_pages = (TOP_K * n_tok) // PAGE + n_chunks * N_EXPERTS
    nt_max = -(-(max_pages // PAGES_PER_TILE + N_EXPERTS) // 8) * 8
    sorted_rows = nt_max * TILE_M
    arb1 = pltpu.CompilerParams(dimension_semantics=("arbitrary",), vmem_limit_bytes=VMEM_LIMIT)
    arb2 = pltpu.CompilerParams(dimension_semantics=("arbitrary", "arbitrary"), vmem_limit_bytes=VMEM_LIMIT)
    cst = lambda shape: pl.BlockSpec(shape, lambda *_: (0,) * len(shape))
    tri = jnp.triu(jnp.ones((TC, TC), BF16))
    pclamp = lambda i: jnp.minimum(i, npc - 1)
    sclamp = lambda i: jnp.maximum(i - npc, 0)

    h2, pos, wts, cnt = pl.pallas_call(
        functools.partial(_route_kernel, n_prompt_chunks=npc),
        grid=(n_chunks,),
        in_specs=[
            pl.BlockSpec((gpc, r, d), lambda i: (pclamp(i), 0, 0)),
            pl.BlockSpec((gpc, r, d), lambda i: (sclamp(i), 0, 0)),
            pl.BlockSpec((1, 1, 2 * d), lambda i: (pclamp(i) // cpb, 0, 0)),
            pl.BlockSpec((gpc, 1, 2 * d), lambda i: (sclamp(i), 0, 0)),
            cst((1, d)), cst((N_EXPERTS, d)), cst((N_EXPERTS, 1)), cst((TC, TC)),
        ],
        out_specs=[
            pl.BlockSpec((TC, d), lambda i: (i, 0)),
            pl.BlockSpec((1, TOP_K, TC), lambda i: (i, 0, 0)),
            pl.BlockSpec((1, TOP_K, TC), lambda i: (i, 0, 0)),
            pl.BlockSpec((1, N_EXPERTS, LANE), lambda i: (i, 0, 0)),
        ],
        out_shape=[
            jax.ShapeDtypeStruct((n_tok, d), BF16),
            jax.ShapeDtypeStruct((n_chunks, TOP_K, TC), jnp.int32),
            jax.ShapeDtypeStruct((n_chunks, TOP_K, TC), F32),
            jax.ShapeDtypeStruct((n_chunks, N_EXPERTS, LANE), jnp.int32),
        ],
        compiler_params=arb1,
        name="moe_route",
    )(xp, xs, modp[:, :, 0:2 * d], mods_g[:, :, 0:2 * d], g_ffn, w_r_t, rb, tri)
    counts = cnt[:, :, 0]
    pos = pos.reshape(n_chunks, 1, TOP_K * TC)
    wts = wts.reshape(n_chunks, 1, TOP_K * TC)

    te, nt, ebase, cbase, etp, slot0, slot1, sel = pl.pallas_call(
        functools.partial(_plan_kernel, n_chunks=n_chunks, nt_max=nt_max),
        in_specs=[_smem_spec()],
        out_specs=[_smem_spec()] * 8,
        out_shape=[
            jax.ShapeDtypeStruct((nt_max,), jnp.int32),
            jax.ShapeDtypeStruct((1,), jnp.int32),
            jax.ShapeDtypeStruct((N_EXPERTS,), jnp.int32),
            jax.ShapeDtypeStruct((n_chunks * N_EXPERTS,), jnp.int32),
            jax.ShapeDtypeStruct((N_EXPERTS,), jnp.int32),
            jax.ShapeDtypeStruct((nt_max // 2,), jnp.int32),
            jax.ShapeDtypeStruct((nt_max // 2,), jnp.int32),
            jax.ShapeDtypeStruct((nt_max,), jnp.int32),
        ],
        name="moe_plan",
    )(counts)

    xsort = pl.pallas_call(
        functools.partial(_dispatch_kernel, n_chunks=n_chunks, nt_max=nt_max),
        grid=(n_chunks,),
        in_specs=[_smem_spec(), _smem_spec(), _smem_spec(), _smem_spec(), _smem_spec(),
                  pl.BlockSpec((1, 1, TOP_K * TC), lambda i: (i, 0, 0), memory_space=pltpu.SMEM),
                  pl.BlockSpec((TC, d), lambda i: (i, 0))],
        out_specs=pl.BlockSpec(memory_space=pl.ANY),
        out_shape=jax.ShapeDtypeStruct((sorted_rows, d), BF16),
        scratch_shapes=[pltpu.VMEM((PLANES * HP_STRIDE, LANE), F32),
                        pltpu.VMEM((PLANES * LOC_STRIDE, LANE), F32),
                        pltpu.VMEM((LOCAL_ROWS, d), BF16),
                        pltpu.VMEM((PAGE, d), BF16),
                        pltpu.VMEM((TILE_M, d), BF16),
                        pltpu.SMEM((2,), jnp.int32),
                        pltpu.SemaphoreType.DMA, pltpu.SemaphoreType.DMA, pltpu.SemaphoreType.DMA],
        compiler_params=arb1,
        name="moe_dispatch",
    )(counts, ebase, cbase, etp, nt, pos, h2)

    tile_map = lambda i, s0, s1, sl, nt_ref: (jnp.minimum(i, (nt_ref[0] - 1) // 2), 0)
    slot_maps = (lambda i, s0, s1, sl, nt_ref: (s0[i], 0, 0), lambda i, s0, s1, sl, nt_ref: (s1[i], 0, 0))
    w_specs = [pl.BlockSpec(shape, slot_maps[slot]) for slot in range(2)
               for shape in ((1, d, D_EXPERT), (1, d, D_EXPERT), (1, D_EXPERT, d))]
    eo = pl.pallas_call(
        _expert_kernel,
        grid_spec=pltpu.PrefetchScalarGridSpec(
            num_scalar_prefetch=4,
            grid=(nt_max // 2,),
            in_specs=[pl.BlockSpec((2 * TILE_M, d), tile_map)] + w_specs,
            out_specs=pl.BlockSpec((2 * TILE_M, d), tile_map),
            scratch_shapes=[pltpu.VMEM((2, d, D_EXPERT), BF16), pltpu.VMEM((2, d, D_EXPERT), BF16),
                            pltpu.VMEM((2, D_EXPERT, d), BF16)],
        ),
        out_shape=jax.ShapeDtypeStruct((sorted_rows, d), BF16),
        input_output_aliases={4: 0},
        compiler_params=arb1,
        name="moe_experts",
    )(slot0, slot1, sel, nt, xsort, weg, weu, wed, weg, weu, wed)

    nph = COMBINE_PHASES
    pstep = lambda i, j: jnp.minimum(i * nph + j, npc * nph - 1)
    sstep = lambda i, j: jnp.maximum((i - npc) * nph + j, 0)
    smem_chunk = pl.BlockSpec((1, 1, TOP_K * TC), lambda i, j: (i, 0, 0), memory_space=pltpu.SMEM)
    yp, ys = pl.pallas_call(
        functools.partial(_combine_kernel, n_prompt_chunks=npc, n_chunks=n_chunks),
        grid=(n_chunks, nph),
        in_specs=[_smem_spec(), _smem_spec(), _smem_spec(), smem_chunk, smem_chunk,
                  pl.BlockSpec((gpp, r, d), lambda i, j: (pstep(i, j), 0, 0)),
                  pl.BlockSpec((gpp, r, d), lambda i, j: (sstep(i, j), 0, 0)),
                  pl.BlockSpec((1, 1, d), lambda i, j: (pclamp(i) // cpb, 0, 0)),
                  pl.BlockSpec((gpp, 1, d), lambda i, j: (sstep(i, j), 0, 0)),
                  pl.BlockSpec((PHASE_ROWS, d), lambda i, j: (i * nph + j, 0)),
                  cst((1, d)), cst((d, D_EXPERT)), cst((d, D_EXPERT)), cst((D_EXPERT, d)),
                  pl.BlockSpec(memory_space=pl.ANY)],
        out_specs=[pl.BlockSpec((gpp, r, d), lambda i, j: (pstep(i, j), 0, 0)),
                   pl.BlockSpec((gpp, r, d), lambda i, j: (sstep(i, j), 0, 0))],
        out_shape=[jax.ShapeDtypeStruct(xp.shape, F32), jax.ShapeDtypeStruct(xs.shape, F32)],
        scratch_shapes=[pltpu.VMEM((PLANES * LOC_STRIDE, LANE), F32),
                        pltpu.VMEM((2, LOCAL_ROWS, d), BF16),
                        pltpu.VMEM((PLANES * Y_STRIDE, LANE), F32),
                        pltpu.SMEM((2,), jnp.int32),
                        pltpu.SemaphoreType.DMA, pltpu.SemaphoreType.DMA],
        compiler_params=arb2,
        name="moe_combine",
    )(counts, ebase, cbase, pos, wts, xp, xs, modp[:, :, 2 * d:3 * d], mods_g[:, :, 2 * d:3 * d],
      h2, g_final, wsg, wsu, wsd, eo)
    return yp.reshape(b, s_len, d), ys.reshape(db, ds, d)


def _t5_bucket(rel):
    nb = NUM_BUCKETS // 2
    max_exact = nb // 2
    ret = jnp.where(rel > 0, nb, 0)
    n = jnp.abs(rel)
    nf = jnp.maximum(n, 1).astype(F32)
    large = max_exact + (jnp.log(nf / max_exact) / math.log(REL_MAX_DIST / max_exact)
                         * (nb - max_exact)).astype(jnp.int32)
    large = jnp.minimum(large, nb - 1)
    return ret + jnp.where(n < max_exact, n, large)


def _pair_layout(per_head):
    groups = []
    for g in range(N_KV_HEADS):
        pars = []
        for par in range(2):
            pars.append(jnp.concatenate([per_head[4 * g + par], per_head[4 * g + par + 2]], axis=0))
        groups.append(jnp.stack(pars))
    return jnp.stack(groups)


REL_SPAN = 2 * LANE


def _rel_bias_kernel(tab_ref, o_ref):
    tab = tab_ref[...]
    for i in range(CHUNK):
        shift = CHUNK - 1 - i
        o_ref[i] = pltpu.roll(tab, (REL_SPAN - shift) % REL_SPAN, 1)[:, 0:BAND]


def _rel_bias(rel_table):
    rel = jnp.arange(REL_SPAN) - (WINDOW + CHUNK - 1)
    tab = rel_table[_t5_bucket(rel)].astype(F32).T
    out = pl.pallas_call(
        _rel_bias_kernel,
        in_specs=[pl.BlockSpec((N_HEADS, REL_SPAN), lambda: (0, 0))],
        out_specs=pl.BlockSpec((CHUNK, N_HEADS, BAND), lambda: (0, 0, 0)),
        out_shape=jax.ShapeDtypeStruct((CHUNK, N_HEADS, BAND), F32),
        name="rel_bias",
    )(tab)
    return jnp.transpose(out, (1, 0, 2))


def _pair_window_layout(bias):
    off = jnp.full((CHUNK, CHUNK), -1e30, F32)
    groups = []
    for g in range(N_KV_HEADS):
        pars = []
        for par in range(2):
            rows = []
            for hp in range(2):
                head = 4 * g + par + 2 * hp
                rows.append(jnp.concatenate([bias[head], off], axis=1))
                rows.append(jnp.concatenate([off, bias[head]], axis=1))
            pars.append(jnp.concatenate(rows, axis=0))
        groups.append(jnp.stack(pars))
    return jnp.stack(groups)


def kernel(x_prompt, x_sample, c_prompt, c_sample, cache_k, cache_v, state_pool, w_ada, b_ada, g_norm_mix, g_norm_ffn, w_in, w_pool_group, pool_scale, attn_sinks, w_pool_proj, w_attn_proj, w_out, rel_table, w_router, router_bias, w_exp_gate, w_exp_up, w_exp_down, w_sh_gate, w_sh_up, w_sh_down, g_final):
    b, s_len, d = x_prompt.shape
    db, ds, _ = x_sample.shape
    l = 0

    c_all = jnp.concatenate([c_prompt, c_sample, jnp.zeros((4, d), F32)], axis=0)
    mods = _ada(c_all, w_ada[l], b_ada[l])
    mod_p = mods[0:b].reshape(b, 1, 6 * d)
    mod_s = mods[b:b + db].reshape(db, 1, 6 * d)

    bias = _rel_bias(rel_table)
    bias4 = jnp.pad(_pair_layout(bias), ((0, 0), (0, 0), (0, 0), (0, PAIR_KEYS - BAND)), constant_values=-1e30)
    sink4 = _pair_layout(jnp.broadcast_to(attn_sinks[l].astype(F32)[:, None, None], (N_HEADS, CHUNK, 128)))
    w_pg = jnp.zeros((POOL_WIDTH, POOL_WIDTH), F32)
    for g in range(len(POOL_WINDOWS)):
        w_pg = w_pg.at[g * POOL_GW:(g + 1) * POOL_GW, g * POOL_GW:(g + 1) * POOL_GW].set(w_pool_group[l, g])
    w_pg = w_pg.astype(BF16)
    w_in_b = w_in[l].astype(BF16)
    w_pp = w_pool_proj[l].astype(BF16)
    w_ap = w_attn_proj[l].astype(BF16)
    w_out_b = w_out[l].astype(BF16)
    g_mix = g_norm_mix[l].reshape(1, d)
    g_ffn = g_norm_ffn[l].reshape(1, d)
    ps = pool_scale[l].reshape(1, POOL_WIDTH)

    bias_pw = _pair_window_layout(bias)
    sk = attn_sinks[l].astype(F32)
    ones = jnp.ones((2 * CHUNK, LANE), F32)
    sink_pw = jnp.stack([jnp.stack([jnp.concatenate([ones * sk[4 * g + par], ones * sk[4 * g + par + 2]], axis=0)
                                    for par in range(2)]) for g in range(N_KV_HEADS)])
    x1p, nk_p, nv_p, np_p = _mixer_prompt(x_prompt, mod_p, g_mix, w_in_b, w_pg, ps, bias_pw, sink_pw,
                                          w_pp, w_ap, w_out_b)
    ck = cache_k[l].reshape(db, WINDOW, KV_WIDTH)
    cv = cache_v[l].reshape(db, WINDOW, KV_WIDTH)
    sp16 = jnp.pad(state_pool[l], ((0, 0), (HIST - POOL_PAD, 0), (0, 0)))
    x1s, nk_s, nv_s, np_s = _mixer_sample(x_sample, mod_s, ck, cv, sp16, g_mix, w_in_b, w_pg, ps,
                                          bias4, sink4, w_pp, w_ap, w_out_b)

    y_p, y_s = _moe_sparse(x1p, x1s, mod_p[:, :, 3 * d:], mod_s[:, :, 3 * d:], g_ffn, g_final.reshape(1, d),
                           w_router[l].T.astype(BF16), router_bias[l].astype(F32).reshape(N_EXPERTS, 1),
                           w_sh_gate[l].astype(BF16), w_sh_up[l].astype(BF16), w_sh_down[l].astype(BF16),
                           w_exp_gate[l], w_exp_up[l], w_exp_down[l])

    kv_shape = lambda n: (1, n, WINDOW, N_KV_HEADS, HEAD_DIM)
    return (y_p, y_s,
            nk_p.reshape(kv_shape(b)), nv_p.reshape(kv_shape(b)),
            np_p[:, HIST - POOL_PAD:, :][None],
            nk_s.reshape(kv_shape(db)), nv_s.reshape(kv_shape(db)),
            np_s[:, HIST - POOL_PAD:, :][None])
```

```python
import functools
import math

import jax
import jax.numpy as jnp
from jax import lax
from jax.experimental import pallas as pl
from jax.experimental.pallas import tpu as pltpu

F32 = jnp.float32
BF16 = jnp.bfloat16

D_MODEL = 1024
CHUNK = 64
EPS = 1e-6
POOL_WIDTH = 512
POOL_WINDOWS = (2, 4, 8, 16)
POOL_GW = 128
POOL_PAD = 15
HEAD_DIM = 64
N_HEADS = 8
N_KV_HEADS = 2
ATTN_WIDTH = 512
KV_WIDTH = 128
WINDOW = 128
BAND = WINDOW + CHUNK
PAIR_KEYS = BAND + CHUNK
NUM_BUCKETS = 32
REL_MAX_DIST = 128
PAST_LEN = 4096
N_EXPERTS = 32
TOP_K = 4
N_GROUPS = 4
TOPK_GROUPS = 2
EXPERTS_PER_GROUP = 8
D_EXPERT = 256
ROUTED_SCALE = 2.5

OFF_U = 0
OFF_Q = OFF_U + POOL_WIDTH
OFF_K = OFF_Q + ATTN_WIDTH
OFF_V = OFF_K + KV_WIDTH
OFF_GP = OFF_V + KV_WIDTH
OFF_GA = OFF_GP + D_MODEL
IN_WIDTH = OFF_GA + D_MODEL

HIST = 16
MIX_ROWS = 512
MERGE_PANEL = 256
VMEM_LIMIT = 56 * 1024 * 1024

GROUP_ROWS = 64
TC = 1024
PAGE = 16
PAGE_SHIFT = 4
TILE_M = 512
PAGES_PER_TILE = TILE_M // PAGE
TILE_SHIFT = 5
LOCAL_ROWS = TOP_K * TC + N_EXPERTS * PAGE
LANE = 128
PLANES = D_MODEL // LANE
PLANE_PAD = 8
HP_STRIDE = TC + PLANE_PAD
LOC_STRIDE = LOCAL_ROWS + PLANE_PAD
COMBINE_PHASES = 4
PHASE_ROWS = TC // COMBINE_PHASES
RUN_COPY = 4
RUN_COPY_SHIFT = 2
Y_STRIDE = PHASE_ROWS + PLANE_PAD


def _dot(a, b):
    return jnp.dot(a, b, preferred_element_type=F32)


def _dot_nt(a, b):
    return lax.dot_general(a, b, (((1,), (1,)), ((), ())), preferred_element_type=F32)


def _norm_mod(x, g, shift, scale):
    ms = jnp.mean(x * x, axis=-1, keepdims=True)
    y = x * lax.rsqrt(ms + EPS) * g
    return y * (1.0 + scale) + shift


def _silu(x):
    return x * jax.nn.sigmoid(x)


def _ada_kernel(c_ref, w_ref, b_ref, o_ref):
    a = _silu(c_ref[...]).astype(BF16)
    o_ref[...] = _dot(a, w_ref[...].astype(BF16)) + b_ref[...]


def _ada(c_all, w_ada, b_ada):
    rows = c_all.shape[0]
    n = w_ada.shape[1]
    tn = 768
    return pl.pallas_call(
        _ada_kernel,
        grid=(n // tn,),
        in_specs=[
            pl.BlockSpec((rows, D_MODEL), lambda j: (0, 0)),
            pl.BlockSpec((D_MODEL, tn), lambda j: (0, j)),
            pl.BlockSpec((1, tn), lambda j: (0, j)),
        ],
        out_specs=pl.BlockSpec((rows, tn), lambda j: (0, j)),
        out_shape=jax.ShapeDtypeStruct((rows, n), F32),
        compiler_params=pltpu.CompilerParams(dimension_semantics=("arbitrary",)),
        name="ada",
    )(c_all, w_ada, b_ada.reshape(1, n))


def _pool_feats(ue, pos0, rows):
    s2 = ue + pltpu.roll(ue, 1, 0)
    s4 = s2 + pltpu.roll(s2, 2, 0)
    s8 = s4 + pltpu.roll(s4, 4, 0)
    s16 = s8 + pltpu.roll(s8, 8, 0)
    pos = pos0 + lax.broadcasted_iota(jnp.int32, (rows, POOL_GW), 0)
    outs = []
    for g, (w, sw) in enumerate(zip(POOL_WINDOWS, (s2, s4, s8, s16))):
        sl = slice(g * POOL_GW, (g + 1) * POOL_GW)
        cnt = jnp.minimum(pos + 1, w).astype(F32)
        outs.append(sw[HIST:, sl] / cnt - ue[HIST:, sl])
    return jnp.concatenate(outs, axis=1)


def _kv_variants(t):
    lane = lax.broadcasted_iota(jnp.int32, t.shape, 1)
    low = lane < HEAD_DIM
    swapped = pltpu.roll(t, HEAD_DIM, 1)
    zero = jnp.zeros_like(t)
    return (jnp.where(low, t, zero), jnp.where(low, zero, swapped),
            jnp.where(low, swapped, zero), jnp.where(low, zero, t))


def _merge_out(h, pool_y, attn_y, win_ref, wpp_ref, wap_ref, wout_ref, m_scr):
    rows = h.shape[0]
    pool_b = pool_y.astype(BF16)
    attn_b = attn_y.astype(BF16)
    for n in range(D_MODEL // MERGE_PANEL):
        lo, hi = n * MERGE_PANEL, (n + 1) * MERGE_PANEL
        gp = _dot(h, win_ref[:, OFF_GP + lo:OFF_GP + hi])
        ga = _dot(h, win_ref[:, OFF_GA + lo:OFF_GA + hi])
        pp = _dot(pool_b, wpp_ref[:, lo:hi])
        ap = _dot(attn_b, wap_ref[:, lo:hi])
        m_scr[0:rows, lo:hi] = (jax.nn.sigmoid(gp) * pp + jax.nn.sigmoid(ga) * ap).astype(BF16)
    return _dot(m_scr[0:rows, :], wout_ref[...])


def _mixer_prompt_kernel(x_ref, mod_ref, g_ref, win_ref, wpg_ref, ps_ref, bias_ref, sink_ref,
                         wpp_ref, wap_ref, wout_ref,
                         x1_ref, nk_ref, nv_ref, npool_ref,
                         uext, k0, k1, k2, k3, v0, v1, v2, v3, m_scr, s_scr, p_scr, a_scr):
    ts = MIX_ROWS
    s = pl.program_id(1)
    kext = (k0, k1, k2, k3)
    vext = (v0, v1, v2, v3)

    @pl.when(s == 0)
    def _():
        uext[0:HIST, :] = jnp.zeros((HIST, POOL_WIDTH), F32)
        for r in kext:
            r[0:WINDOW, :] = jnp.zeros((WINDOW, KV_WIDTH), BF16)
        for r in vext:
            r[:, 0:WINDOW] = jnp.zeros((KV_WIDTH, WINDOW), BF16)

    x = x_ref[0]
    shift = mod_ref[0, :, 0:D_MODEL]
    scale = mod_ref[0, :, D_MODEL:2 * D_MODEL]
    gate = mod_ref[0, :, 2 * D_MODEL:3 * D_MODEL]
    h = _norm_mod(x, g_ref[...], shift, scale).astype(BF16)

    u = _dot(h, win_ref[:, OFF_U:OFF_U + POOL_WIDTH])
    q = (_dot(h, win_ref[:, OFF_Q:OFF_Q + ATTN_WIDTH]) * (HEAD_DIM ** -0.5)).astype(BF16)
    kv = _dot(h, win_ref[:, OFF_K:OFF_V + KV_WIDTH])
    k = kv[:, 0:KV_WIDTH]
    v = kv[:, KV_WIDTH:2 * KV_WIDTH]
    nk_ref[0] = k[ts - WINDOW:ts, :]
    nv_ref[0] = v[ts - WINDOW:ts, :]
    npool_ref[0] = u[ts - HIST:ts, :]

    pos0 = s * ts
    uext[HIST:HIST + ts, :] = u
    p = _pool_feats(uext[...], pos0, ts)
    uext[0:HIST, :] = u[ts - HIST:ts, :]
    pool_y = _dot(p.astype(BF16), wpg_ref[...]) * ps_ref[...]

    for r, t in zip(kext, _kv_variants(k)):
        r[WINDOW:WINDOW + ts, :] = t.astype(BF16)
    for r, t in zip(vext, _kv_variants(v)):
        r[:, WINDOW:WINDOW + ts] = t.T.astype(BF16)
    pair_rows = 2 * CHUNK
    blk_q = 2 * pair_rows
    n_pairs = ts // pair_rows
    per_pair = N_KV_HEADS * 2
    key_i = lax.broadcasted_iota(jnp.int32, (PAIR_KEYS, blk_q), 0)
    for pr in range(n_pairs):
        lo = pr * pair_rows
        valid = (pos0 + lo + key_i) >= WINDOW
        for g in range(N_KV_HEADS):
            qs = jnp.concatenate([q[lo:lo + pair_rows, (2 * g) * LANE:(2 * g + 1) * LANE],
                                  q[lo:lo + pair_rows, (2 * g + 1) * LANE:(2 * g + 2) * LANE]], axis=0)
            for par in range(2):
                s = _dot_nt(kext[2 * g + par][lo:lo + PAIR_KEYS, :], qs) + bias_ref[g, par]
                s_scr[(pr * N_KV_HEADS + g) * 2 + par] = jnp.where(valid, s, -1e30)
    sink = sink_ref[...]
    for pr in range(n_pairs):
        s = jnp.concatenate([s_scr[pr * per_pair + b] for b in range(per_pair)], axis=1)
        m = jnp.maximum(jnp.max(s, axis=0, keepdims=True), sink)
        e = jnp.exp(s - m)
        denom = jnp.sum(e, axis=0, keepdims=True) + jnp.exp(sink - m)
        p = (e / denom).astype(BF16)
        for b in range(per_pair):
            p_scr[pr * per_pair + b] = p[:, b * blk_q:(b + 1) * blk_q]
    for pr in range(n_pairs):
        lo = pr * pair_rows
        for g in range(N_KV_HEADS):
            blk = (pr * N_KV_HEADS + g) * 2
            ot = (_dot(vext[2 * g][:, lo:lo + PAIR_KEYS], p_scr[blk])
                  + _dot(vext[2 * g + 1][:, lo:lo + PAIR_KEYS], p_scr[blk + 1]))
            o = ot.T
            a_scr[lo:lo + pair_rows, (2 * g) * LANE:(2 * g + 1) * LANE] = o[0:pair_rows].astype(BF16)
            a_scr[lo:lo + pair_rows, (2 * g + 1) * LANE:(2 * g + 2) * LANE] = o[pair_rows:].astype(BF16)
    for r in kext:
        r[0:WINDOW, :] = r[ts:ts + WINDOW, :]
    for r in vext:
        r[:, 0:WINDOW] = r[:, ts:ts + WINDOW]

    y = _merge_out(h, pool_y, a_scr[...], win_ref, wpp_ref, wap_ref, wout_ref, m_scr)
    x1_ref[0] = x + gate * y


def _const_spec(shape):
    nd = len(shape)
    return pl.BlockSpec(shape, lambda *_: (0,) * nd)


def _mixer_prompt(x, mod, g_mix, w_in, w_pg, pool_scale, bias4, sink4, w_pp, w_ap, w_out):
    b, s_len, d = x.shape
    ts = MIX_ROWS
    pair_rows = 2 * CHUNK
    n_blocks = (ts // pair_rows) * N_KV_HEADS * 2
    kv_scratch = ([pltpu.VMEM((WINDOW + ts, KV_WIDTH), BF16) for _ in range(4)]
                  + [pltpu.VMEM((KV_WIDTH, WINDOW + ts), BF16) for _ in range(4)])
    return pl.pallas_call(
        _mixer_prompt_kernel,
        grid=(b, s_len // ts),
        in_specs=[
            pl.BlockSpec((1, ts, d), lambda i, j: (i, j, 0)),
            pl.BlockSpec((1, 1, 6 * d), lambda i, j: (i, 0, 0)),
            _const_spec((1, d)),
            _const_spec((d, IN_WIDTH)),
            _const_spec((POOL_WIDTH, POOL_WIDTH)),
            _const_spec((1, POOL_WIDTH)),
            _const_spec(bias4.shape),
            _const_spec(sink4.shape),
            _const_spec((POOL_WIDTH, d)),
            _const_spec((ATTN_WIDTH, d)),
            _const_spec((d, d)),
        ],
        out_specs=[
            pl.BlockSpec((1, ts, d), lambda i, j: (i, j, 0)),
            pl.BlockSpec((1, WINDOW, KV_WIDTH), lambda i, j: (i, 0, 0)),
            pl.BlockSpec((1, WINDOW, KV_WIDTH), lambda i, j: (i, 0, 0)),
            pl.BlockSpec((1, HIST, POOL_WIDTH), lambda i, j: (i, 0, 0)),
        ],
        out_shape=[
            jax.ShapeDtypeStruct((b, s_len, d), F32),
            jax.ShapeDtypeStruct((b, WINDOW, KV_WIDTH), F32),
            jax.ShapeDtypeStruct((b, WINDOW, KV_WIDTH), F32),
            jax.ShapeDtypeStruct((b, HIST, POOL_WIDTH), F32),
        ],
        scratch_shapes=[pltpu.VMEM((HIST + ts, POOL_WIDTH), F32)] + kv_scratch
                       + [pltpu.VMEM((ts, d), BF16),
                          pltpu.VMEM((n_blocks, 2 * pair_rows, PAIR_KEYS), F32),
                          pltpu.VMEM((n_blocks, 2 * pair_rows, PAIR_KEYS), BF16),
                          pltpu.VMEM((ts, ATTN_WIDTH), BF16)],
        compiler_params=pltpu.CompilerParams(
            dimension_semantics=("arbitrary", "arbitrary"), vmem_limit_bytes=VMEM_LIMIT),
        name="mixer_prompt",
    )(x, mod, g_mix, w_in, w_pg, pool_scale, bias4, sink4, w_pp, w_ap, w_out)


def _mixer_sample_kernel(x_ref, mod_ref, ck_ref, cv_ref, sp_ref, g_ref, win_ref, wpg_ref, ps_ref,
                         bias_ref, sink_ref, wpp_ref, wap_ref, wout_ref,
                         x1_ref, nk_ref, nv_ref, npool_ref, m_scr, ks, vs, s_scr, p_scr, a_scr):
    nseq, ds, d = x_ref.shape
    rows = nseq * ds
    x = x_ref[...]
    shift = mod_ref[:, :, 0:D_MODEL]
    scale = mod_ref[:, :, D_MODEL:2 * D_MODEL]
    gate = mod_ref[:, :, 2 * D_MODEL:3 * D_MODEL]
    h = _norm_mod(x, g_ref[...], shift, scale).reshape(rows, d).astype(BF16)

    u = _dot(h, win_ref[:, OFF_U:OFF_U + POOL_WIDTH])
    q = (_dot(h, win_ref[:, OFF_Q:OFF_Q + ATTN_WIDTH]) * (HEAD_DIM ** -0.5)).astype(BF16)
    kv = _dot(h, win_ref[:, OFF_K:OFF_V + KV_WIDTH])
    k = kv[:, 0:KV_WIDTH]
    v = kv[:, KV_WIDTH:2 * KV_WIDTH]

    pool_parts = []
    key_pad = jnp.zeros((PAIR_KEYS - BAND, KV_WIDTH), F32)
    for b in range(nseq):
        lo = b * ds
        ub = u[lo:lo + ds, :]
        kb = k[lo:lo + ds, :]
        vb = v[lo:lo + ds, :]
        ck = ck_ref[b]
        cv = cv_ref[b]
        nk_ref[b] = jnp.concatenate([ck[ds:WINDOW, :], kb], axis=0)
        nv_ref[b] = jnp.concatenate([cv[ds:WINDOW, :], vb], axis=0)
        npool_ref[b] = ub[ds - HIST:ds, :]
        ue = jnp.concatenate([sp_ref[b], ub], axis=0)
        pool_parts.append(_pool_feats(ue, PAST_LEN, ds))
        for var, t in enumerate(_kv_variants(jnp.concatenate([ck, kb, key_pad], axis=0))):
            ks[var, b] = t.astype(BF16)
        for var, t in enumerate(_kv_variants(jnp.concatenate([cv, vb, key_pad], axis=0))):
            vs[var, b] = t.astype(BF16)
    p = jnp.concatenate(pool_parts, axis=0)
    pool_y = _dot(p.astype(BF16), wpg_ref[...]) * ps_ref[...]

    per_seq = N_KV_HEADS * 2
    for b in range(nseq):
        lo = b * ds
        for g in range(N_KV_HEADS):
            qs = jnp.concatenate([q[lo:lo + ds, (2 * g) * LANE:(2 * g + 1) * LANE],
                                  q[lo:lo + ds, (2 * g + 1) * LANE:(2 * g + 2) * LANE]], axis=0)
            for par in range(2):
                s_scr[b * per_seq + 2 * g + par] = _dot_nt(qs, ks[2 * g + par, b]) + bias_ref[g, par]
    group = 2
    grp_rows = group * per_seq * 2 * ds
    sink1 = sink_ref[...].reshape(per_seq * 2 * ds, LANE)[:, 0:1]
    sink = jnp.concatenate([sink1] * group, axis=0)
    for gi in range(nseq // group):
        blks = slice(gi * group * per_seq, (gi + 1) * group * per_seq)
        s = s_scr[blks].reshape(grp_rows, PAIR_KEYS)
        m = jnp.maximum(jnp.max(s, axis=-1, keepdims=True), sink)
        e = jnp.exp(s - m)
        esum = _dot(e.astype(BF16), jnp.ones((PAIR_KEYS, LANE), BF16))
        denom = esum + jnp.exp(sink - m)
        p_scr[blks] = ((e / jnp.concatenate([denom, denom], axis=1)).astype(BF16)
                       .reshape(group * per_seq, 2 * ds, PAIR_KEYS))
    for b in range(nseq):
        lo = b * ds
        for g in range(N_KV_HEADS):
            blk = b * per_seq + 2 * g
            o = _dot(p_scr[blk], vs[2 * g, b]) + _dot(p_scr[blk + 1], vs[2 * g + 1, b])
            a_scr[lo:lo + ds, (2 * g) * LANE:(2 * g + 1) * LANE] = o[0:ds].astype(BF16)
            a_scr[lo:lo + ds, (2 * g + 1) * LANE:(2 * g + 2) * LANE] = o[ds:].astype(BF16)

    y = _merge_out(h, pool_y, a_scr[...], win_ref, wpp_ref, wap_ref, wout_ref, m_scr)
    x1_ref[...] = x + gate * y.reshape(nseq, ds, d)


def _mixer_sample(x, mod, ck, cv, sp16, g_mix, w_in, w_pg, pool_scale, bias4, sink4, w_pp, w_ap, w_out):
    db, ds, d = x.shape
    nseq = MIX_ROWS // ds
    seq_spec = lambda shape: pl.BlockSpec((nseq,) + shape, lambda i: (i, 0, 0))
    return pl.pallas_call(
        _mixer_sample_kernel,
        grid=(db // nseq,),
        in_specs=[
            seq_spec((ds, d)),
            seq_spec((1, 6 * d)),
            seq_spec((WINDOW, KV_WIDTH)),
            seq_spec((WINDOW, KV_WIDTH)),
            seq_spec((HIST, POOL_WIDTH)),
            _const_spec((1, d)),
            _const_spec((d, IN_WIDTH)),
            _const_spec((POOL_WIDTH, POOL_WIDTH)),
            _const_spec((1, POOL_WIDTH)),
            _const_spec(bias4.shape),
            _const_spec(sink4.shape),
            _const_spec((POOL_WIDTH, d)),
            _const_spec((ATTN_WIDTH, d)),
            _const_spec((d, d)),
        ],
        out_specs=[
            seq_spec((ds, d)),
            seq_spec((WINDOW, KV_WIDTH)),
            seq_spec((WINDOW, KV_WIDTH)),
            seq_spec((HIST, POOL_WIDTH)),
        ],
        out_shape=[
            jax.ShapeDtypeStruct((db, ds, d), F32),
            jax.ShapeDtypeStruct((db, WINDOW, KV_WIDTH), F32),
            jax.ShapeDtypeStruct((db, WINDOW, KV_WIDTH), F32),
            jax.ShapeDtypeStruct((db, HIST, POOL_WIDTH), F32),
        ],
        scratch_shapes=[pltpu.VMEM((MIX_ROWS, d), BF16),
                        pltpu.VMEM((4, nseq, PAIR_KEYS, KV_WIDTH), BF16),
                        pltpu.VMEM((4, nseq, PAIR_KEYS, KV_WIDTH), BF16),
                        pltpu.VMEM((nseq * N_KV_HEADS * 2, 2 * ds, PAIR_KEYS), F32),
                        pltpu.VMEM((nseq * N_KV_HEADS * 2, 2 * ds, PAIR_KEYS), BF16),
                        pltpu.VMEM((MIX_ROWS, ATTN_WIDTH), BF16)],
        compiler_params=pltpu.CompilerParams(
            dimension_semantics=("arbitrary",), vmem_limit_bytes=VMEM_LIMIT),
        name="mixer_sample",
    )(x, mod, ck, cv, sp16, g_mix, w_in, w_pg, pool_scale, bias4, sink4, w_pp, w_ap, w_out)


def _first_argmax(vals, idx, size):
    m = jnp.max(vals, axis=0, keepdims=True)
    first = jnp.min(jnp.where(vals == m, idx, size), axis=0, keepdims=True)
    return m, first


def _router_picks(h, wr_ref, rb_ref):
    t = h.shape[0]
    logits = _dot_nt(wr_ref[...], h)
    scores = jax.nn.sigmoid(logits)
    sel = scores + rb_ref[...]
    neg = jnp.float32(-jnp.inf)
    eidx = lax.broadcasted_iota(jnp.int32, (N_EXPERTS, t), 0)

    gscores = []
    bidx = lax.broadcasted_iota(jnp.int32, (EXPERTS_PER_GROUP, t), 0)
    for g in range(N_GROUPS):
        blk = sel[g * EXPERTS_PER_GROUP:(g + 1) * EXPERTS_PER_GROUP, :]
        m1, i1 = _first_argmax(blk, bidx, EXPERTS_PER_GROUP)
        m2 = jnp.max(jnp.where(bidx == i1, neg, blk), axis=0, keepdims=True)
        gscores.append(m1 + m2)
    gs = jnp.concatenate(gscores, axis=0)
    gidx = lax.broadcasted_iota(jnp.int32, (N_GROUPS, t), 0)
    _, g1 = _first_argmax(gs, gidx, N_GROUPS)
    _, g2 = _first_argmax(jnp.where(gidx == g1, neg, gs), gidx, N_GROUPS)
    egrp = eidx // EXPERTS_PER_GROUP
    cand = jnp.where((egrp == g1) | (egrp == g2), sel, neg)

    picks = []
    for _ in range(TOP_K):
        _, i = _first_argmax(cand, eidx, N_EXPERTS)
        hit = eidx == i
        picks.append(hit)
        cand = jnp.where(hit, neg, cand)
    return picks, scores


def _swiglu_rows(h, wg, wu, wd):
    act = _silu(_dot(h, wg)) * _dot(h, wu)
    return _dot(act.astype(BF16), wd)


def _route_kernel(xp_ref, xs_ref, modp_ref, mods_ref, g_ref, wr_ref, rb_ref, tri_ref,
                  h_ref, pos_ref, w_ref, cnt_ref, *, n_prompt_chunks):
    is_prompt = pl.program_id(0) < n_prompt_chunks
    x = jnp.where(is_prompt, xp_ref[...], xs_ref[...])
    shift = jnp.where(is_prompt, modp_ref[:, :, 0:D_MODEL], mods_ref[:, :, 0:D_MODEL])
    scale = jnp.where(is_prompt, modp_ref[:, :, D_MODEL:2 * D_MODEL], mods_ref[:, :, D_MODEL:2 * D_MODEL])
    h = _norm_mod(x, g_ref[...], shift, scale).reshape(TC, D_MODEL).astype(BF16)
    h_ref[...] = h
    picks, scores = _router_picks(h, wr_ref, rb_ref)
    picked = picks[0] | picks[1] | picks[2] | picks[3]
    cnt = _dot(jnp.where(picked, 1.0, 0.0).astype(BF16), tri_ref[...])
    n_e = jnp.max(cnt, axis=1, keepdims=True).astype(jnp.int32)
    n_b = jnp.broadcast_to(n_e, (N_EXPERTS, LANE))
    cnt_ref[0] = n_b
    padded = lax.shift_left(lax.shift_right_logical(n_b + (PAGE - 1), PAGE_SHIFT), PAGE_SHIFT)
    row = lax.broadcasted_iota(jnp.int32, (N_EXPERTS, LANE), 0)
    scan = padded
    for s in (1, 2, 4, 8, 16):
        scan = scan + jnp.where(row >= s, pltpu.roll(scan, s, 0), 0)
    base = (scan - padded)[:, 0:1].astype(F32)
    slot = base + cnt - 1.0
    pos = [jnp.sum(jnp.where(p, slot, 0.0), axis=0, keepdims=True) for p in picks]
    wk = [jnp.sum(jnp.where(p, scores, 0.0), axis=0, keepdims=True) for p in picks]
    wsum = wk[0] + wk[1] + wk[2] + wk[3]
    pos_ref[0] = jnp.concatenate(pos, axis=0).astype(jnp.int32)
    w_ref[0] = jnp.concatenate(wk, axis=0) / wsum * ROUTED_SCALE


def _plan_kernel(cnt_ref, te_ref, nt_ref, ebase_ref, cbase_ref, etp_ref, slot0_ref, slot1_ref, sel_ref,
                 *, n_chunks, nt_max):
    def per_expert(e, tile0):
        def per_chunk(c, acc):
            cbase_ref[c * N_EXPERTS + e] = acc
            return acc + lax.shift_right_logical(cnt_ref[c, e] + (PAGE - 1), PAGE_SHIFT)
        tp = lax.fori_loop(0, n_chunks, per_chunk, 0)
        etp_ref[e] = tp
        ebase_ref[e] = tile0 * PAGES_PER_TILE
        nt = lax.shift_right_logical(tp + (PAGES_PER_TILE - 1), TILE_SHIFT)

        def fill(j, carry):
            te_ref[tile0 + j] = e
            return carry
        lax.fori_loop(0, nt, fill, 0)
        return tile0 + nt
    total = lax.fori_loop(0, N_EXPERTS, per_expert, 0)
    nt_ref[0] = total

    def tail(j, carry):
        te_ref[j] = N_EXPERTS - 1
        return carry
    lax.fori_loop(total, nt_max, tail, 0)

    def per_step(i, carry):
        c0, c1 = carry
        used = 2 * i < total
        a = te_ref[jnp.minimum(2 * i, total - 1)]
        b = te_ref[jnp.minimum(2 * i + 1, total - 1)]
        hit_a = (a == c0) | (a == c1)
        evict0 = hit_a | (c0 != b)
        n0 = jnp.where(used & ~hit_a & evict0, a, c0)
        n1 = jnp.where(used & ~hit_a & ~evict0, a, c1)
        sa = jnp.where(a == n0, 0, 1)
        hit_b = (b == n0) | (b == n1)
        m0 = jnp.where(used & ~hit_b & (sa == 1), b, n0)
        m1 = jnp.where(used & ~hit_b & (sa == 0), b, n1)
        slot0_ref[i] = jnp.maximum(m0, 0)
        slot1_ref[i] = jnp.maximum(m1, 0)
        sel_ref[2 * i] = sa
        sel_ref[2 * i + 1] = jnp.where(b == m0, 0, 1)
        return (m0, m1)
    lax.fori_loop(0, nt_max // 2, per_step, (-1, -1))


def _page_copy(src, src_page, dst, dst_page, sem, pages=1):
    rows = pages * PAGE
    s = src.at[pl.ds(pl.multiple_of(src_page * PAGE, PAGE), rows), :]
    t = dst.at[pl.ds(pl.multiple_of(dst_page * PAGE, PAGE), rows), :]
    return pltpu.make_async_copy(s, t, sem)


def _for_each_run(c, cnt_ref, ebase_ref, cbase_ref, fn):
    def per_expert(e, carry):
        lpage, nbig, nsmall = carry
        n = cnt_ref[c, e]
        npages = lax.shift_right_logical(n + (PAGE - 1), PAGE_SHIFT)
        fn(n, lpage, ebase_ref[e] + cbase_ref[c * N_EXPERTS + e], npages)
        return (lpage + npages, nbig + lax.shift_right_logical(npages, RUN_COPY_SHIFT),
                nsmall + (npages & (RUN_COPY - 1)))
    return lax.fori_loop(0, N_EXPERTS, per_expert, (0, 0, 0))


def _start_run_copies(src, src_page, dst, dst_page, npages, sem_big, sem_small):
    nbig = lax.shift_right_logical(npages, RUN_COPY_SHIFT)

    def big(j, carry):
        _page_copy(src, src_page + j * RUN_COPY, dst, dst_page + j * RUN_COPY, sem_big, RUN_COPY).start()
        return carry
    lax.fori_loop(0, nbig, big, 0)

    def small(j, carry):
        _page_copy(src, src_page + j, dst, dst_page + j, sem_small).start()
        return carry
    lax.fori_loop(nbig * RUN_COPY, npages, small, 0)


def _wait_copies(src, dst, nbig, nsmall, sem_big, sem_small):
    def big(j, carry):
        _page_copy(src, 0, dst, 0, sem_big, RUN_COPY).wait()
        return carry
    lax.fori_loop(0, nbig, big, 0)

    def small(j, carry):
        _page_copy(src, 0, dst, 0, sem_small).wait()
        return carry
    lax.fori_loop(0, nsmall, small, 0)


def _row_slab(ref, row, stride):
    return ref.at[pl.ds(row, PLANES, stride=stride), :]


def _dispatch_kernel(cnt_ref, ebase_ref, cbase_ref, etp_ref, nt_ref, pos_ref, h_ref, xs_hbm,
                     hpl, loc, locb, zpage, ztile, pend, sem, bsem, tsem, *, n_chunks, nt_max):
    c = pl.program_id(0)

    @pl.when(c == 0)
    def _():
        pend[0] = 0
        pend[1] = 0
        loc[...] = jnp.zeros(loc.shape, F32)

    h = h_ref[...]
    for ch in range(PLANES):
        hpl[ch * HP_STRIDE:ch * HP_STRIDE + TC, :] = h[:, ch * LANE:(ch + 1) * LANE].astype(F32)
    zpage[...] = jnp.zeros(zpage.shape, BF16)
    ztile[...] = jnp.zeros(ztile.shape, BF16)

    def scatter(t8, carry):
        for u in range(8):
            t = t8 * 8 + u
            slab = _row_slab(hpl, t, HP_STRIDE)[...]
            for k in range(TOP_K):
                _row_slab(loc, pos_ref[0, 0, k * TC + t], LOC_STRIDE)[...] = slab
        return carry
    lax.fori_loop(0, TC // 8, scatter, 0)

    _wait_copies(locb, xs_hbm, pend[0], pend[1], bsem, sem)
    for ch in range(PLANES):
        locb[:, ch * LANE:(ch + 1) * LANE] = loc[ch * LOC_STRIDE:ch * LOC_STRIDE + LOCAL_ROWS, :].astype(BF16)

    def send_run(n, lpage, gpage, npages):
        _start_run_copies(locb, lpage, xs_hbm, gpage, npages, bsem, sem)
    _, n_big, n_small = _for_each_run(c, cnt_ref, ebase_ref, cbase_ref, send_run)
    pend[0] = n_big
    pend[1] = n_small

    @pl.when(c == n_chunks - 1)
    def _():
        def pad_expert(e, n):
            tp = etp_ref[e]
            full = lax.shift_left(lax.shift_right_logical(tp + (PAGES_PER_TILE - 1), TILE_SHIFT), TILE_SHIFT)

            def per_page(j, carry):
                _page_copy(zpage, 0, xs_hbm, ebase_ref[e] + j, sem).start()
                return carry
            lax.fori_loop(tp, full, per_page, 0)
            return n + full - tp
        n_pad = lax.fori_loop(0, N_EXPERTS, pad_expert, 0)

        def tile_copy(i):
            dst = xs_hbm.at[pl.ds(pl.multiple_of(i * TILE_M, TILE_M), TILE_M), :]
            return pltpu.make_async_copy(ztile, dst, tsem)

        def tail_start(i, carry):
            tile_copy(i).start()
            return carry
        lax.fori_loop(nt_ref[0], nt_max, tail_start, 0)
        _wait_copies(locb, xs_hbm, n_big, n_small + n_pad, bsem, sem)

        def tail_wait(i, carry):
            tile_copy(i).wait()
            return carry
        lax.fori_loop(nt_ref[0], nt_max, tail_wait, 0)


def _expert_kernel(slot0_ref, slot1_ref, sel_ref, nt_ref, xs_ref, wg0_ref, wu0_ref, wd0_ref,
                   wg1_ref, wu1_ref, wd1_ref, o_ref, wgb, wub, wdb):
    i = pl.program_id(0)
    slots = ((slot0_ref, wg0_ref, wu0_ref, wd0_ref), (slot1_ref, wg1_ref, wu1_ref, wd1_ref))

    @pl.when(2 * i < nt_ref[0])
    def _():
        for slot, (idx_ref, wg_ref, wu_ref, wd_ref) in enumerate(slots):
            @pl.when((i == 0) | (idx_ref[i] != idx_ref[jnp.maximum(i - 1, 0)]))
            def _():
                wgb[slot] = wg_ref[0].astype(BF16)
                wub[slot] = wu_ref[0].astype(BF16)
                wdb[slot] = wd_ref[0].astype(BF16)
        for t in range(2):
            rows = slice(t * TILE_M, (t + 1) * TILE_M)
            sl = sel_ref[2 * i + t]
            o_ref[rows, :] = _swiglu_rows(xs_ref[rows, :], wgb[sl], wub[sl], wdb[sl]).astype(BF16)


def _combine_kernel(cnt_ref, ebase_ref, cbase_ref, pos_ref, w_ref, xp_ref, xs_ref, gp_ref, gs_ref,
                    h_ref, gf_ref, wsg_ref, wsu_ref, wsd_ref, eo_hbm, yp_ref, ys_ref,
                    loc, locb, ypl, pend, sem, bsem, *, n_prompt_chunks, n_chunks):
    c = pl.program_id(0)
    ph = pl.program_id(1)

    def fetch(chunk):
        buf = locb.at[chunk % 2]

        def fetch_run(n, lpage, gpage, npages):
            _start_run_copies(eo_hbm, gpage, buf, lpage, npages, bsem, sem)
        _, n_big, n_small = _for_each_run(chunk, cnt_ref, ebase_ref, cbase_ref, fetch_run)
        pend[0] = n_big
        pend[1] = n_small

    @pl.when((c == 0) & (ph == 0))
    def _():
        locb[...] = jnp.zeros(locb.shape, BF16)
        fetch(c)

    @pl.when(ph == 0)
    def _():
        cur = locb.at[c % 2]
        _wait_copies(eo_hbm, cur, pend[0], pend[1], bsem, sem)

        @pl.when(c + 1 < n_chunks)
        def _():
            fetch(c + 1)
        for ch in range(PLANES):
            loc[ch * LOC_STRIDE:ch * LOC_STRIDE + LOCAL_ROWS, :] = cur[:, ch * LANE:(ch + 1) * LANE].astype(F32)

    shared = _swiglu_rows(h_ref[...], wsg_ref[...], wsu_ref[...], wsd_ref[...])
    t0 = ph * PHASE_ROWS

    def gather(t8, carry):
        for u in range(8):
            t = t8 * 8 + u
            acc = jnp.zeros((PLANES, LANE), F32)
            for k in range(TOP_K):
                idx = k * TC + t0 + t
                acc = acc + w_ref[0, 0, idx] * _row_slab(loc, pos_ref[0, 0, idx], LOC_STRIDE)[...]
            _row_slab(ypl, t, Y_STRIDE)[...] = acc
        return carry
    lax.fori_loop(0, PHASE_ROWS // 8, gather, 0)
    routed = jnp.concatenate([ypl[ch * Y_STRIDE:ch * Y_STRIDE + PHASE_ROWS, :] for ch in range(PLANES)], axis=1)
    f = shared + routed

    def finish(x_ref, gate_ref, y_ref):
        xo = x_ref[...] + gate_ref[...] * f.reshape(x_ref.shape)
        ms = jnp.mean(xo * xo, axis=-1, keepdims=True)
        y_ref[...] = xo * lax.rsqrt(ms + EPS) * gf_ref[...]

    @pl.when(c < n_prompt_chunks)
    def _():
        finish(xp_ref, gp_ref, yp_ref)

    @pl.when(c >= n_prompt_chunks)
    def _():
        finish(xs_ref, gs_ref, ys_ref)


def _smem_spec():
    return pl.BlockSpec(memory_space=pltpu.SMEM)


def _moe_sparse(x1p, x1s, modp, mods, g_ffn, g_final, w_r_t, rb, wsg, wsu, wsd, weg, weu, wed):
    b, s_len, d = x1p.shape
    db, ds, _ = x1s.shape
    r = GROUP_ROWS
    xp = x1p.reshape(b * s_len // r, r, d)
    xs = x1s.reshape(db * ds // r, r, d)
    mods_g = jnp.repeat(mods, ds // r, axis=0) if ds != r else mods
    npc = b * s_len // TC
    nsc = db * ds // TC
    n_chunks = npc + nsc
    n_tok = n_chunks * TC
    cpb = s_len // TC
    gpc = TC // r
    gpp = PHASE_ROWS // r
    max_pages = (TOP_K * n_tok) // PAGE + n_chunks * N_EXPERTS
    nt_max = -(-(max_pages // PAGES_PER_TILE + N_EXPERTS) // 8) * 8
    sorted_rows = nt_max * TILE_M
    arb1 = pltpu.CompilerParams(dimension_semantics=("arbitrary",), vmem_limit_bytes=VMEM_LIMIT)
    arb2 = pltpu.CompilerParams(dimension_semantics=("arbitrary", "arbitrary"), vmem_limit_bytes=VMEM_LIMIT)
    cst = lambda shape: pl.BlockSpec(shape, lambda *_: (0,) * len(shape))
    tri = jnp.triu(jnp.ones((TC, TC), BF16))
    pclamp = lambda i: jnp.minimum(i, npc - 1)
    sclamp = lambda i: jnp.maximum(i - npc, 0)

    h2, pos, wts, cnt = pl.pallas_call(
        functools.partial(_route_kernel, n_prompt_chunks=npc),
        grid=(n_chunks,),
        in_specs=[
            pl.BlockSpec((gpc, r, d), lambda i: (pclamp(i), 0, 0)),
            pl.BlockSpec((gpc, r, d), lambda i: (sclamp(i), 0, 0)),
            pl.BlockSpec((1, 1, 2 * d), lambda i: (pclamp(i) // cpb, 0, 0)),
            pl.BlockSpec((gpc, 1, 2 * d), lambda i: (sclamp(i), 0, 0)),
            cst((1, d)), cst((N_EXPERTS, d)), cst((N_EXPERTS, 1)), cst((TC, TC)),
        ],
        out_specs=[
            pl.BlockSpec((TC, d), lambda i: (i, 0)),
            pl.BlockSpec((1, TOP_K, TC), lambda i: (i, 0, 0)),
            pl.BlockSpec((1, TOP_K, TC), lambda i: (i, 0, 0)),
            pl.BlockSpec((1, N_EXPERTS, LANE), lambda i: (i, 0, 0)),
        ],
        out_shape=[
            jax.ShapeDtypeStruct((n_tok, d), BF16),
            jax.ShapeDtypeStruct((n_chunks, TOP_K, TC), jnp.int32),
            jax.ShapeDtypeStruct((n_chunks, TOP_K, TC), F32),
            jax.ShapeDtypeStruct((n_chunks, N_EXPERTS, LANE), jnp.int32),
        ],
        compiler_params=arb1,
        name="moe_route",
    )(xp, xs, modp[:, :, 0:2 * d], mods_g[:, :, 0:2 * d], g_ffn, w_r_t, rb, tri)
    counts = cnt[:, :, 0]
    pos = pos.reshape(n_chunks, 1, TOP_K * TC)
    wts = wts.reshape(n_chunks, 1, TOP_K * TC)

    te, nt, ebase, cbase, etp, slot0, slot1, sel = pl.pallas_call(
        functools.partial(_plan_kernel, n_chunks=n_chunks, nt_max=nt_max),
        in_specs=[_smem_spec()],
        out_specs=[_smem_spec()] * 8,
        out_shape=[
            jax.ShapeDtypeStruct((nt_max,), jnp.int32),
            jax.ShapeDtypeStruct((1,), jnp.int32),
            jax.ShapeDtypeStruct((N_EXPERTS,), jnp.int32),
            jax.ShapeDtypeStruct((n_chunks * N_EXPERTS,), jnp.int32),
            jax.ShapeDtypeStruct((N_EXPERTS,), jnp.int32),
            jax.ShapeDtypeStruct((nt_max // 2,), jnp.int32),
            jax.ShapeDtypeStruct((nt_max // 2,), jnp.int32),
            jax.ShapeDtypeStruct((nt_max,), jnp.int32),
        ],
        name="moe_plan",
    )(counts)

    xsort = pl.pallas_call(
        functools.partial(_dispatch_kernel, n_chunks=n_chunks, nt_max=nt_max),
        grid=(n_chunks,),
        in_specs=[_smem_spec(), _smem_spec(), _smem_spec(), _smem_spec(), _smem_spec(),
                  pl.BlockSpec((1, 1, TOP_K * TC), lambda i: (i, 0, 0), memory_space=pltpu.SMEM),
                  pl.BlockSpec((TC, d), lambda i: (i, 0))],
        out_specs=pl.BlockSpec(memory_space=pl.ANY),
        out_shape=jax.ShapeDtypeStruct((sorted_rows, d), BF16),
        scratch_shapes=[pltpu.VMEM((PLANES * HP_STRIDE, LANE), F32),
                        pltpu.VMEM((PLANES * LOC_STRIDE, LANE), F32),
                        pltpu.VMEM((LOCAL_ROWS, d), BF16),
                        pltpu.VMEM((PAGE, d), BF16),
                        pltpu.VMEM((TILE_M, d), BF16),
                        pltpu.SMEM((2,), jnp.int32),
                        pltpu.SemaphoreType.DMA, pltpu.SemaphoreType.DMA, pltpu.SemaphoreType.DMA],
        compiler_params=arb1,
        name="moe_dispatch",
    )(counts, ebase, cbase, etp, nt, pos, h2)

    tile_map = lambda i, s0, s1, sl, nt_ref: (jnp.minimum(i, (nt_ref[0] - 1) // 2), 0)
    slot_maps = (lambda i, s0, s1, sl, nt_ref: (s0[i], 0, 0), lambda i, s0, s1, sl, nt_ref: (s1[i], 0, 0))
    w_specs = [pl.BlockSpec(shape, slot_maps[slot]) for slot in range(2)
               for shape in ((1, d, D_EXPERT), (1, d, D_EXPERT), (1, D_EXPERT, d))]
    eo = pl.pallas_call(
        _expert_kernel,
        grid_spec=pltpu.PrefetchScalarGridSpec(
            num_scalar_prefetch=4,
            grid=(nt_max // 2,),
            in_specs=[pl.BlockSpec((2 * TILE_M, d), tile_map)] + w_specs,
            out_specs=pl.BlockSpec((2 * TILE_M, d), tile_map),
            scratch_shapes=[pltpu.VMEM((2, d, D_EXPERT), BF16), pltpu.VMEM((2, d, D_EXPERT), BF16),
                            pltpu.VMEM((2, D_EXPERT, d), BF16)],
        ),
        out_shape=jax.ShapeDtypeStruct((sorted_rows, d), BF16),
        input_output_aliases={4: 0},
        compiler_params=arb1,
        name="moe_experts",
    )(slot0, slot1, sel, nt, xsort, weg, weu, wed, weg, weu, wed)

    nph = COMBINE_PHASES
    pstep = lambda i, j: jnp.minimum(i * nph + j, npc * nph - 1)
    sstep = lambda i, j: jnp.maximum((i - npc) * nph + j, 0)
    smem_chunk = pl.BlockSpec((1, 1, TOP_K * TC), lambda i, j: (i, 0, 0), memory_space=pltpu.SMEM)
    yp, ys = pl.pallas_call(
        functools.partial(_combine_kernel, n_prompt_chunks=npc, n_chunks=n_chunks),
        grid=(n_chunks, nph),
        in_specs=[_smem_spec(), _smem_spec(), _smem_spec(), smem_chunk, smem_chunk,
                  pl.BlockSpec((gpp, r, d), lambda i, j: (pstep(i, j), 0, 0)),
                  pl.BlockSpec((gpp, r, d), lambda i, j: (sstep(i, j), 0, 0)),
                  pl.BlockSpec((1, 1, d), lambda i, j: (pclamp(i) // cpb, 0, 0)),
                  pl.BlockSpec((gpp, 1, d), lambda i, j: (sstep(i, j), 0, 0)),
                  pl.BlockSpec((PHASE_ROWS, d), lambda i, j: (i * nph + j, 0)),
                  cst((1, d)), cst((d, D_EXPERT)), cst((d, D_EXPERT)), cst((D_EXPERT, d)),
                  pl.BlockSpec(memory_space=pl.ANY)],
        out_specs=[pl.BlockSpec((gpp, r, d), lambda i, j: (pstep(i, j), 0, 0)),
                   pl.BlockSpec((gpp, r, d), lambda i, j: (sstep(i, j), 0, 0))],
        out_shape=[jax.ShapeDtypeStruct(xp.shape, F32), jax.ShapeDtypeStruct(xs.shape, F32)],
        scratch_shapes=[pltpu.VMEM((PLANES * LOC_STRIDE, LANE), F32),
                        pltpu.VMEM((2, LOCAL_ROWS, d), BF16),
                        pltpu.VMEM((PLANES * Y_STRIDE, LANE), F32),
                        pltpu.SMEM((2,), jnp.int32),
                        pltpu.SemaphoreType.DMA, pltpu.SemaphoreType.DMA],
        compiler_params=arb2,
        name="moe_combine",
    )(counts, ebase, cbase, pos, wts, xp, xs, modp[:, :, 2 * d:3 * d], mods_g[:, :, 2 * d:3 * d],
      h2, g_final, wsg, wsu, wsd, eo)
    return yp.reshape(b, s_len, d), ys.reshape(db, ds, d)


def _t5_bucket(rel):
    nb = NUM_BUCKETS // 2
    max_exact = nb // 2
    ret = jnp.where(rel > 0, nb, 0)
    n = jnp.abs(rel)
    nf = jnp.maximum(n, 1).astype(F32)
    large = max_exact + (jnp.log(nf / max_exact) / math.log(REL_MAX_DIST / max_exact)
                         * (nb - max_exact)).astype(jnp.int32)
    large = jnp.minimum(large, nb - 1)
    return ret + jnp.where(n < max_exact, n, large)


def _pair_layout(per_head):
    groups = []
    for g in range(N_KV_HEADS):
        pars = []
        for par in range(2):
            pars.append(jnp.concatenate([per_head[4 * g + par], per_head[4 * g + par + 2]], axis=0))
        groups.append(jnp.stack(pars))
    return jnp.stack(groups)


REL_SPAN = 2 * LANE


def _rel_bias_kernel(tab_ref, o_ref):
    tab = tab_ref[...]
    for i in range(CHUNK):
        shift = CHUNK - 1 - i
        o_ref[i] = pltpu.roll(tab, (REL_SPAN - shift) % REL_SPAN, 1)[:, 0:BAND]


def _rel_bias(rel_table):
    rel = jnp.arange(REL_SPAN) - (WINDOW + CHUNK - 1)
    tab = rel_table[_t5_bucket(rel)].astype(F32).T
    out = pl.pallas_call(
        _rel_bias_kernel,
        in_specs=[pl.BlockSpec((N_HEADS, REL_SPAN), lambda: (0, 0))],
        out_specs=pl.BlockSpec((CHUNK, N_HEADS, BAND), lambda: (0, 0, 0)),
        out_shape=jax.ShapeDtypeStruct((CHUNK, N_HEADS, BAND), F32),
        name="rel_bias",
    )(tab)
    return jnp.transpose(out, (1, 0, 2))


def _pair_window_layout(bias):
    off = jnp.full((CHUNK, CHUNK), -1e30, F32)
    groups = []
    for g in range(N_KV_HEADS):
        pars = []
        for par in range(2):
            rows = []
            for hp in range(2):
                head = 4 * g + par + 2 * hp
                rows.append(jnp.concatenate([bias[head], off], axis=1))
                rows.append(jnp.concatenate([off, bias[head]], axis=1))
            pars.append(jnp.concatenate(rows, axis=0))
        groups.append(jnp.stack(pars))
    return jnp.stack(groups)


def kernel(x_prompt, x_sample, c_prompt, c_sample, cache_k, cache_v, state_pool, w_ada, b_ada, g_norm_mix, g_norm_ffn, w_in, w_pool_group, pool_scale, attn_sinks, w_pool_proj, w_attn_proj, w_out, rel_table, w_router, router_bias, w_exp_gate, w_exp_up, w_exp_down, w_sh_gate, w_sh_up, w_sh_down, g_final):
    b, s_len, d = x_prompt.shape
    db, ds, _ = x_sample.shape
    l = 0

    c_all = jnp.concatenate([c_prompt, c_sample, jnp.zeros((4, d), F32)], axis=0)
    mods = _ada(c_all, w_ada[l], b_ada[l])
    mod_p = mods[0:b].reshape(b, 1, 6 * d)
    mod_s = mods[b:b + db].reshape(db, 1, 6 * d)

    bias = _rel_bias(rel_table)
    bias4 = jnp.pad(_pair_layout(bias), ((0, 0), (0, 0), (0, 0), (0, PAIR_KEYS - BAND)), constant_values=-1e30)
    sink4 = _pair_layout(jnp.broadcast_to(attn_sinks[l].astype(F32)[:, None, None], (N_HEADS, CHUNK, 128)))
    w_pg = jnp.zeros((POOL_WIDTH, POOL_WIDTH), F32)
    for g in range(len(POOL_WINDOWS)):
        w_pg = w_pg.at[g * POOL_GW:(g + 1) * POOL_GW, g * POOL_GW:(g + 1) * POOL_GW].set(w_pool_group[l, g])
    w_pg = w_pg.astype(BF16)
    w_in_b = w_in[l].astype(BF16)
    w_pp = w_pool_proj[l].astype(BF16)
    w_ap = w_attn_proj[l].astype(BF16)
    w_out_b = w_out[l].astype(BF16)
    g_mix = g_norm_mix[l].reshape(1, d)
    g_ffn = g_norm_ffn[l].reshape(1, d)
    ps = pool_scale[l].reshape(1, POOL_WIDTH)

    bias_pw = jnp.swapaxes(_pair_window_layout(bias), 2, 3)
    sk = attn_sinks[l].astype(F32)
    sink_pw = jnp.concatenate([jnp.full((2 * CHUNK,), 1.0, F32) * sk[4 * g + par + 2 * hp]
                               for g in range(N_KV_HEADS) for par in range(2) for hp in range(2)]).reshape(1, -1)
    x1p, nk_p, nv_p, np_p = _mixer_prompt(x_prompt, mod_p, g_mix, w_in_b, w_pg, ps, bias_pw, sink_pw,
                                          w_pp, w_ap, w_out_b)
    ck = cache_k[l].reshape(db, WINDOW, KV_WIDTH)
    cv = cache_v[l].reshape(db, WINDOW, KV_WIDTH)
    sp16 = jnp.pad(state_pool[l], ((0, 0), (HIST - POOL_PAD, 0), (0, 0)))
    x1s, nk_s, nv_s, np_s = _mixer_sample(x_sample, mod_s, ck, cv, sp16, g_mix, w_in_b, w_pg, ps,
                                          bias4, sink4, w_pp, w_ap, w_out_b)

    y_p, y_s = _moe_sparse(x1p, x1s, mod_p[:, :, 3 * d:], mod_s[:, :, 3 * d:], g_ffn, g_final.reshape(1, d),
                           w_router[l].T.astype(BF16), router_bias[l].astype(F32).reshape(N_EXPERTS, 1),
                           w_sh_gate[l].astype(BF16), w_sh_up[l].astype(BF16), w_sh_down[l].astype(BF16),
                           w_exp_gate[l], w_exp_up[l], w_exp_down[l])

    kv_shape = lambda n: (1, n, WINDOW, N_KV_HEADS, HEAD_DIM)
    return (y_p, y_s,
            nk_p.reshape(kv_shape(b)), nv_p.reshape(kv_shape(b)),
            np_p[:, HIST - POOL_PAD:, :][None],
            nk_s.reshape(kv_shape(db)), nv_s.reshape(kv_shape(db)),
            np_s[:, HIST - POOL_PAD:, :][None])
```

```python
import functools
import math

import jax
import jax.numpy as jnp
from jax import lax
from jax.experimental import pallas as pl
from jax.experimental.pallas import tpu as pltpu

F32 = jnp.float32
BF16 = jnp.bfloat16

D_MODEL = 1024
CHUNK = 64
EPS = 1e-6
POOL_WIDTH = 512
POOL_WINDOWS = (2, 4, 8, 16)
POOL_GW = 128
POOL_PAD = 15
HEAD_DIM = 64
N_HEADS = 8
N_KV_HEADS = 2
ATTN_WIDTH = 512
KV_WIDTH = 128
WINDOW = 128
BAND = WINDOW + CHUNK
PAIR_KEYS = BAND + CHUNK
NUM_BUCKETS = 32
REL_MAX_DIST = 128
PAST_LEN = 4096
N_EXPERTS = 32
TOP_K = 4
N_GROUPS = 4
TOPK_GROUPS = 2
EXPERTS_PER_GROUP = 8
D_EXPERT = 256
ROUTED_SCALE = 2.5

OFF_U = 0
OFF_Q = OFF_U + POOL_WIDTH
OFF_K = OFF_Q + ATTN_WIDTH
OFF_V = OFF_K + KV_WIDTH
OFF_GP = OFF_V + KV_WIDTH
OFF_GA = OFF_GP + D_MODEL
IN_WIDTH = OFF_GA + D_MODEL

HIST = 16
MIX_ROWS = 512
MERGE_PANEL = 256
VMEM_LIMIT = 56 * 1024 * 1024

GROUP_ROWS = 64
TC = 1024
PAGE = 16
PAGE_SHIFT = 4
TILE_M = 512
PAGES_PER_TILE = TILE_M // PAGE
TILE_SHIFT = 5
LOCAL_ROWS = TOP_K * TC + N_EXPERTS * PAGE
LANE = 128
PLANES = D_MODEL // LANE
PLANE_PAD = 8
HP_STRIDE = TC + PLANE_PAD
LOC_STRIDE = LOCAL_ROWS + PLANE_PAD
COMBINE_PHASES = 4
PHASE_ROWS = TC // COMBINE_PHASES
RUN_COPY = 4
RUN_COPY_SHIFT = 2
Y_STRIDE = PHASE_ROWS + PLANE_PAD


def _dot(a, b):
    return jnp.dot(a, b, preferred_element_type=F32)


def _dot_nt(a, b):
    return lax.dot_general(a, b, (((1,), (1,)), ((), ())), preferred_element_type=F32)


def _norm_mod(x, g, shift, scale):
    ms = jnp.mean(x * x, axis=-1, keepdims=True)
    y = x * lax.rsqrt(ms + EPS) * g
    return y * (1.0 + scale) + shift


def _silu(x):
    return x * jax.nn.sigmoid(x)


def _ada_kernel(c_ref, w_ref, b_ref, o_ref):
    a = _silu(c_ref[...]).astype(BF16)
    o_ref[...] = _dot(a, w_ref[...].astype(BF16)) + b_ref[...]


def _ada(c_all, w_ada, b_ada):
    rows = c_all.shape[0]
    n = w_ada.shape[1]
    tn = 768
    return pl.pallas_call(
        _ada_kernel,
        grid=(n // tn,),
        in_specs=[
            pl.BlockSpec((rows, D_MODEL), lambda j: (0, 0)),
            pl.BlockSpec((D_MODEL, tn), lambda j: (0, j)),
            pl.BlockSpec((1, tn), lambda j: (0, j)),
        ],
        out_specs=pl.BlockSpec((rows, tn), lambda j: (0, j)),
        out_shape=jax.ShapeDtypeStruct((rows, n), F32),
        compiler_params=pltpu.CompilerParams(dimension_semantics=("arbitrary",)),
        name="ada",
    )(c_all, w_ada, b_ada.reshape(1, n))


def _pool_feats(ue, pos0, rows):
    s2 = ue + pltpu.roll(ue, 1, 0)
    s4 = s2 + pltpu.roll(s2, 2, 0)
    s8 = s4 + pltpu.roll(s4, 4, 0)
    s16 = s8 + pltpu.roll(s8, 8, 0)
    pos = pos0 + lax.broadcasted_iota(jnp.int32, (rows, POOL_GW), 0)
    outs = []
    for g, (w, sw) in enumerate(zip(POOL_WINDOWS, (s2, s4, s8, s16))):
        sl = slice(g * POOL_GW, (g + 1) * POOL_GW)
        cnt = jnp.minimum(pos + 1, w).astype(F32)
        outs.append(sw[HIST:, sl] / cnt - ue[HIST:, sl])
    return jnp.concatenate(outs, axis=1)


def _kv_variants(t):
    lane = lax.broadcasted_iota(jnp.int32, t.shape, 1)
    low = lane < HEAD_DIM
    swapped = pltpu.roll(t, HEAD_DIM, 1)
    zero = jnp.zeros_like(t)
    return (jnp.where(low, t, zero), jnp.where(low, zero, swapped),
            jnp.where(low, swapped, zero), jnp.where(low, zero, t))


def _merge_out(h, pool_y, attn_y, win_ref, wpp_ref, wap_ref, wout_ref, m_scr):
    rows = h.shape[0]
    pool_b = pool_y.astype(BF16)
    attn_b = attn_y.astype(BF16)
    for n in range(D_MODEL // MERGE_PANEL):
        lo, hi = n * MERGE_PANEL, (n + 1) * MERGE_PANEL
        gp = _dot(h, win_ref[:, OFF_GP + lo:OFF_GP + hi])
        ga = _dot(h, win_ref[:, OFF_GA + lo:OFF_GA + hi])
        pp = _dot(pool_b, wpp_ref[:, lo:hi])
        ap = _dot(attn_b, wap_ref[:, lo:hi])
        m_scr[0:rows, lo:hi] = (jax.nn.sigmoid(gp) * pp + jax.nn.sigmoid(ga) * ap).astype(BF16)
    return _dot(m_scr[0:rows, :], wout_ref[...])


def _mixer_prompt_kernel(x_ref, mod_ref, g_ref, win_ref, wpg_ref, ps_ref, bias_ref, sink_ref,
                         wpp_ref, wap_ref, wout_ref,
                         x1_ref, nk_ref, nv_ref, npool_ref,
                         uext, k0, k1, k2, k3, v0, v1, v2, v3, m_scr, s_scr, p_scr, a_scr):
    ts = MIX_ROWS
    s = pl.program_id(1)
    kext = (k0, k1, k2, k3)
    vext = (v0, v1, v2, v3)

    @pl.when(s == 0)
    def _():
        uext[0:HIST, :] = jnp.zeros((HIST, POOL_WIDTH), F32)
        for r in kext:
            r[0:WINDOW, :] = jnp.zeros((WINDOW, KV_WIDTH), BF16)
        for r in vext:
            r[:, 0:WINDOW] = jnp.zeros((KV_WIDTH, WINDOW), BF16)

    x = x_ref[0]
    shift = mod_ref[0, :, 0:D_MODEL]
    scale = mod_ref[0, :, D_MODEL:2 * D_MODEL]
    gate = mod_ref[0, :, 2 * D_MODEL:3 * D_MODEL]
    h = _norm_mod(x, g_ref[...], shift, scale).astype(BF16)

    u = _dot(h, win_ref[:, OFF_U:OFF_U + POOL_WIDTH])
    q = (_dot(h, win_ref[:, OFF_Q:OFF_Q + ATTN_WIDTH]) * (HEAD_DIM ** -0.5)).astype(BF16)
    kv = _dot(h, win_ref[:, OFF_K:OFF_V + KV_WIDTH])
    k = kv[:, 0:KV_WIDTH]
    v = kv[:, KV_WIDTH:2 * KV_WIDTH]
    nk_ref[0] = k[ts - WINDOW:ts, :]
    nv_ref[0] = v[ts - WINDOW:ts, :]
    npool_ref[0] = u[ts - HIST:ts, :]

    pos0 = s * ts
    uext[HIST:HIST + ts, :] = u
    p = _pool_feats(uext[...], pos0, ts)
    uext[0:HIST, :] = u[ts - HIST:ts, :]
    pool_y = _dot(p.astype(BF16), wpg_ref[...]) * ps_ref[...]

    for r, t in zip(kext, _kv_variants(k)):
        r[WINDOW:WINDOW + ts, :] = t.astype(BF16)
    for r, t in zip(vext, _kv_variants(v)):
        r[:, WINDOW:WINDOW + ts] = t.T.astype(BF16)
    pair_rows = 2 * CHUNK
    blk_q = 2 * pair_rows
    n_pairs = ts // pair_rows
    per_pair = N_KV_HEADS * 2
    key_i = lax.broadcasted_iota(jnp.int32, (PAIR_KEYS, blk_q), 0)
    for pr in range(n_pairs):
        lo = pr * pair_rows
        valid = (pos0 + lo + key_i) >= WINDOW
        for g in range(N_KV_HEADS):
            qs = jnp.concatenate([q[lo:lo + pair_rows, (2 * g) * LANE:(2 * g + 1) * LANE],
                                  q[lo:lo + pair_rows, (2 * g + 1) * LANE:(2 * g + 2) * LANE]], axis=0)
            for par in range(2):
                s = _dot_nt(kext[2 * g + par][lo:lo + PAIR_KEYS, :], qs) + bias_ref[g, par]
                s_scr[(pr * N_KV_HEADS + g) * 2 + par] = jnp.where(valid, s, -1e30)
    sink = sink_ref[...]
    for pr in range(n_pairs):
        s = jnp.concatenate([s_scr[pr * per_pair + b] for b in range(per_pair)], axis=1)
        m = jnp.maximum(jnp.max(s, axis=0, keepdims=True), sink)
        e = jnp.exp(s - m)
        denom = jnp.sum(e, axis=0, keepdims=True) + jnp.exp(sink - m)
        p = (e / denom).astype(BF16)
        for b in range(per_pair):
            p_scr[pr * per_pair + b] = p[:, b * blk_q:(b + 1) * blk_q]
    for pr in range(n_pairs):
        lo = pr * pair_rows
        for g in range(N_KV_HEADS):
            blk = (pr * N_KV_HEADS + g) * 2
            ot = (_dot(vext[2 * g][:, lo:lo + PAIR_KEYS], p_scr[blk])
                  + _dot(vext[2 * g + 1][:, lo:lo + PAIR_KEYS], p_scr[blk + 1]))
            o = ot.T
            a_scr[lo:lo + pair_rows, (2 * g) * LANE:(2 * g + 1) * LANE] = o[0:pair_rows].astype(BF16)
            a_scr[lo:lo + pair_rows, (2 * g + 1) * LANE:(2 * g + 2) * LANE] = o[pair_rows:].astype(BF16)
    for r in kext:
        r[0:WINDOW, :] = r[ts:ts + WINDOW, :]
    for r in vext:
        r[:, 0:WINDOW] = r[:, ts:ts + WINDOW]

    y = _merge_out(h, pool_y, a_scr[...], win_ref, wpp_ref, wap_ref, wout_ref, m_scr)
    x1_ref[0] = x + gate * y


def _const_spec(shape):
    nd = len(shape)
    return pl.BlockSpec(shape, lambda *_: (0,) * nd)


def _mixer_prompt(x, mod, g_mix, w_in, w_pg, pool_scale, bias4, sink4, w_pp, w_ap, w_out):
    b, s_len, d = x.shape
    ts = MIX_ROWS
    pair_rows = 2 * CHUNK
    n_blocks = (ts // pair_rows) * N_KV_HEADS * 2
    kv_scratch = ([pltpu.VMEM((WINDOW + ts, KV_WIDTH), BF16) for _ in range(4)]
                  + [pltpu.VMEM((KV_WIDTH, WINDOW + ts), BF16) for _ in range(4)])
    return pl.pallas_call(
        _mixer_prompt_kernel,
        grid=(b, s_len // ts),
        in_specs=[
            pl.BlockSpec((1, ts, d), lambda i, j: (i, j, 0)),
            pl.BlockSpec((1, 1, 6 * d), lambda i, j: (i, 0, 0)),
            _const_spec((1, d)),
            _const_spec((d, IN_WIDTH)),
            _const_spec((POOL_WIDTH, POOL_WIDTH)),
            _const_spec((1, POOL_WIDTH)),
            _const_spec(bias4.shape),
            _const_spec(sink4.shape),
            _const_spec((POOL_WIDTH, d)),
            _const_spec((ATTN_WIDTH, d)),
            _const_spec((d, d)),
        ],
        out_specs=[
            pl.BlockSpec((1, ts, d), lambda i, j: (i, j, 0)),
            pl.BlockSpec((1, WINDOW, KV_WIDTH), lambda i, j: (i, 0, 0)),
            pl.BlockSpec((1, WINDOW, KV_WIDTH), lambda i, j: (i, 0, 0)),
            pl.BlockSpec((1, HIST, POOL_WIDTH), lambda i, j: (i, 0, 0)),
        ],
        out_shape=[
            jax.ShapeDtypeStruct((b, s_len, d), F32),
            jax.ShapeDtypeStruct((b, WINDOW, KV_WIDTH), F32),
            jax.ShapeDtypeStruct((b, WINDOW, KV_WIDTH), F32),
            jax.ShapeDtypeStruct((b, HIST, POOL_WIDTH), F32),
        ],
        scratch_shapes=[pltpu.VMEM((HIST + ts, POOL_WIDTH), F32)] + kv_scratch
                       + [pltpu.VMEM((ts, d), BF16),
                          pltpu.VMEM((n_blocks, 2 * pair_rows, PAIR_KEYS), F32),
                          pltpu.VMEM((n_blocks, 2 * pair_rows, PAIR_KEYS), BF16),
                          pltpu.VMEM((ts, ATTN_WIDTH), BF16)],
        compiler_params=pltpu.CompilerParams(
            dimension_semantics=("arbitrary", "arbitrary"), vmem_limit_bytes=VMEM_LIMIT),
        name="mixer_prompt",
    )(x, mod, g_mix, w_in, w_pg, pool_scale, bias4, sink4, w_pp, w_ap, w_out)


def _mixer_sample_kernel(x_ref, mod_ref, ck_ref, cv_ref, sp_ref, g_ref, win_ref, wpg_ref, ps_ref,
                         bias_ref, sink_ref, wpp_ref, wap_ref, wout_ref,
                         x1_ref, nk_ref, nv_ref, npool_ref, m_scr, ks, vs, s_scr, p_scr, a_scr):
    nseq, ds, d = x_ref.shape
    rows = nseq * ds
    x = x_ref[...]
    shift = mod_ref[:, :, 0:D_MODEL]
    scale = mod_ref[:, :, D_MODEL:2 * D_MODEL]
    gate = mod_ref[:, :, 2 * D_MODEL:3 * D_MODEL]
    h = _norm_mod(x, g_ref[...], shift, scale).reshape(rows, d).astype(BF16)

    u = _dot(h, win_ref[:, OFF_U:OFF_U + POOL_WIDTH])
    q = (_dot(h, win_ref[:, OFF_Q:OFF_Q + ATTN_WIDTH]) * (HEAD_DIM ** -0.5)).astype(BF16)
    kv = _dot(h, win_ref[:, OFF_K:OFF_V + KV_WIDTH])
    k = kv[:, 0:KV_WIDTH]
    v = kv[:, KV_WIDTH:2 * KV_WIDTH]

    pool_parts = []
    key_pad = jnp.zeros((PAIR_KEYS - BAND, KV_WIDTH), F32)
    for b in range(nseq):
        lo = b * ds
        ub = u[lo:lo + ds, :]
        kb = k[lo:lo + ds, :]
        vb = v[lo:lo + ds, :]
        ck = ck_ref[b]
        cv = cv_ref[b]
        nk_ref[b] = jnp.concatenate([ck[ds:WINDOW, :], kb], axis=0)
        nv_ref[b] = jnp.concatenate([cv[ds:WINDOW, :], vb], axis=0)
        npool_ref[b] = ub[ds - HIST:ds, :]
        ue = jnp.concatenate([sp_ref[b], ub], axis=0)
        pool_parts.append(_pool_feats(ue, PAST_LEN, ds))
        for var, t in enumerate(_kv_variants(jnp.concatenate([ck, kb, key_pad], axis=0))):
            ks[var, b] = t.astype(BF16)
        for var, t in enumerate(_kv_variants(jnp.concatenate([cv, vb, key_pad], axis=0))):
            vs[var, b] = t.astype(BF16)
    p = jnp.concatenate(pool_parts, axis=0)
    pool_y = _dot(p.astype(BF16), wpg_ref[...]) * ps_ref[...]

    per_seq = N_KV_HEADS * 2
    for b in range(nseq):
        lo = b * ds
        for g in range(N_KV_HEADS):
            qs = jnp.concatenate([q[lo:lo + ds, (2 * g) * LANE:(2 * g + 1) * LANE],
                                  q[lo:lo + ds, (2 * g + 1) * LANE:(2 * g + 2) * LANE]], axis=0)
            for par in range(2):
                s_scr[b * per_seq + 2 * g + par] = _dot_nt(qs, ks[2 * g + par, b]) + bias_ref[g, par]
    group = 2
    grp_rows = group * per_seq * 2 * ds
    sink1 = sink_ref[...].reshape(per_seq * 2 * ds, LANE)[:, 0:1]
    sink = jnp.concatenate([sink1] * group, axis=0)
    for gi in range(nseq // group):
        blks = slice(gi * group * per_seq, (gi + 1) * group * per_seq)
        s = s_scr[blks].reshape(grp_rows, PAIR_KEYS)
        m = jnp.maximum(jnp.max(s, axis=-1, keepdims=True), sink)
        e = jnp.exp(s - m)
        esum = _dot(e.astype(BF16), jnp.ones((PAIR_KEYS, LANE), BF16))
        denom = esum + jnp.exp(sink - m)
        p_scr[blks] = ((e / jnp.concatenate([denom, denom], axis=1)).astype(BF16)
                       .reshape(group * per_seq, 2 * ds, PAIR_KEYS))
    for b in range(nseq):
        lo = b * ds
        for g in range(N_KV_HEADS):
            blk = b * per_seq + 2 * g
            o = _dot(p_scr[blk], vs[2 * g, b]) + _dot(p_scr[blk + 1], vs[2 * g + 1, b])
            a_scr[lo:lo + ds, (2 * g) * LANE:(2 * g + 1) * LANE] = o[0:ds].astype(BF16)
            a_scr[lo:lo + ds, (2 * g + 1) * LANE:(2 * g + 2) * LANE] = o[ds:].astype(BF16)

    y = _merge_out(h, pool_y, a_scr[...], win_ref, wpp_ref, wap_ref, wout_ref, m_scr)
    x1_ref[...] = x + gate * y.reshape(nseq, ds, d)


def _mixer_sample(x, mod, ck, cv, sp16, g_mix, w_in, w_pg, pool_scale, bias4, sink4, w_pp, w_ap, w_out):
    db, ds, d = x.shape
    nseq = MIX_ROWS // ds
    seq_spec = lambda shape: pl.BlockSpec((nseq,) + shape, lambda i: (i, 0, 0))
    return pl.pallas_call(
        _mixer_sample_kernel,
        grid=(db // nseq,),
        in_specs=[
            seq_spec((ds, d)),
            seq_spec((1, 6 * d)),
            seq_spec((WINDOW, KV_WIDTH)),
            seq_spec((WINDOW, KV_WIDTH)),
            seq_spec((HIST, POOL_WIDTH)),
            _const_spec((1, d)),
            _const_spec((d, IN_WIDTH)),
            _const_spec((POOL_WIDTH, POOL_WIDTH)),
            _const_spec((1, POOL_WIDTH)),
            _const_spec(bias4.shape),
            _const_spec(sink4.shape),
            _const_spec((POOL_WIDTH, d)),
            _const_spec((ATTN_WIDTH, d)),
            _const_spec((d, d)),
        ],
        out_specs=[
            seq_spec((ds, d)),
            seq_spec((WINDOW, KV_WIDTH)),
            seq_spec((WINDOW, KV_WIDTH)),
            seq_spec((HIST, POOL_WIDTH)),
        ],
        out_shape=[
            jax.ShapeDtypeStruct((db, ds, d), F32),
            jax.ShapeDtypeStruct((db, WINDOW, KV_WIDTH), F32),
            jax.ShapeDtypeStruct((db, WINDOW, KV_WIDTH), F32),
            jax.ShapeDtypeStruct((db, HIST, POOL_WIDTH), F32),
        ],
        scratch_shapes=[pltpu.VMEM((MIX_ROWS, d), BF16),
                        pltpu.VMEM((4, nseq, PAIR_KEYS, KV_WIDTH), BF16),
                        pltpu.VMEM((4, nseq, PAIR_KEYS, KV_WIDTH), BF16),
                        pltpu.VMEM((nseq * N_KV_HEADS * 2, 2 * ds, PAIR_KEYS), F32),
                        pltpu.VMEM((nseq * N_KV_HEADS * 2, 2 * ds, PAIR_KEYS), BF16),
                        pltpu.VMEM((MIX_ROWS, ATTN_WIDTH), BF16)],
        compiler_params=pltpu.CompilerParams(
            dimension_semantics=("arbitrary",), vmem_limit_bytes=VMEM_LIMIT),
        name="mixer_sample",
    )(x, mod, ck, cv, sp16, g_mix, w_in, w_pg, pool_scale, bias4, sink4, w_pp, w_ap, w_out)


def _first_argmax(vals, idx, size):
    m = jnp.max(vals, axis=0, keepdims=True)
    first = jnp.min(jnp.where(vals == m, idx, size), axis=0, keepdims=True)
    return m, first


def _router_picks(h, wr_ref, rb_ref):
    t = h.shape[0]
    logits = _dot_nt(wr_ref[...], h)
    scores = jax.nn.sigmoid(logits)
    sel = scores + rb_ref[...]
    neg = jnp.float32(-jnp.inf)
    eidx = lax.broadcasted_iota(jnp.int32, (N_EXPERTS, t), 0)

    gscores = []
    bidx = lax.broadcasted_iota(jnp.int32, (EXPERTS_PER_GROUP, t), 0)
    for g in range(N_GROUPS):
        blk = sel[g * EXPERTS_PER_GROUP:(g + 1) * EXPERTS_PER_GROUP, :]
        m1, i1 = _first_argmax(blk, bidx, EXPERTS_PER_GROUP)
        m2 = jnp.max(jnp.where(bidx == i1, neg, blk), axis=0, keepdims=True)
        gscores.append(m1 + m2)
    gs = jnp.concatenate(gscores, axis=0)
    gidx = lax.broadcasted_iota(jnp.int32, (N_GROUPS, t), 0)
    _, g1 = _first_argmax(gs, gidx, N_GROUPS)
    _, g2 = _first_argmax(jnp.where(gidx == g1, neg, gs), gidx, N_GROUPS)
    egrp = eidx // EXPERTS_PER_GROUP
    cand = jnp.where((egrp == g1) | (egrp == g2), sel, neg)

    picks = []
    for _ in range(TOP_K):
        _, i = _first_argmax(cand, eidx, N_EXPERTS)
        hit = eidx == i
        picks.append(hit)
        cand = jnp.where(hit, neg, cand)
    return picks, scores


def _swiglu_rows(h, wg, wu, wd):
    act = _silu(_dot(h, wg)) * _dot(h, wu)
    return _dot(act.astype(BF16), wd)


def _route_kernel(xp_ref, xs_ref, modp_ref, mods_ref, g_ref, wr_ref, rb_ref, tri_ref,
                  h_ref, pos_ref, w_ref, cnt_ref, *, n_prompt_chunks):
    is_prompt = pl.program_id(0) < n_prompt_chunks
    x = jnp.where(is_prompt, xp_ref[...], xs_ref[...])
    shift = jnp.where(is_prompt, modp_ref[:, :, 0:D_MODEL], mods_ref[:, :, 0:D_MODEL])
    scale = jnp.where(is_prompt, modp_ref[:, :, D_MODEL:2 * D_MODEL], mods_ref[:, :, D_MODEL:2 * D_MODEL])
    h = _norm_mod(x, g_ref[...], shift, scale).reshape(TC, D_MODEL).astype(BF16)
    h_ref[...] = h
    picks, scores = _router_picks(h, wr_ref, rb_ref)
    picked = picks[0] | picks[1] | picks[2] | picks[3]
    cnt = _dot(jnp.where(picked, 1.0, 0.0).astype(BF16), tri_ref[...])
    n_e = jnp.max(cnt, axis=1, keepdims=True).astype(jnp.int32)
    n_b = jnp.broadcast_to(n_e, (N_EXPERTS, LANE))
    cnt_ref[0] = n_b
    padded = lax.shift_left(lax.shift_right_logical(n_b + (PAGE - 1), PAGE_SHIFT), PAGE_SHIFT)
    row = lax.broadcasted_iota(jnp.int32, (N_EXPERTS, LANE), 0)
    scan = padded
    for s in (1, 2, 4, 8, 16):
        scan = scan + jnp.where(row >= s, pltpu.roll(scan, s, 0), 0)
    base = (scan - padded)[:, 0:1].astype(F32)
    slot = base + cnt - 1.0
    pos = [jnp.sum(jnp.where(p, slot, 0.0), axis=0, keepdims=True) for p in picks]
    wk = [jnp.sum(jnp.where(p, scores, 0.0), axis=0, keepdims=True) for p in picks]
    wsum = wk[0] + wk[1] + wk[2] + wk[3]
    pos_ref[0] = jnp.concatenate(pos, axis=0).astype(jnp.int32)
    w_ref[0] = jnp.concatenate(wk, axis=0) / wsum * ROUTED_SCALE


def _plan_kernel(cnt_ref, te_ref, nt_ref, ebase_ref, cbase_ref, etp_ref, slot0_ref, slot1_ref, sel_ref,
                 *, n_chunks, nt_max):
    def per_expert(e, tile0):
        def per_chunk(c, acc):
            cbase_ref[c * N_EXPERTS + e] = acc
            return acc + lax.shift_right_logical(cnt_ref[c, e] + (PAGE - 1), PAGE_SHIFT)
        tp = lax.fori_loop(0, n_chunks, per_chunk, 0)
        etp_ref[e] = tp
        ebase_ref[e] = tile0 * PAGES_PER_TILE
        nt = lax.shift_right_logical(tp + (PAGES_PER_TILE - 1), TILE_SHIFT)

        def fill(j, carry):
            te_ref[tile0 + j] = e
            return carry
        lax.fori_loop(0, nt, fill, 0)
        return tile0 + nt
    total = lax.fori_loop(0, N_EXPERTS, per_expert, 0)
    nt_ref[0] = total

    def tail(j, carry):
        te_ref[j] = N_EXPERTS - 1
        return carry
    lax.fori_loop(total, nt_max, tail, 0)

    def per_step(i, carry):
        c0, c1 = carry
        used = 2 * i < total
        a = te_ref[jnp.minimum(2 * i, total - 1)]
        b = te_ref[jnp.minimum(2 * i + 1, total - 1)]
        hit_a = (a == c0) | (a == c1)
        evict0 = hit_a | (c0 != b)
        n0 = jnp.where(used & ~hit_a & evict0, a, c0)
        n1 = jnp.where(used & ~hit_a & ~evict0, a, c1)
        sa = jnp.where(a == n0, 0, 1)
        hit_b = (b == n0) | (b == n1)
        m0 = jnp.where(used & ~hit_b & (sa == 1), b, n0)
        m1 = jnp.where(used & ~hit_b & (sa == 0), b, n1)
        slot0_ref[i] = jnp.maximum(m0, 0)
        slot1_ref[i] = jnp.maximum(m1, 0)
        sel_ref[2 * i] = sa
        sel_ref[2 * i + 1] = jnp.where(b == m0, 0, 1)
        return (m0, m1)
    lax.fori_loop(0, nt_max // 2, per_step, (-1, -1))


def _page_copy(src, src_page, dst, dst_page, sem, pages=1):
    rows = pages * PAGE
    s = src.at[pl.ds(pl.multiple_of(src_page * PAGE, PAGE), rows), :]
    t = dst.at[pl.ds(pl.multiple_of(dst_page * PAGE, PAGE), rows), :]
    return pltpu.make_async_copy(s, t, sem)


def _for_each_run(c, cnt_ref, ebase_ref, cbase_ref, fn):
    def per_expert(e, carry):
        lpage, nbig, nsmall = carry
        n = cnt_ref[c, e]
        npages = lax.shift_right_logical(n + (PAGE - 1), PAGE_SHIFT)
        fn(n, lpage, ebase_ref[e] + cbase_ref[c * N_EXPERTS + e], npages)
        return (lpage + npages, nbig + lax.shift_right_logical(npages, RUN_COPY_SHIFT),
                nsmall + (npages & (RUN_COPY - 1)))
    return lax.fori_loop(0, N_EXPERTS, per_expert, (0, 0, 0))


def _start_run_copies(src, src_page, dst, dst_page, npages, sem_big, sem_small):
    nbig = lax.shift_right_logical(npages, RUN_COPY_SHIFT)

    def big(j, carry):
        _page_copy(src, src_page + j * RUN_COPY, dst, dst_page + j * RUN_COPY, sem_big, RUN_COPY).start()
        return carry
    lax.fori_loop(0, nbig, big, 0)

    def small(j, carry):
        _page_copy(src, src_page + j, dst, dst_page + j, sem_small).start()
        return carry
    lax.fori_loop(nbig * RUN_COPY, npages, small, 0)


def _wait_copies(src, dst, nbig, nsmall, sem_big, sem_small):
    def big(j, carry):
        _page_copy(src, 0, dst, 0, sem_big, RUN_COPY).wait()
        return carry
    lax.fori_loop(0, nbig, big, 0)

    def small(j, carry):
        _page_copy(src, 0, dst, 0, sem_small).wait()
        return carry
    lax.fori_loop(0, nsmall, small, 0)


def _row_slab(ref, row, stride):
    return ref.at[pl.ds(row, PLANES, stride=stride), :]


def _dispatch_kernel(cnt_ref, ebase_ref, cbase_ref, etp_ref, nt_ref, pos_ref, h_ref, xs_hbm,
                     hpl, loc, locb, zpage, ztile, pend, sem, bsem, tsem, *, n_chunks, nt_max):
    c = pl.program_id(0)
    half = c % 2
    buf = locb.at[half]
    bsem_h = bsem.at[half]
    sem_h = sem.at[half]

    @pl.when(c == 0)
    def _():
        for j in range(4):
            pend[j] = 0
        loc[...] = jnp.zeros(loc.shape, F32)

    h = h_ref[...]
    for ch in range(PLANES):
        hpl[ch * HP_STRIDE:ch * HP_STRIDE + TC, :] = h[:, ch * LANE:(ch + 1) * LANE].astype(F32)
    zpage[...] = jnp.zeros(zpage.shape, BF16)
    ztile[...] = jnp.zeros(ztile.shape, BF16)

    def scatter(t8, carry):
        for u in range(8):
            t = t8 * 8 + u
            slab = _row_slab(hpl, t, HP_STRIDE)[...]
            for k in range(TOP_K):
                _row_slab(loc, pos_ref[0, 0, k * TC + t], LOC_STRIDE)[...] = slab
        return carry
    lax.fori_loop(0, TC // 8, scatter, 0)

    _wait_copies(buf, xs_hbm, pend[2 * half], pend[2 * half + 1], bsem_h, sem_h)
    for ch in range(PLANES):
        buf[:, ch * LANE:(ch + 1) * LANE] = loc[ch * LOC_STRIDE:ch * LOC_STRIDE + LOCAL_ROWS, :].astype(BF16)

    def send_run(n, lpage, gpage, npages):
        _start_run_copies(buf, lpage, xs_hbm, gpage, npages, bsem_h, sem_h)
    _, n_big, n_small = _for_each_run(c, cnt_ref, ebase_ref, cbase_ref, send_run)
    pend[2 * half] = n_big
    pend[2 * half + 1] = n_small

    @pl.when(c == n_chunks - 1)
    def _():
        def pad_expert(e, n):
            tp = etp_ref[e]
            full = lax.shift_left(lax.shift_right_logical(tp + (PAGES_PER_TILE - 1), TILE_SHIFT), TILE_SHIFT)

            def per_page(j, carry):
                _page_copy(zpage, 0, xs_hbm, ebase_ref[e] + j, sem_h).start()
                return carry
            lax.fori_loop(tp, full, per_page, 0)
            return n + full - tp
        n_pad = lax.fori_loop(0, N_EXPERTS, pad_expert, 0)

        def tile_copy(i):
            dst = xs_hbm.at[pl.ds(pl.multiple_of(i * TILE_M, TILE_M), TILE_M), :]
            return pltpu.make_async_copy(ztile, dst, tsem)

        def tail_start(i, carry):
            tile_copy(i).start()
            return carry
        lax.fori_loop(nt_ref[0], nt_max, tail_start, 0)
        _wait_copies(buf, xs_hbm, n_big, n_small + n_pad, bsem_h, sem_h)
        other = 1 - half
        _wait_copies(buf, xs_hbm, pend[2 * other], pend[2 * other + 1], bsem.at[other], sem.at[other])

        def tail_wait(i, carry):
            tile_copy(i).wait()
            return carry
        lax.fori_loop(nt_ref[0], nt_max, tail_wait, 0)


def _expert_kernel(slot0_ref, slot1_ref, sel_ref, nt_ref, xs_ref, wg0_ref, wu0_ref, wd0_ref,
                   wg1_ref, wu1_ref, wd1_ref, o_ref, wgb, wub, wdb):
    i = pl.program_id(0)
    slots = ((slot0_ref, wg0_ref, wu0_ref, wd0_ref), (slot1_ref, wg1_ref, wu1_ref, wd1_ref))

    @pl.when(2 * i < nt_ref[0])
    def _():
        for slot, (idx_ref, wg_ref, wu_ref, wd_ref) in enumerate(slots):
            @pl.when((i == 0) | (idx_ref[i] != idx_ref[jnp.maximum(i - 1, 0)]))
            def _():
                wgb[slot] = wg_ref[0].astype(BF16)
                wub[slot] = wu_ref[0].astype(BF16)
                wdb[slot] = wd_ref[0].astype(BF16)
        for t in range(2):
            rows = slice(t * TILE_M, (t + 1) * TILE_M)
            sl = sel_ref[2 * i + t]
            o_ref[rows, :] = _swiglu_rows(xs_ref[rows, :], wgb[sl], wub[sl], wdb[sl]).astype(BF16)


def _combine_kernel(cnt_ref, ebase_ref, cbase_ref, pos_ref, w_ref, xp_ref, xs_ref, gp_ref, gs_ref,
                    h_ref, gf_ref, wsg_ref, wsu_ref, wsd_ref, eo_hbm, yp_ref, ys_ref,
                    loc, locb, ypl, pend, sem, bsem, *, n_prompt_chunks, n_chunks):
    c = pl.program_id(0)
    ph = pl.program_id(1)

    def fetch(chunk):
        buf = locb.at[chunk % 2]

        def fetch_run(n, lpage, gpage, npages):
            _start_run_copies(eo_hbm, gpage, buf, lpage, npages, bsem, sem)
        _, n_big, n_small = _for_each_run(chunk, cnt_ref, ebase_ref, cbase_ref, fetch_run)
        pend[0] = n_big
        pend[1] = n_small

    @pl.when((c == 0) & (ph == 0))
    def _():
        locb[...] = jnp.zeros(locb.shape, BF16)
        fetch(c)

    @pl.when(ph == 0)
    def _():
        cur = locb.at[c % 2]
        _wait_copies(eo_hbm, cur, pend[0], pend[1], bsem, sem)

        @pl.when(c + 1 < n_chunks)
        def _():
            fetch(c + 1)
        for ch in range(PLANES):
            loc[ch * LOC_STRIDE:ch * LOC_STRIDE + LOCAL_ROWS, :] = cur[:, ch * LANE:(ch + 1) * LANE].astype(F32)

    shared = _swiglu_rows(h_ref[...], wsg_ref[...], wsu_ref[...], wsd_ref[...])
    t0 = ph * PHASE_ROWS

    def gather(t8, carry):
        for u in range(8):
            t = t8 * 8 + u
            acc = jnp.zeros((PLANES, LANE), F32)
            for k in range(TOP_K):
                idx = k * TC + t0 + t
                acc = acc + w_ref[0, 0, idx] * _row_slab(loc, pos_ref[0, 0, idx], LOC_STRIDE)[...]
            _row_slab(ypl, t, Y_STRIDE)[...] = acc
        return carry
    lax.fori_loop(0, PHASE_ROWS // 8, gather, 0)
    routed = jnp.concatenate([ypl[ch * Y_STRIDE:ch * Y_STRIDE + PHASE_ROWS, :] for ch in range(PLANES)], axis=1)
    f = shared + routed

    def finish(x_ref, gate_ref, y_ref):
        xo = x_ref[...] + gate_ref[...] * f.reshape(x_ref.shape)
        ms = jnp.mean(xo * xo, axis=-1, keepdims=True)
        y_ref[...] = xo * lax.rsqrt(ms + EPS) * gf_ref[...]

    @pl.when(c < n_prompt_chunks)
    def _():
        finish(xp_ref, gp_ref, yp_ref)

    @pl.when(c >= n_prompt_chunks)
    def _():
        finish(xs_ref, gs_ref, ys_ref)


def _smem_spec():
    return pl.BlockSpec(memory_space=pltpu.SMEM)


def _moe_sparse(x1p, x1s, modp, mods, g_ffn, g_final, w_r_t, rb, wsg, wsu, wsd, weg, weu, wed):
    b, s_len, d = x1p.shape
    db, ds, _ = x1s.shape
    r = GROUP_ROWS
    xp = x1p.reshape(b * s_len // r, r, d)
    xs = x1s.reshape(db * ds // r, r, d)
    mods_g = jnp.repeat(mods, ds // r, axis=0) if ds != r else mods
    npc = b * s_len // TC
    nsc = db * ds // TC
    n_chunks = npc + nsc
    n_tok = n_chunks * TC
    cpb = s_len // TC
    gpc = TC // r
    gpp = PHASE_ROWS // r
    max_pages = (TOP_K * n_tok) // PAGE + n_chunks * N_EXPERTS
    nt_max = -(-(max_pages // PAGES_PER_TILE + N_EXPERTS) // 8) * 8
    sorted_rows = nt_max * TILE_M
    arb1 = pltpu.CompilerParams(dimension_semantics=("arbitrary",), vmem_limit_bytes=VMEM_LIMIT)
    arb2 = pltpu.CompilerParams(dimension_semantics=("arbitrary", "arbitrary"), vmem_limit_bytes=VMEM_LIMIT)
    cst = lambda shape: pl.BlockSpec(shape, lambda *_: (0,) * len(shape))
    tri = jnp.triu(jnp.ones((TC, TC), BF16))
    pclamp = lambda i: jnp.minimum(i, npc - 1)
    sclamp = lambda i: jnp.maximum(i - npc, 0)

    h2, pos, wts, cnt = pl.pallas_call(
        functools.partial(_route_kernel, n_prompt_chunks=npc),
        grid=(n_chunks,),
        in_specs=[
            pl.BlockSpec((gpc, r, d), lambda i: (pclamp(i), 0, 0)),
            pl.BlockSpec((gpc, r, d), lambda i: (sclamp(i), 0, 0)),
            pl.BlockSpec((1, 1, 2 * d), lambda i: (pclamp(i) // cpb, 0, 0)),
            pl.BlockSpec((gpc, 1, 2 * d), lambda i: (sclamp(i), 0, 0)),
            cst((1, d)), cst((N_EXPERTS, d)), cst((N_EXPERTS, 1)), cst((TC, TC)),
        ],
        out_specs=[
            pl.BlockSpec((TC, d), lambda i: (i, 0)),
            pl.BlockSpec((1, TOP_K, TC), lambda i: (i, 0, 0)),
            pl.BlockSpec((1, TOP_K, TC), lambda i: (i, 0, 0)),
            pl.BlockSpec((1, N_EXPERTS, LANE), lambda i: (i, 0, 0)),
        ],
        out_shape=[
            jax.ShapeDtypeStruct((n_tok, d), BF16),
            jax.ShapeDtypeStruct((n_chunks, TOP_K, TC), jnp.int32),
            jax.ShapeDtypeStruct((n_chunks, TOP_K, TC), F32),
            jax.ShapeDtypeStruct((n_chunks, N_EXPERTS, LANE), jnp.int32),
        ],
        compiler_params=arb1,
        name="moe_route",
    )(xp, xs, modp[:, :, 0:2 * d], mods_g[:, :, 0:2 * d], g_ffn, w_r_t, rb, tri)
    counts = cnt[:, :, 0]
    pos = pos.reshape(n_chunks, 1, TOP_K * TC)
    wts = wts.reshape(n_chunks, 1, TOP_K * TC)

    te, nt, ebase, cbase, etp, slot0, slot1, sel = pl.pallas_call(
        functools.partial(_plan_kernel, n_chunks=n_chunks, nt_max=nt_max),
        in_specs=[_smem_spec()],
        out_specs=[_smem_spec()] * 8,
        out_shape=[
            jax.ShapeDtypeStruct((nt_max,), jnp.int32),
            jax.ShapeDtypeStruct((1,), jnp.int32),
            jax.ShapeDtypeStruct((N_EXPERTS,), jnp.int32),
            jax.ShapeDtypeStruct((n_chunks * N_EXPERTS,), jnp.int32),
            jax.ShapeDtypeStruct((N_EXPERTS,), jnp.int32),
            jax.ShapeDtypeStruct((nt_max // 2,), jnp.int32),
            jax.ShapeDtypeStruct((nt_max // 2,), jnp.int32),
            jax.ShapeDtypeStruct((nt_max,), jnp.int32),
        ],
        name="moe_plan",
    )(counts)

    xsort = pl.pallas_call(
        functools.partial(_dispatch_kernel, n_chunks=n_chunks, nt_max=nt_max),
        grid=(n_chunks,),
        in_specs=[_smem_spec(), _smem_spec(), _smem_spec(), _smem_spec(), _smem_spec(),
                  pl.BlockSpec((1, 1, TOP_K * TC), lambda i: (i, 0, 0), memory_space=pltpu.SMEM),
                  pl.BlockSpec((TC, d), lambda i: (i, 0))],
        out_specs=pl.BlockSpec(memory_space=pl.ANY),
        out_shape=jax.ShapeDtypeStruct((sorted_rows, d), BF16),
        scratch_shapes=[pltpu.VMEM((PLANES * HP_STRIDE, LANE), F32),
                        pltpu.VMEM((PLANES * LOC_STRIDE, LANE), F32),
                        pltpu.VMEM((2, LOCAL_ROWS, d), BF16),
                        pltpu.VMEM((PAGE, d), BF16),
                        pltpu.VMEM((TILE_M, d), BF16),
                        pltpu.SMEM((4,), jnp.int32),
                        pltpu.SemaphoreType.DMA((2,)), pltpu.SemaphoreType.DMA((2,)), pltpu.SemaphoreType.DMA],
        compiler_params=arb1,
        name="moe_dispatch",
    )(counts, ebase, cbase, etp, nt, pos, h2)

    tile_map = lambda i, s0, s1, sl, nt_ref: (jnp.minimum(i, (nt_ref[0] - 1) // 2), 0)
    slot_maps = (lambda i, s0, s1, sl, nt_ref: (s0[i], 0, 0), lambda i, s0, s1, sl, nt_ref: (s1[i], 0, 0))
    w_specs = [pl.BlockSpec(shape, slot_maps[slot]) for slot in range(2)
               for shape in ((1, d, D_EXPERT), (1, d, D_EXPERT), (1, D_EXPERT, d))]
    eo = pl.pallas_call(
        _expert_kernel,
        grid_spec=pltpu.PrefetchScalarGridSpec(
            num_scalar_prefetch=4,
            grid=(nt_max // 2,),
            in_specs=[pl.BlockSpec((2 * TILE_M, d), tile_map)] + w_specs,
            out_specs=pl.BlockSpec((2 * TILE_M, d), tile_map),
            scratch_shapes=[pltpu.VMEM((2, d, D_EXPERT), BF16), pltpu.VMEM((2, d, D_EXPERT), BF16),
                            pltpu.VMEM((2, D_EXPERT, d), BF16)],
        ),
        out_shape=jax.ShapeDtypeStruct((sorted_rows, d), BF16),
        input_output_aliases={4: 0},
        compiler_params=arb1,
        name="moe_experts",
    )(slot0, slot1, sel, nt, xsort, weg, weu, wed, weg, weu, wed)

    nph = COMBINE_PHASES
    pstep = lambda i, j: jnp.minimum(i * nph + j, npc * nph - 1)
    sstep = lambda i, j: jnp.maximum((i - npc) * nph + j, 0)
    smem_chunk = pl.BlockSpec((1, 1, TOP_K * TC), lambda i, j: (i, 0, 0), memory_space=pltpu.SMEM)
    yp, ys = pl.pallas_call(
        functools.partial(_combine_kernel, n_prompt_chunks=npc, n_chunks=n_chunks),
        grid=(n_chunks, nph),
        in_specs=[_smem_spec(), _smem_spec(), _smem_spec(), smem_chunk, smem_chunk,
                  pl.BlockSpec((gpp, r, d), lambda i, j: (pstep(i, j), 0, 0)),
                  pl.BlockSpec((gpp, r, d), lambda i, j: (sstep(i, j), 0, 0)),
                  pl.BlockSpec((1, 1, d), lambda i, j: (pclamp(i) // cpb, 0, 0)),
                  pl.BlockSpec((gpp, 1, d), lambda i, j: (sstep(i, j), 0, 0)),
                  pl.BlockSpec((PHASE_ROWS, d), lambda i, j: (i * nph + j, 0)),
                  cst((1, d)), cst((d, D_EXPERT)), cst((d, D_EXPERT)), cst((D_EXPERT, d)),
                  pl.BlockSpec(memory_space=pl.ANY)],
        out_specs=[pl.BlockSpec((gpp, r, d), lambda i, j: (pstep(i, j), 0, 0)),
                   pl.BlockSpec((gpp, r, d), lambda i, j: (sstep(i, j), 0, 0))],
        out_shape=[jax.ShapeDtypeStruct(xp.shape, F32), jax.ShapeDtypeStruct(xs.shape, F32)],
        scratch_shapes=[pltpu.VMEM((PLANES * LOC_STRIDE, LANE), F32),
                        pltpu.VMEM((2, LOCAL_ROWS, d), BF16),
                        pltpu.VMEM((PLANES * Y_STRIDE, LANE), F32),
                        pltpu.SMEM((2,), jnp.int32),
                        pltpu.SemaphoreType.DMA, pltpu.SemaphoreType.DMA],
        compiler_params=arb2,
        name="moe_combine",
    )(counts, ebase, cbase, pos, wts, xp, xs, modp[:, :, 2 * d:3 * d], mods_g[:, :, 2 * d:3 * d],
      h2, g_final, wsg, wsu, wsd, eo)
    return yp.reshape(b, s_len, d), ys.reshape(db, ds, d)


def _t5_bucket(rel):
    nb = NUM_BUCKETS // 2
    max_exact = nb // 2
    ret = jnp.where(rel > 0, nb, 0)
    n = jnp.abs(rel)
    nf = jnp.maximum(n, 1).astype(F32)
    large = max_exact + (jnp.log(nf / max_exact) / math.log(REL_MAX_DIST / max_exact)
                         * (nb - max_exact)).astype(jnp.int32)
    large = jnp.minimum(large, nb - 1)
    return ret + jnp.where(n < max_exact, n, large)


def _pair_layout(per_head):
    groups = []
    for g in range(N_KV_HEADS):
        pars = []
        for par in range(2):
            pars.append(jnp.concatenate([per_head[4 * g + par], per_head[4 * g + par + 2]], axis=0))
        groups.append(jnp.stack(pars))
    return jnp.stack(groups)


REL_SPAN = 2 * LANE


def _rel_bias_kernel(tab_ref, o_ref):
    tab = tab_ref[...]
    for i in range(CHUNK):
        shift = CHUNK - 1 - i
        o_ref[i] = pltpu.roll(tab, (REL_SPAN - shift) % REL_SPAN, 1)[:, 0:BAND]


def _rel_bias(rel_table):
    rel = jnp.arange(REL_SPAN) - (WINDOW + CHUNK - 1)
    tab = rel_table[_t5_bucket(rel)].astype(F32).T
    out = pl.pallas_call(
        _rel_bias_kernel,
        in_specs=[pl.BlockSpec((N_HEADS, REL_SPAN), lambda: (0, 0))],
        out_specs=pl.BlockSpec((CHUNK, N_HEADS, BAND), lambda: (0, 0, 0)),
        out_shape=jax.ShapeDtypeStruct((CHUNK, N_HEADS, BAND), F32),
        name="rel_bias",
    )(tab)
    return jnp.transpose(out, (1, 0, 2))


def _pair_window_layout(bias):
    off = jnp.full((CHUNK, CHUNK), -1e30, F32)
    groups = []
    for g in range(N_KV_HEADS):
        pars = []
        for par in range(2):
            rows = []
            for hp in range(2):
                head = 4 * g + par + 2 * hp
                rows.append(jnp.concatenate([bias[head], off], axis=1))
                rows.append(jnp.concatenate([off, bias[head]], axis=1))
            pars.append(jnp.concatenate(rows, axis=0))
        groups.append(jnp.stack(pars))
    return jnp.stack(groups)


def kernel(x_prompt, x_sample, c_prompt, c_sample, cache_k, cache_v, state_pool, w_ada, b_ada, g_norm_mix, g_norm_ffn, w_in, w_pool_group, pool_scale, attn_sinks, w_pool_proj, w_attn_proj, w_out, rel_table, w_router, router_bias, w_exp_gate, w_exp_up, w_exp_down, w_sh_gate, w_sh_up, w_sh_down, g_final):
    b, s_len, d = x_prompt.shape
    db, ds, _ = x_sample.shape
    l = 0

    c_all = jnp.concatenate([c_prompt, c_sample, jnp.zeros((4, d), F32)], axis=0)
    mods = _ada(c_all, w_ada[l], b_ada[l])
    mod_p = mods[0:b].reshape(b, 1, 6 * d)
    mod_s = mods[b:b + db].reshape(db, 1, 6 * d)

    bias = _rel_bias(rel_table)
    bias4 = jnp.pad(_pair_layout(bias), ((0, 0), (0, 0), (0, 0), (0, PAIR_KEYS - BAND)), constant_values=-1e30)
    sink4 = _pair_layout(jnp.broadcast_to(attn_sinks[l].astype(F32)[:, None, None], (N_HEADS, CHUNK, 128)))
    w_pg = jnp.zeros((POOL_WIDTH, POOL_WIDTH), F32)
    for g in range(len(POOL_WINDOWS)):
        w_pg = w_pg.at[g * POOL_GW:(g + 1) * POOL_GW, g * POOL_GW:(g + 1) * POOL_GW].set(w_pool_group[l, g])
    w_pg = w_pg.astype(BF16)
    w_in_b = w_in[l].astype(BF16)
    w_pp = w_pool_proj[l].astype(BF16)
    w_ap = w_attn_proj[l].astype(BF16)
    w_out_b = w_out[l].astype(BF16)
    g_mix = g_norm_mix[l].reshape(1, d)
    g_ffn = g_norm_ffn[l].reshape(1, d)
    ps = pool_scale[l].reshape(1, POOL_WIDTH)

    bias_pw = jnp.swapaxes(_pair_window_layout(bias), 2, 3)
    sk = attn_sinks[l].astype(F32)
    sink_pw = jnp.concatenate([jnp.full((2 * CHUNK,), 1.0, F32) * sk[4 * g + par + 2 * hp]
                               for g in range(N_KV_HEADS) for par in range(2) for hp in range(2)]).reshape(1, -1)
    x1p, nk_p, nv_p, np_p = _mixer_prompt(x_prompt, mod_p, g_mix, w_in_b, w_pg, ps, bias_pw, sink_pw,
                                          w_pp, w_ap, w_out_b)
    ck = cache_k[l].reshape(db, WINDOW, KV_WIDTH)
    cv = cache_v[l].reshape(db, WINDOW, KV_WIDTH)
    sp16 = jnp.pad(state_pool[l], ((0, 0), (HIST - POOL_PAD, 0), (0, 0)))
    x1s, nk_s, nv_s, np_s = _mixer_sample(x_sample, mod_s, ck, cv, sp16, g_mix, w_in_b, w_pg, ps,
                                          bias4, sink4, w_pp, w_ap, w_out_b)

    y_p, y_s = _moe_sparse(x1p, x1s, mod_p[:, :, 3 * d:], mod_s[:, :, 3 * d:], g_ffn, g_final.reshape(1, d),
                           w_router[l].T.astype(BF16), router_bias[l].astype(F32).reshape(N_EXPERTS, 1),
                           w_sh_gate[l].astype(BF16), w_sh_up[l].astype(BF16), w_sh_down[l].astype(BF16),
                           w_exp_gate[l], w_exp_up[l], w_exp_down[l])

    kv_shape = lambda n: (1, n, WINDOW, N_KV_HEADS, HEAD_DIM)
    return (y_p, y_s,
            nk_p.reshape(kv_shape(b)), nv_p.reshape(kv_shape(b)),
            np_p[:, HIST - POOL_PAD:, :][None],
            nk_s.reshape(kv_shape(db)), nv_s.reshape(kv_shape(db)),
            np_s[:, HIST - POOL_PAD:, :][None])
```

```python
import functools
import math

import jax
import jax.numpy as jnp
from jax import lax
from jax.experimental import pallas as pl
from jax.experimental.pallas import tpu as pltpu

F32 = jnp.float32
BF16 = jnp.bfloat16

D_MODEL = 1024
CHUNK = 64
EPS = 1e-6
POOL_WIDTH = 512
POOL_WINDOWS = (2, 4, 8, 16)
POOL_GW = 128
POOL_PAD = 15
HEAD_DIM = 64
N_HEADS = 8
N_KV_HEADS = 2
ATTN_WIDTH = 512
KV_WIDTH = 128
WINDOW = 128
BAND = WINDOW + CHUNK
PAIR_KEYS = BAND + CHUNK
NUM_BUCKETS = 32
REL_MAX_DIST = 128
PAST_LEN = 4096
N_EXPERTS = 32
TOP_K = 4
N_GROUPS = 4
TOPK_GROUPS = 2
EXPERTS_PER_GROUP = 8
D_EXPERT = 256
ROUTED_SCALE = 2.5

OFF_U = 0
OFF_Q = OFF_U + POOL_WIDTH
OFF_K = OFF_Q + ATTN_WIDTH
OFF_V = OFF_K + KV_WIDTH
OFF_GP = OFF_V + KV_WIDTH
OFF_GA = OFF_GP + D_MODEL
IN_WIDTH = OFF_GA + D_MODEL

HIST = 16
MIX_ROWS = 512
MERGE_PANEL = 256
VMEM_LIMIT = 56 * 1024 * 1024

GROUP_ROWS = 64
TC = 1024
PAGE = 16
PAGE_SHIFT = 4
TILE_M = 512
PAGES_PER_TILE = TILE_M // PAGE
TILE_SHIFT = 5
LOCAL_ROWS = TOP_K * TC + N_EXPERTS * PAGE
LANE = 128
PLANES = D_MODEL // LANE
PLANE_PAD = 8
HP_STRIDE = TC + PLANE_PAD
LOC_STRIDE = LOCAL_ROWS + PLANE_PAD
COMBINE_PHASES = 4
PHASE_ROWS = TC // COMBINE_PHASES
RUN_COPY = 4
RUN_COPY_SHIFT = 2
Y_STRIDE = PHASE_ROWS + PLANE_PAD


def _dot(a, b):
    return jnp.dot(a, b, preferred_element_type=F32)


def _dot_nt(a, b):
    return lax.dot_general(a, b, (((1,), (1,)), ((), ())), preferred_element_type=F32)


def _norm_mod(x, g, shift, scale):
    ms = jnp.mean(x * x, axis=-1, keepdims=True)
    y = x * lax.rsqrt(ms + EPS) * g
    return y * (1.0 + scale) + shift


def _silu(x):
    return x * jax.nn.sigmoid(x)


def _ada_kernel(c_ref, w_ref, b_ref, o_ref):
    a = _silu(c_ref[...]).astype(BF16)
    o_ref[...] = _dot(a, w_ref[...].astype(BF16)) + b_ref[...]


def _ada(c_all, w_ada, b_ada):
    rows = c_all.shape[0]
    n = w_ada.shape[1]
    tn = 768
    return pl.pallas_call(
        _ada_kernel,
        grid=(n // tn,),
        in_specs=[
            pl.BlockSpec((rows, D_MODEL), lambda j: (0, 0)),
            pl.BlockSpec((D_MODEL, tn), lambda j: (0, j)),
            pl.BlockSpec((1, tn), lambda j: (0, j)),
        ],
        out_specs=pl.BlockSpec((rows, tn), lambda j: (0, j)),
        out_shape=jax.ShapeDtypeStruct((rows, n), F32),
        compiler_params=pltpu.CompilerParams(dimension_semantics=("arbitrary",)),
        name="ada",
    )(c_all, w_ada, b_ada.reshape(1, n))


def _pool_feats(ue, pos0, rows):
    s2 = ue + pltpu.roll(ue, 1, 0)
    s4 = s2 + pltpu.roll(s2, 2, 0)
    s8 = s4 + pltpu.roll(s4, 4, 0)
    s16 = s8 + pltpu.roll(s8, 8, 0)
    pos = pos0 + lax.broadcasted_iota(jnp.int32, (rows, POOL_GW), 0)
    outs = []
    for g, (w, sw) in enumerate(zip(POOL_WINDOWS, (s2, s4, s8, s16))):
        sl = slice(g * POOL_GW, (g + 1) * POOL_GW)
        cnt = jnp.minimum(pos + 1, w).astype(F32)
        outs.append(sw[HIST:, sl] / cnt - ue[HIST:, sl])
    return jnp.concatenate(outs, axis=1)


def _kv_variants(t):
    lane = lax.broadcasted_iota(jnp.int32, t.shape, 1)
    low = lane < HEAD_DIM
    swapped = pltpu.roll(t, HEAD_DIM, 1)
    zero = jnp.zeros_like(t)
    return (jnp.where(low, t, zero), jnp.where(low, zero, swapped),
            jnp.where(low, swapped, zero), jnp.where(low, zero, t))


def _merge_out(h, pool_y, attn_y, win_ref, wpp_ref, wap_ref, wout_ref, m_scr):
    rows = h.shape[0]
    pool_b = pool_y.astype(BF16)
    attn_b = attn_y.astype(BF16)
    for n in range(D_MODEL // MERGE_PANEL):
        lo, hi = n * MERGE_PANEL, (n + 1) * MERGE_PANEL
        gp = _dot(h, win_ref[:, OFF_GP + lo:OFF_GP + hi])
        ga = _dot(h, win_ref[:, OFF_GA + lo:OFF_GA + hi])
        pp = _dot(pool_b, wpp_ref[:, lo:hi])
        ap = _dot(attn_b, wap_ref[:, lo:hi])
        m_scr[0:rows, lo:hi] = (jax.nn.sigmoid(gp) * pp + jax.nn.sigmoid(ga) * ap).astype(BF16)
    return _dot(m_scr[0:rows, :], wout_ref[...])


def _mixer_prompt_kernel(x_ref, mod_ref, g_ref, win_ref, wpg_ref, ps_ref, bias_ref, sink_ref,
                         wpp_ref, wap_ref, wout_ref,
                         x1_ref, nk_ref, nv_ref, npool_ref,
                         uext, k0, k1, k2, k3, v0, v1, v2, v3, m_scr, s_scr, p_scr, a_scr):
    ts = MIX_ROWS
    s = pl.program_id(1)
    kext = (k0, k1, k2, k3)
    vext = (v0, v1, v2, v3)

    @pl.when(s == 0)
    def _():
        uext[0:HIST, :] = jnp.zeros((HIST, POOL_WIDTH), F32)
        for r in kext:
            r[0:WINDOW, :] = jnp.zeros((WINDOW, KV_WIDTH), BF16)
        for r in vext:
            r[:, 0:WINDOW] = jnp.zeros((KV_WIDTH, WINDOW), BF16)

    x = x_ref[0]
    shift = mod_ref[0, :, 0:D_MODEL]
    scale = mod_ref[0, :, D_MODEL:2 * D_MODEL]
    gate = mod_ref[0, :, 2 * D_MODEL:3 * D_MODEL]
    h = _norm_mod(x, g_ref[...], shift, scale).astype(BF16)

    u = _dot(h, win_ref[:, OFF_U:OFF_U + POOL_WIDTH])
    q = (_dot(h, win_ref[:, OFF_Q:OFF_Q + ATTN_WIDTH]) * (HEAD_DIM ** -0.5)).astype(BF16)
    kv = _dot(h, win_ref[:, OFF_K:OFF_V + KV_WIDTH])
    k = kv[:, 0:KV_WIDTH]
    v = kv[:, KV_WIDTH:2 * KV_WIDTH]
    nk_ref[0] = k[ts - WINDOW:ts, :]
    nv_ref[0] = v[ts - WINDOW:ts, :]
    npool_ref[0] = u[ts - HIST:ts, :]

    pos0 = s * ts
    uext[HIST:HIST + ts, :] = u
    p = _pool_feats(uext[...], pos0, ts)
    uext[0:HIST, :] = u[ts - HIST:ts, :]
    pool_y = _dot(p.astype(BF16), wpg_ref[...]) * ps_ref[...]

    for r, t in zip(kext, _kv_variants(k)):
        r[WINDOW:WINDOW + ts, :] = t.astype(BF16)
    for r, t in zip(vext, _kv_variants(v)):
        r[:, WINDOW:WINDOW + ts] = t.T.astype(BF16)
    pair_rows = 2 * CHUNK
    blk_q = 2 * pair_rows
    n_pairs = ts // pair_rows
    per_pair = N_KV_HEADS * 2
    key_i = lax.broadcasted_iota(jnp.int32, (PAIR_KEYS, blk_q), 0)
    for pr in range(n_pairs):
        lo = pr * pair_rows
        valid = (pos0 + lo + key_i) >= WINDOW
        for g in range(N_KV_HEADS):
            qs = jnp.concatenate([q[lo:lo + pair_rows, (2 * g) * LANE:(2 * g + 1) * LANE],
                                  q[lo:lo + pair_rows, (2 * g + 1) * LANE:(2 * g + 2) * LANE]], axis=0)
            for par in range(2):
                s = _dot_nt(kext[2 * g + par][lo:lo + PAIR_KEYS, :], qs) + bias_ref[g, par]
                s_scr[(pr * N_KV_HEADS + g) * 2 + par] = jnp.where(valid, s, -1e30)
    sink = sink_ref[...]
    for pr in range(n_pairs):
        s = jnp.concatenate([s_scr[pr * per_pair + b] for b in range(per_pair)], axis=1)
        m = jnp.maximum(jnp.max(s, axis=0, keepdims=True), sink)
        e = jnp.exp(s - m)
        denom = jnp.sum(e, axis=0, keepdims=True) + jnp.exp(sink - m)
        p = (e / denom).astype(BF16)
        for b in range(per_pair):
            p_scr[pr * per_pair + b] = p[:, b * blk_q:(b + 1) * blk_q]
    for pr in range(n_pairs):
        lo = pr * pair_rows
        for g in range(N_KV_HEADS):
            blk = (pr * N_KV_HEADS + g) * 2
            ot = (_dot(vext[2 * g][:, lo:lo + PAIR_KEYS], p_scr[blk])
                  + _dot(vext[2 * g + 1][:, lo:lo + PAIR_KEYS], p_scr[blk + 1]))
            o = ot.T
            a_scr[lo:lo + pair_rows, (2 * g) * LANE:(2 * g + 1) * LANE] = o[0:pair_rows].astype(BF16)
            a_scr[lo:lo + pair_rows, (2 * g + 1) * LANE:(2 * g + 2) * LANE] = o[pair_rows:].astype(BF16)
    for r in kext:
        r[0:WINDOW, :] = r[ts:ts + WINDOW, :]
    for r in vext:
        r[:, 0:WINDOW] = r[:, ts:ts + WINDOW]

    y = _merge_out(h, pool_y, a_scr[...], win_ref, wpp_ref, wap_ref, wout_ref, m_scr)
    x1_ref[0] = x + gate * y


def _const_spec(shape):
    nd = len(shape)
    return pl.BlockSpec(shape, lambda *_: (0,) * nd)


def _mixer_prompt(x, mod, g_mix, w_in, w_pg, pool_scale, bias4, sink4, w_pp, w_ap, w_out):
    b, s_len, d = x.shape
    ts = MIX_ROWS
    pair_rows = 2 * CHUNK
    n_blocks = (ts // pair_rows) * N_KV_HEADS * 2
    kv_scratch = ([pltpu.VMEM((WINDOW + ts, KV_WIDTH), BF16) for _ in range(4)]
                  + [pltpu.VMEM((KV_WIDTH, WINDOW + ts), BF16) for _ in range(4)])
    return pl.pallas_call(
        _mixer_prompt_kernel,
        grid=(b, s_len // ts),
        in_specs=[
            pl.BlockSpec((1, ts, d), lambda i, j: (i, j, 0)),
            pl.BlockSpec((1, 1, 6 * d), lambda i, j: (i, 0, 0)),
            _const_spec((1, d)),
            _const_spec((d, IN_WIDTH)),
            _const_spec((POOL_WIDTH, POOL_WIDTH)),
            _const_spec((1, POOL_WIDTH)),
            _const_spec(bias4.shape),
            _const_spec(sink4.shape),
            _const_spec((POOL_WIDTH, d)),
            _const_spec((ATTN_WIDTH, d)),
            _const_spec((d, d)),
        ],
        out_specs=[
            pl.BlockSpec((1, ts, d), lambda i, j: (i, j, 0)),
            pl.BlockSpec((1, WINDOW, KV_WIDTH), lambda i, j: (i, 0, 0)),
            pl.BlockSpec((1, WINDOW, KV_WIDTH), lambda i, j: (i, 0, 0)),
            pl.BlockSpec((1, HIST, POOL_WIDTH), lambda i, j: (i, 0, 0)),
        ],
        out_shape=[
            jax.ShapeDtypeStruct((b, s_len, d), F32),
            jax.ShapeDtypeStruct((b, WINDOW, KV_WIDTH), F32),
            jax.ShapeDtypeStruct((b, WINDOW, KV_WIDTH), F32),
            jax.ShapeDtypeStruct((b, HIST, POOL_WIDTH), F32),
        ],
        scratch_shapes=[pltpu.VMEM((HIST + ts, POOL_WIDTH), F32)] + kv_scratch
                       + [pltpu.VMEM((ts, d), BF16),
                          pltpu.VMEM((n_blocks, 2 * pair_rows, PAIR_KEYS), F32),
                          pltpu.VMEM((n_blocks, 2 * pair_rows, PAIR_KEYS), BF16),
                          pltpu.VMEM((ts, ATTN_WIDTH), BF16)],
        compiler_params=pltpu.CompilerParams(
            dimension_semantics=("arbitrary", "arbitrary"), vmem_limit_bytes=VMEM_LIMIT),
        name="mixer_prompt",
    )(x, mod, g_mix, w_in, w_pg, pool_scale, bias4, sink4, w_pp, w_ap, w_out)


def _mixer_sample_kernel(x_ref, mod_ref, ck_ref, cv_ref, sp_ref, g_ref, win_ref, wpg_ref, ps_ref,
                         bias_ref, sink_ref, wpp_ref, wap_ref, wout_ref,
                         x1_ref, nk_ref, nv_ref, npool_ref, m_scr, ks, vs, s_scr, p_scr, a_scr):
    nseq, ds, d = x_ref.shape
    rows = nseq * ds
    x = x_ref[...]
    shift = mod_ref[:, :, 0:D_MODEL]
    scale = mod_ref[:, :, D_MODEL:2 * D_MODEL]
    gate = mod_ref[:, :, 2 * D_MODEL:3 * D_MODEL]
    h = _norm_mod(x, g_ref[...], shift, scale).reshape(rows, d).astype(BF16)

    u = _dot(h, win_ref[:, OFF_U:OFF_U + POOL_WIDTH])
    q = (_dot(h, win_ref[:, OFF_Q:OFF_Q + ATTN_WIDTH]) * (HEAD_DIM ** -0.5)).astype(BF16)
    kv = _dot(h, win_ref[:, OFF_K:OFF_V + KV_WIDTH])
    k = kv[:, 0:KV_WIDTH]
    v = kv[:, KV_WIDTH:2 * KV_WIDTH]

    pool_parts = []
    key_pad = jnp.zeros((PAIR_KEYS - BAND, KV_WIDTH), F32)
    for b in range(nseq):
        lo = b * ds
        ub = u[lo:lo + ds, :]
        kb = k[lo:lo + ds, :]
        vb = v[lo:lo + ds, :]
        ck = ck_ref[b]
        cv = cv_ref[b]
        nk_ref[b] = jnp.concatenate([ck[ds:WINDOW, :], kb], axis=0)
        nv_ref[b] = jnp.concatenate([cv[ds:WINDOW, :], vb], axis=0)
        npool_ref[b] = ub[ds - HIST:ds, :]
        ue = jnp.concatenate([sp_ref[b], ub], axis=0)
        pool_parts.append(_pool_feats(ue, PAST_LEN, ds))
        for var, t in enumerate(_kv_variants(jnp.concatenate([ck, kb, key_pad], axis=0))):
            ks[var, b] = t.astype(BF16)
        for var, t in enumerate(_kv_variants(jnp.concatenate([cv, vb, key_pad], axis=0))):
            vs[var, b] = t.astype(BF16)
    p = jnp.concatenate(pool_parts, axis=0)
    pool_y = _dot(p.astype(BF16), wpg_ref[...]) * ps_ref[...]

    per_seq = N_KV_HEADS * 2
    for b in range(nseq):
        lo = b * ds
        for g in range(N_KV_HEADS):
            qs = jnp.concatenate([q[lo:lo + ds, (2 * g) * LANE:(2 * g + 1) * LANE],
                                  q[lo:lo + ds, (2 * g + 1) * LANE:(2 * g + 2) * LANE]], axis=0)
            for par in range(2):
                s_scr[b * per_seq + 2 * g + par] = _dot_nt(qs, ks[2 * g + par, b]) + bias_ref[g, par]
    group = 2
    grp_rows = group * per_seq * 2 * ds
    sink1 = sink_ref[...].reshape(per_seq * 2 * ds, LANE)[:, 0:1]
    sink = jnp.concatenate([sink1] * group, axis=0)
    for gi in range(nseq // group):
        blks = slice(gi * group * per_seq, (gi + 1) * group * per_seq)
        s = s_scr[blks].reshape(grp_rows, PAIR_KEYS)
        m = jnp.maximum(jnp.max(s, axis=-1, keepdims=True), sink)
        e = jnp.exp(s - m)
        esum = _dot(e.astype(BF16), jnp.ones((PAIR_KEYS, LANE), BF16))
        denom = esum + jnp.exp(sink - m)
        p_scr[blks] = ((e / jnp.concatenate([denom, denom], axis=1)).astype(BF16)
                       .reshape(group * per_seq, 2 * ds, PAIR_KEYS))
    for b in range(nseq):
        lo = b * ds
        for g in range(N_KV_HEADS):
            blk = b * per_seq + 2 * g
            o = _dot(p_scr[blk], vs[2 * g, b]) + _dot(p_scr[blk + 1], vs[2 * g + 1, b])
            a_scr[lo:lo + ds, (2 * g) * LANE:(2 * g + 1) * LANE] = o[0:ds].astype(BF16)
            a_scr[lo:lo + ds, (2 * g + 1) * LANE:(2 * g + 2) * LANE] = o[ds:].astype(BF16)

    y = _merge_out(h, pool_y, a_scr[...], win_ref, wpp_ref, wap_ref, wout_ref, m_scr)
    x1_ref[...] = x + gate * y.reshape(nseq, ds, d)


def _mixer_sample(x, mod, ck, cv, sp16, g_mix, w_in, w_pg, pool_scale, bias4, sink4, w_pp, w_ap, w_out):
    db, ds, d = x.shape
    nseq = MIX_ROWS // ds
    seq_spec = lambda shape: pl.BlockSpec((nseq,) + shape, lambda i: (i, 0, 0))
    return pl.pallas_call(
        _mixer_sample_kernel,
        grid=(db // nseq,),
        in_specs=[
            seq_spec((ds, d)),
            seq_spec((1, 6 * d)),
            seq_spec((WINDOW, KV_WIDTH)),
            seq_spec((WINDOW, KV_WIDTH)),
            seq_spec((HIST, POOL_WIDTH)),
            _const_spec((1, d)),
            _const_spec((d, IN_WIDTH)),
            _const_spec((POOL_WIDTH, POOL_WIDTH)),
            _const_spec((1, POOL_WIDTH)),
            _const_spec(bias4.shape),
            _const_spec(sink4.shape),
            _const_spec((POOL_WIDTH, d)),
            _const_spec((ATTN_WIDTH, d)),
            _const_spec((d, d)),
        ],
        out_specs=[
            seq_spec((ds, d)),
            seq_spec((WINDOW, KV_WIDTH)),
            seq_spec((WINDOW, KV_WIDTH)),
            seq_spec((HIST, POOL_WIDTH)),
        ],
        out_shape=[
            jax.ShapeDtypeStruct((db, ds, d), F32),
            jax.ShapeDtypeStruct((db, WINDOW, KV_WIDTH), F32),
            jax.ShapeDtypeStruct((db, WINDOW, KV_WIDTH), F32),
            jax.ShapeDtypeStruct((db, HIST, POOL_WIDTH), F32),
        ],
        scratch_shapes=[pltpu.VMEM((MIX_ROWS, d), BF16),
                        pltpu.VMEM((4, nseq, PAIR_KEYS, KV_WIDTH), BF16),
                        pltpu.VMEM((4, nseq, PAIR_KEYS, KV_WIDTH), BF16),
                        pltpu.VMEM((nseq * N_KV_HEADS * 2, 2 * ds, PAIR_KEYS), F32),
                        pltpu.VMEM((nseq * N_KV_HEADS * 2, 2 * ds, PAIR_KEYS), BF16),
                        pltpu.VMEM((MIX_ROWS, ATTN_WIDTH), BF16)],
        compiler_params=pltpu.CompilerParams(
            dimension_semantics=("arbitrary",), vmem_limit_bytes=VMEM_LIMIT),
        name="mixer_sample",
    )(x, mod, ck, cv, sp16, g_mix, w_in, w_pg, pool_scale, bias4, sink4, w_pp, w_ap, w_out)


def _first_argmax(vals, idx, size):
    m = jnp.max(vals, axis=0, keepdims=True)
    first = jnp.min(jnp.where(vals == m, idx, size), axis=0, keepdims=True)
    return m, first


def _router_picks(h, wr_ref, rb_ref):
    t = h.shape[0]
    logits = _dot_nt(wr_ref[...], h)
    scores = jax.nn.sigmoid(logits)
    sel = scores + rb_ref[...]
    neg = jnp.float32(-jnp.inf)
    eidx = lax.broadcasted_iota(jnp.int32, (N_EXPERTS, t), 0)

    gscores = []
    bidx = lax.broadcasted_iota(jnp.int32, (EXPERTS_PER_GROUP, t), 0)
    for g in range(N_GROUPS):
        blk = sel[g * EXPERTS_PER_GROUP:(g + 1) * EXPERTS_PER_GROUP, :]
        m1, i1 = _first_argmax(blk, bidx, EXPERTS_PER_GROUP)
        m2 = jnp.max(jnp.where(bidx == i1, neg, blk), axis=0, keepdims=True)
        gscores.append(m1 + m2)
    gs = jnp.concatenate(gscores, axis=0)
    gidx = lax.broadcasted_iota(jnp.int32, (N_GROUPS, t), 0)
    _, g1 = _first_argmax(gs, gidx, N_GROUPS)
    _, g2 = _first_argmax(jnp.where(gidx == g1, neg, gs), gidx, N_GROUPS)
    egrp = eidx // EXPERTS_PER_GROUP
    cand = jnp.where((egrp == g1) | (egrp == g2), sel, neg)

    picks = []
    for _ in range(TOP_K):
        _, i = _first_argmax(cand, eidx, N_EXPERTS)
        hit = eidx == i
        picks.append(hit)
        cand = jnp.where(hit, neg, cand)
    return picks, scores


def _swiglu_rows(h, wg, wu, wd):
    act = _silu(_dot(h, wg)) * _dot(h, wu)
    return _dot(act.astype(BF16), wd)


def _route_kernel(xp_ref, xs_ref, modp_ref, mods_ref, g_ref, wr_ref, rb_ref, tri_ref,
                  wsg_ref, wsu_ref, wsd_ref, h_ref, pos_ref, w_ref, cnt_ref, sh_ref, *, n_prompt_chunks):
    is_prompt = pl.program_id(0) < n_prompt_chunks
    x = jnp.where(is_prompt, xp_ref[...], xs_ref[...])
    shift = jnp.where(is_prompt, modp_ref[:, :, 0:D_MODEL], mods_ref[:, :, 0:D_MODEL])
    scale = jnp.where(is_prompt, modp_ref[:, :, D_MODEL:2 * D_MODEL], mods_ref[:, :, D_MODEL:2 * D_MODEL])
    h = _norm_mod(x, g_ref[...], shift, scale).reshape(TC, D_MODEL).astype(BF16)
    h_ref[...] = h
    sh_ref[...] = _swiglu_rows(h, wsg_ref[...], wsu_ref[...], wsd_ref[...])
    picks, scores = _router_picks(h, wr_ref, rb_ref)
    picked = picks[0] | picks[1] | picks[2] | picks[3]
    cnt = _dot(jnp.where(picked, 1.0, 0.0).astype(BF16), tri_ref[...])
    n_e = jnp.max(cnt, axis=1, keepdims=True).astype(jnp.int32)
    n_b = jnp.broadcast_to(n_e, (N_EXPERTS, LANE))
    cnt_ref[0] = n_b
    padded = lax.shift_left(lax.shift_right_logical(n_b + (PAGE - 1), PAGE_SHIFT), PAGE_SHIFT)
    row = lax.broadcasted_iota(jnp.int32, (N_EXPERTS, LANE), 0)
    scan = padded
    for s in (1, 2, 4, 8, 16):
        scan = scan + jnp.where(row >= s, pltpu.roll(scan, s, 0), 0)
    base = (scan - padded)[:, 0:1].astype(F32)
    slot = base + cnt - 1.0
    pos = [jnp.sum(jnp.where(p, slot, 0.0), axis=0, keepdims=True) for p in picks]
    wk = [jnp.sum(jnp.where(p, scores, 0.0), axis=0, keepdims=True) for p in picks]
    wsum = wk[0] + wk[1] + wk[2] + wk[3]
    pos_ref[0] = jnp.concatenate(pos, axis=0).astype(jnp.int32)
    w_ref[0] = jnp.concatenate(wk, axis=0) / wsum * ROUTED_SCALE


def _plan_kernel(cnt_ref, te_ref, nt_ref, ebase_ref, cbase_ref, etp_ref, slot0_ref, slot1_ref, sel_ref,
                 *, n_chunks, nt_max):
    def per_expert(e, tile0):
        def per_chunk(c, acc):
            cbase_ref[c * N_EXPERTS + e] = acc
            return acc + lax.shift_right_logical(cnt_ref[c, e] + (PAGE - 1), PAGE_SHIFT)
        tp = lax.fori_loop(0, n_chunks, per_chunk, 0)
        etp_ref[e] = tp
        ebase_ref[e] = tile0 * PAGES_PER_TILE
        nt = lax.shift_right_logical(tp + (PAGES_PER_TILE - 1), TILE_SHIFT)

        def fill(j, carry):
            te_ref[tile0 + j] = e
            return carry
        lax.fori_loop(0, nt, fill, 0)
        return tile0 + nt
    total = lax.fori_loop(0, N_EXPERTS, per_expert, 0)
    nt_ref[0] = total

    def tail(j, carry):
        te_ref[j] = N_EXPERTS - 1
        return carry
    lax.fori_loop(total, nt_max, tail, 0)

    def per_step(i, carry):
        c0, c1 = carry
        used = 2 * i < total
        a = te_ref[jnp.minimum(2 * i, total - 1)]
        b = te_ref[jnp.minimum(2 * i + 1, total - 1)]
        hit_a = (a == c0) | (a == c1)
        evict0 = hit_a | (c0 != b)
        n0 = jnp.where(used & ~hit_a & evict0, a, c0)
        n1 = jnp.where(used & ~hit_a & ~evict0, a, c1)
        sa = jnp.where(a == n0, 0, 1)
        hit_b = (b == n0) | (b == n1)
        m0 = jnp.where(used & ~hit_b & (sa == 1), b, n0)
        m1 = jnp.where(used & ~hit_b & (sa == 0), b, n1)
        slot0_ref[i] = jnp.maximum(m0, 0)
        slot1_ref[i] = jnp.maximum(m1, 0)
        sel_ref[2 * i] = sa
        sel_ref[2 * i + 1] = jnp.where(b == m0, 0, 1)
        return (m0, m1)
    lax.fori_loop(0, nt_max // 2, per_step, (-1, -1))


def _page_copy(src, src_page, dst, dst_page, sem, pages=1):
    rows = pages * PAGE
    s = src.at[pl.ds(pl.multiple_of(src_page * PAGE, PAGE), rows), :]
    t = dst.at[pl.ds(pl.multiple_of(dst_page * PAGE, PAGE), rows), :]
    return pltpu.make_async_copy(s, t, sem)


def _for_each_run(c, cnt_ref, ebase_ref, cbase_ref, fn):
    def per_expert(e, carry):
        lpage, nbig, nsmall = carry
        n = cnt_ref[c, e]
        npages = lax.shift_right_logical(n + (PAGE - 1), PAGE_SHIFT)
        fn(n, lpage, ebase_ref[e] + cbase_ref[c * N_EXPERTS + e], npages)
        return (lpage + npages, nbig + lax.shift_right_logical(npages, RUN_COPY_SHIFT),
                nsmall + (npages & (RUN_COPY - 1)))
    return lax.fori_loop(0, N_EXPERTS, per_expert, (0, 0, 0))


def _start_run_copies(src, src_page, dst, dst_page, npages, sem_big, sem_small):
    nbig = lax.shift_right_logical(npages, RUN_COPY_SHIFT)

    def big(j, carry):
        _page_copy(src, src_page + j * RUN_COPY, dst, dst_page + j * RUN_COPY, sem_big, RUN_COPY).start()
        return carry
    lax.fori_loop(0, nbig, big, 0)

    def small(j, carry):
        _page_copy(src, src_page + j, dst, dst_page + j, sem_small).start()
        return carry
    lax.fori_loop(nbig * RUN_COPY, npages, small, 0)


def _wait_copies(src, dst, nbig, nsmall, sem_big, sem_small):
    def big(j, carry):
        _page_copy(src, 0, dst, 0, sem_big, RUN_COPY).wait()
        return carry
    lax.fori_loop(0, nbig, big, 0)

    def small(j, carry):
        _page_copy(src, 0, dst, 0, sem_small).wait()
        return carry
    lax.fori_loop(0, nsmall, small, 0)


def _row_slab(ref, row, stride):
    return ref.at[pl.ds(row, PLANES, stride=stride), :]


def _dispatch_kernel(cnt_ref, ebase_ref, cbase_ref, etp_ref, nt_ref, pos_ref, h_ref, xs_hbm,
                     hpl, loc, locb, zpage, ztile, pend, sem, bsem, tsem, *, n_chunks, nt_max):
    c = pl.program_id(0)
    half = c % 2
    buf = locb.at[half]
    bsem_h = bsem.at[half]
    sem_h = sem.at[half]

    @pl.when(c == 0)
    def _():
        for j in range(4):
            pend[j] = 0
        loc[...] = jnp.zeros(loc.shape, F32)

    h = h_ref[...]
    for ch in range(PLANES):
        hpl[ch * HP_STRIDE:ch * HP_STRIDE + TC, :] = h[:, ch * LANE:(ch + 1) * LANE].astype(F32)
    zpage[...] = jnp.zeros(zpage.shape, BF16)
    ztile[...] = jnp.zeros(ztile.shape, BF16)

    def scatter(t8, carry):
        for u in range(8):
            t = t8 * 8 + u
            slab = _row_slab(hpl, t, HP_STRIDE)[...]
            for k in range(TOP_K):
                _row_slab(loc, pos_ref[0, 0, k * TC + t], LOC_STRIDE)[...] = slab
        return carry
    lax.fori_loop(0, TC // 8, scatter, 0)

    _wait_copies(buf, xs_hbm, pend[2 * half], pend[2 * half + 1], bsem_h, sem_h)
    for ch in range(PLANES):
        buf[:, ch * LANE:(ch + 1) * LANE] = loc[ch * LOC_STRIDE:ch * LOC_STRIDE + LOCAL_ROWS, :].astype(BF16)

    def send_run(n, lpage, gpage, npages):
        _start_run_copies(buf, lpage, xs_hbm, gpage, npages, bsem_h, sem_h)
    _, n_big, n_small = _for_each_run(c, cnt_ref, ebase_ref, cbase_ref, send_run)
    pend[2 * half] = n_big
    pend[2 * half + 1] = n_small

    @pl.when(c == n_chunks - 1)
    def _():
        def pad_expert(e, n):
            tp = etp_ref[e]
            full = lax.shift_left(lax.shift_right_logical(tp + (PAGES_PER_TILE - 1), TILE_SHIFT), TILE_SHIFT)

            def per_page(j, carry):
                _page_copy(zpage, 0, xs_hbm, ebase_ref[e] + j, sem_h).start()
                return carry
            lax.fori_loop(tp, full, per_page, 0)
            return n + full - tp
        n_pad = lax.fori_loop(0, N_EXPERTS, pad_expert, 0)

        def tile_copy(i):
            dst = xs_hbm.at[pl.ds(pl.multiple_of(i * TILE_M, TILE_M), TILE_M), :]
            return pltpu.make_async_copy(ztile, dst, tsem)

        def tail_start(i, carry):
            tile_copy(i).start()
            return carry
        lax.fori_loop(nt_ref[0], nt_max, tail_start, 0)
        _wait_copies(buf, xs_hbm, n_big, n_small + n_pad, bsem_h, sem_h)
        other = 1 - half
        _wait_copies(buf, xs_hbm, pend[2 * other], pend[2 * other + 1], bsem.at[other], sem.at[other])

        def tail_wait(i, carry):
            tile_copy(i).wait()
            return carry
        lax.fori_loop(nt_ref[0], nt_max, tail_wait, 0)


def _expert_kernel(slot0_ref, slot1_ref, sel_ref, nt_ref, xs_ref, wg0_ref, wu0_ref, wd0_ref,
                   wg1_ref, wu1_ref, wd1_ref, o_ref, wgb, wub, wdb):
    i = pl.program_id(0)
    slots = ((slot0_ref, wg0_ref, wu0_ref, wd0_ref), (slot1_ref, wg1_ref, wu1_ref, wd1_ref))

    @pl.when(2 * i < nt_ref[0])
    def _():
        for slot, (idx_ref, wg_ref, wu_ref, wd_ref) in enumerate(slots):
            @pl.when((i == 0) | (idx_ref[i] != idx_ref[jnp.maximum(i - 1, 0)]))
            def _():
                wgb[slot] = wg_ref[0].astype(BF16)
                wub[slot] = wu_ref[0].astype(BF16)
                wdb[slot] = wd_ref[0].astype(BF16)
        for t in range(2):
            rows = slice(t * TILE_M, (t + 1) * TILE_M)
            sl = sel_ref[2 * i + t]
            o_ref[rows, :] = _swiglu_rows(xs_ref[rows, :], wgb[sl], wub[sl], wdb[sl]).astype(BF16)


def _combine_kernel(cnt_ref, ebase_ref, cbase_ref, pos_ref, w_ref, xp_ref, xs_ref, gp_ref, gs_ref,
                    sh_ref, gf_ref, eo_hbm, yp_ref, ys_ref,
                    loc, locb, ypl, pend, sem, bsem, *, n_prompt_chunks, n_chunks):
    c = pl.program_id(0)
    ph = pl.program_id(1)

    def fetch(chunk):
        buf = locb.at[chunk % 2]

        def fetch_run(n, lpage, gpage, npages):
            _start_run_copies(eo_hbm, gpage, buf, lpage, npages, bsem, sem)
        _, n_big, n_small = _for_each_run(chunk, cnt_ref, ebase_ref, cbase_ref, fetch_run)
        pend[0] = n_big
        pend[1] = n_small

    @pl.when((c == 0) & (ph == 0))
    def _():
        locb[...] = jnp.zeros(locb.shape, BF16)
        fetch(c)

    @pl.when(ph == 0)
    def _():
        cur = locb.at[c % 2]
        _wait_copies(eo_hbm, cur, pend[0], pend[1], bsem, sem)

        @pl.when(c + 1 < n_chunks)
        def _():
            fetch(c + 1)
        for ch in range(PLANES):
            loc[ch * LOC_STRIDE:ch * LOC_STRIDE + LOCAL_ROWS, :] = cur[:, ch * LANE:(ch + 1) * LANE].astype(F32)

    shared = sh_ref[...]
    t0 = ph * PHASE_ROWS

    def gather(t8, carry):
        for u in range(8):
            t = t8 * 8 + u
            acc = jnp.zeros((PLANES, LANE), F32)
            for k in range(TOP_K):
                idx = k * TC + t0 + t
                acc = acc + w_ref[0, 0, idx] * _row_slab(loc, pos_ref[0, 0, idx], LOC_STRIDE)[...]
            _row_slab(ypl, t, Y_STRIDE)[...] = acc
        return carry
    lax.fori_loop(0, PHASE_ROWS // 8, gather, 0)
    routed = jnp.concatenate([ypl[ch * Y_STRIDE:ch * Y_STRIDE + PHASE_ROWS, :] for ch in range(PLANES)], axis=1)
    f = shared + routed

    def finish(x_ref, gate_ref, y_ref):
        xo = x_ref[...] + gate_ref[...] * f.reshape(x_ref.shape)
        ms = jnp.mean(xo * xo, axis=-1, keepdims=True)
        y_ref[...] = xo * lax.rsqrt(ms + EPS) * gf_ref[...]

    @pl.when(c < n_prompt_chunks)
    def _():
        finish(xp_ref, gp_ref, yp_ref)

    @pl.when(c >= n_prompt_chunks)
    def _():
        finish(xs_ref, gs_ref, ys_ref)


def _smem_spec():
    return pl.BlockSpec(memory_space=pltpu.SMEM)


def _moe_sparse(x1p, x1s, modp, mods, g_ffn, g_final, w_r_t, rb, wsg, wsu, wsd, weg, weu, wed):
    b, s_len, d = x1p.shape
    db, ds, _ = x1s.shape
    r = GROUP_ROWS
    xp = x1p.reshape(b * s_len // r, r, d)
    xs = x1s.reshape(db * ds // r, r, d)
    mods_g = jnp.repeat(mods, ds // r, axis=0) if ds != r else mods
    npc = b * s_len // TC
    nsc = db * ds // TC
    n_chunks = npc + nsc
    n_tok = n_chunks * TC
    cpb = s_len // TC
    gpc = TC // r
    gpp = PHASE_ROWS // r
    max_pages = (TOP_K * n_tok) // PAGE + n_chunks * N_EXPERTS
    nt_max = -(-(max_pages // PAGES_PER_TILE + N_EXPERTS) // 8) * 8
    sorted_rows = nt_max * TILE_M
    arb1 = pltpu.CompilerParams(dimension_semantics=("arbitrary",), vmem_limit_bytes=VMEM_LIMIT)
    arb2 = pltpu.CompilerParams(dimension_semantics=("arbitrary", "arbitrary"), vmem_limit_bytes=VMEM_LIMIT)
    cst = lambda shape: pl.BlockSpec(shape, lambda *_: (0,) * len(shape))
    tri = jnp.triu(jnp.ones((TC, TC), BF16))
    pclamp = lambda i: jnp.minimum(i, npc - 1)
    sclamp = lambda i: jnp.maximum(i - npc, 0)

    h2, pos, wts, cnt, shared = pl.pallas_call(
        functools.partial(_route_kernel, n_prompt_chunks=npc),
        grid=(n_chunks,),
        in_specs=[
            pl.BlockSpec((gpc, r, d), lambda i: (pclamp(i), 0, 0)),
            pl.BlockSpec((gpc, r, d), lambda i: (sclamp(i), 0, 0)),
            pl.BlockSpec((1, 1, 2 * d), lambda i: (pclamp(i) // cpb, 0, 0)),
            pl.BlockSpec((gpc, 1, 2 * d), lambda i: (sclamp(i), 0, 0)),
            cst((1, d)), cst((N_EXPERTS, d)), cst((N_EXPERTS, 1)), cst((TC, TC)),
            cst((d, D_EXPERT)), cst((d, D_EXPERT)), cst((D_EXPERT, d)),
        ],
        out_specs=[
            pl.BlockSpec((TC, d), lambda i: (i, 0)),
            pl.BlockSpec((1, TOP_K, TC), lambda i: (i, 0, 0)),
            pl.BlockSpec((1, TOP_K, TC), lambda i: (i, 0, 0)),
            pl.BlockSpec((1, N_EXPERTS, LANE), lambda i: (i, 0, 0)),
            pl.BlockSpec((TC, d), lambda i: (i, 0)),
        ],
        out_shape=[
            jax.ShapeDtypeStruct((n_tok, d), BF16),
            jax.ShapeDtypeStruct((n_chunks, TOP_K, TC), jnp.int32),
            jax.ShapeDtypeStruct((n_chunks, TOP_K, TC), F32),
            jax.ShapeDtypeStruct((n_chunks, N_EXPERTS, LANE), jnp.int32),
            jax.ShapeDtypeStruct((n_tok, d), F32),
        ],
        compiler_params=arb1,
        name="moe_route",
    )(xp, xs, modp[:, :, 0:2 * d], mods_g[:, :, 0:2 * d], g_ffn, w_r_t, rb, tri, wsg, wsu, wsd)
    counts = cnt[:, :, 0]
    pos = pos.reshape(n_chunks, 1, TOP_K * TC)
    wts = wts.reshape(n_chunks, 1, TOP_K * TC)

    te, nt, ebase, cbase, etp, slot0, slot1, sel = pl.pallas_call(
        functools.partial(_plan_kernel, n_chunks=n_chunks, nt_max=nt_max),
        in_specs=[_smem_spec()],
        out_specs=[_smem_spec()] * 8,
        out_shape=[
            jax.ShapeDtypeStruct((nt_max,), jnp.int32),
            jax.ShapeDtypeStruct((1,), jnp.int32),
            jax.ShapeDtypeStruct((N_EXPERTS,), jnp.int32),
            jax.ShapeDtypeStruct((n_chunks * N_EXPERTS,), jnp.int32),
            jax.ShapeDtypeStruct((N_EXPERTS,), jnp.int32),
            jax.ShapeDtypeStruct((nt_max // 2,), jnp.int32),
            jax.ShapeDtypeStruct((nt_max // 2,), jnp.int32),
            jax.ShapeDtypeStruct((nt_max,), jnp.int32),
        ],
        name="moe_plan",
    )(counts)

    xsort = pl.pallas_call(
        functools.partial(_dispatch_kernel, n_chunks=n_chunks, nt_max=nt_max),
        grid=(n_chunks,),
        in_specs=[_smem_spec(), _smem_spec(), _smem_spec(), _smem_spec(), _smem_spec(),
                  pl.BlockSpec((1, 1, TOP_K * TC), lambda i: (i, 0, 0), memory_space=pltpu.SMEM),
                  pl.BlockSpec((TC, d), lambda i: (i, 0))],
        out_specs=pl.BlockSpec(memory_space=pl.ANY),
        out_shape=jax.ShapeDtypeStruct((sorted_rows, d), BF16),
        scratch_shapes=[pltpu.VMEM((PLANES * HP_STRIDE, LANE), F32),
                        pltpu.VMEM((PLANES * LOC_STRIDE, LANE), F32),
                        pltpu.VMEM((2, LOCAL_ROWS, d), BF16),
                        pltpu.VMEM((PAGE, d), BF16),
                        pltpu.VMEM((TILE_M, d), BF16),
                        pltpu.SMEM((4,), jnp.int32),
                        pltpu.SemaphoreType.DMA((2,)), pltpu.SemaphoreType.DMA((2,)), pltpu.SemaphoreType.DMA],
        compiler_params=arb1,
        name="moe_dispatch",
    )(counts, ebase, cbase, etp, nt, pos, h2)

    tile_map = lambda i, s0, s1, sl, nt_ref: (jnp.minimum(i, (nt_ref[0] - 1) // 2), 0)
    slot_maps = (lambda i, s0, s1, sl, nt_ref: (s0[i], 0, 0), lambda i, s0, s1, sl, nt_ref: (s1[i], 0, 0))
    w_specs = [pl.BlockSpec(shape, slot_maps[slot]) for slot in range(2)
               for shape in ((1, d, D_EXPERT), (1, d, D_EXPERT), (1, D_EXPERT, d))]
    eo = pl.pallas_call(
        _expert_kernel,
        grid_spec=pltpu.PrefetchScalarGridSpec(
            num_scalar_prefetch=4,
            grid=(nt_max // 2,),
            in_specs=[pl.BlockSpec((2 * TILE_M, d), tile_map)] + w_specs,
            out_specs=pl.BlockSpec((2 * TILE_M, d), tile_map),
            scratch_shapes=[pltpu.VMEM((2, d, D_EXPERT), BF16), pltpu.VMEM((2, d, D_EXPERT), BF16),
                            pltpu.VMEM((2, D_EXPERT, d), BF16)],
        ),
        out_shape=jax.ShapeDtypeStruct((sorted_rows, d), BF16),
        input_output_aliases={4: 0},
        compiler_params=arb1,
        name="moe_experts",
    )(slot0, slot1, sel, nt, xsort, weg, weu, wed, weg, weu, wed)

    nph = COMBINE_PHASES
    pstep = lambda i, j: jnp.minimum(i * nph + j, npc * nph - 1)
    sstep = lambda i, j: jnp.maximum((i - npc) * nph + j, 0)
    smem_chunk = pl.BlockSpec((1, 1, TOP_K * TC), lambda i, j: (i, 0, 0), memory_space=pltpu.SMEM)
    yp, ys = pl.pallas_call(
        functools.partial(_combine_kernel, n_prompt_chunks=npc, n_chunks=n_chunks),
        grid=(n_chunks, nph),
        in_specs=[_smem_spec(), _smem_spec(), _smem_spec(), smem_chunk, smem_chunk,
                  pl.BlockSpec((gpp, r, d), lambda i, j: (pstep(i, j), 0, 0)),
                  pl.BlockSpec((gpp, r, d), lambda i, j: (sstep(i, j), 0, 0)),
                  pl.BlockSpec((1, 1, d), lambda i, j: (pclamp(i) // cpb, 0, 0)),
                  pl.BlockSpec((gpp, 1, d), lambda i, j: (sstep(i, j), 0, 0)),
                  pl.BlockSpec((PHASE_ROWS, d), lambda i, j: (i * nph + j, 0)),
                  cst((1, d)),
                  pl.BlockSpec(memory_space=pl.ANY)],
        out_specs=[pl.BlockSpec((gpp, r, d), lambda i, j: (pstep(i, j), 0, 0)),
                   pl.BlockSpec((gpp, r, d), lambda i, j: (sstep(i, j), 0, 0))],
        out_shape=[jax.ShapeDtypeStruct(xp.shape, F32), jax.ShapeDtypeStruct(xs.shape, F32)],
        scratch_shapes=[pltpu.VMEM((PLANES * LOC_STRIDE, LANE), F32),
                        pltpu.VMEM((2, LOCAL_ROWS, d), BF16),
                        pltpu.VMEM((PLANES * Y_STRIDE, LANE), F32),
                        pltpu.SMEM((2,), jnp.int32),
                        pltpu.SemaphoreType.DMA, pltpu.SemaphoreType.DMA],
        compiler_params=arb2,
        name="moe_combine",
    )(counts, ebase, cbase, pos, wts, xp, xs, modp[:, :, 2 * d:3 * d], mods_g[:, :, 2 * d:3 * d],
      shared, g_final, eo)
    return yp.reshape(b, s_len, d), ys.reshape(db, ds, d)


def _t5_bucket(rel):
    nb = NUM_BUCKETS // 2
    max_exact = nb // 2
    ret = jnp.where(rel > 0, nb, 0)
    n = jnp.abs(rel)
    nf = jnp.maximum(n, 1).astype(F32)
    large = max_exact + (jnp.log(nf / max_exact) / math.log(REL_MAX_DIST / max_exact)
                         * (nb - max_exact)).astype(jnp.int32)
    large = jnp.minimum(large, nb - 1)
    return ret + jnp.where(n < max_exact, n, large)


def _pair_layout(per_head):
    groups = []
    for g in range(N_KV_HEADS):
        pars = []
        for par in range(2):
            pars.append(jnp.concatenate([per_head[4 * g + par], per_head[4 * g + par + 2]], axis=0))
        groups.append(jnp.stack(pars))
    return jnp.stack(groups)


REL_SPAN = 2 * LANE


def _rel_bias_kernel(tab_ref, o_ref):
    tab = tab_ref[...]
    for i in range(CHUNK):
        shift = CHUNK - 1 - i
        o_ref[i] = pltpu.roll(tab, (REL_SPAN - shift) % REL_SPAN, 1)[:, 0:BAND]


def _rel_bias(rel_table):
    rel = jnp.arange(REL_SPAN) - (WINDOW + CHUNK - 1)
    tab = rel_table[_t5_bucket(rel)].astype(F32).T
    out = pl.pallas_call(
        _rel_bias_kernel,
        in_specs=[pl.BlockSpec((N_HEADS, REL_SPAN), lambda: (0, 0))],
        out_specs=pl.BlockSpec((CHUNK, N_HEADS, BAND), lambda: (0, 0, 0)),
        out_shape=jax.ShapeDtypeStruct((CHUNK, N_HEADS, BAND), F32),
        name="rel_bias",
    )(tab)
    return jnp.transpose(out, (1, 0, 2))


def _pair_window_layout(bias):
    off = jnp.full((CHUNK, CHUNK), -1e30, F32)
    groups = []
    for g in range(N_KV_HEADS):
        pars = []
        for par in range(2):
            rows = []
            for hp in range(2):
                head = 4 * g + par + 2 * hp
                rows.append(jnp.concatenate([bias[head], off], axis=1))
                rows.append(jnp.concatenate([off, bias[head]], axis=1))
            pars.append(jnp.concatenate(rows, axis=0))
        groups.append(jnp.stack(pars))
    return jnp.stack(groups)


def kernel(x_prompt, x_sample, c_prompt, c_sample, cache_k, cache_v, state_pool, w_ada, b_ada, g_norm_mix, g_norm_ffn, w_in, w_pool_group, pool_scale, attn_sinks, w_pool_proj, w_attn_proj, w_out, rel_table, w_router, router_bias, w_exp_gate, w_exp_up, w_exp_down, w_sh_gate, w_sh_up, w_sh_down, g_final):
    b, s_len, d = x_prompt.shape
    db, ds, _ = x_sample.shape
    l = 0

    c_all = jnp.concatenate([c_prompt, c_sample, jnp.zeros((4, d), F32)], axis=0)
    mods = _ada(c_all, w_ada[l], b_ada[l])
    mod_p = mods[0:b].reshape(b, 1, 6 * d)
    mod_s = mods[b:b + db].reshape(db, 1, 6 * d)

    bias = _rel_bias(rel_table)
    bias4 = jnp.pad(_pair_layout(bias), ((0, 0), (0, 0), (0, 0), (0, PAIR_KEYS - BAND)), constant_values=-1e30)
    sink4 = _pair_layout(jnp.broadcast_to(attn_sinks[l].astype(F32)[:, None, None], (N_HEADS, CHUNK, 128)))
    w_pg = jnp.zeros((POOL_WIDTH, POOL_WIDTH), F32)
    for g in range(len(POOL_WINDOWS)):
        w_pg = w_pg.at[g * POOL_GW:(g + 1) * POOL_GW, g * POOL_GW:(g + 1) * POOL_GW].set(w_pool_group[l, g])
    w_pg = w_pg.astype(BF16)
    w_in_b = w_in[l].astype(BF16)
    w_pp = w_pool_proj[l].astype(BF16)
    w_ap = w_attn_proj[l].astype(BF16)
    w_out_b = w_out[l].astype(BF16)
    g_mix = g_norm_mix[l].reshape(1, d)
    g_ffn = g_norm_ffn[l].reshape(1, d)
    ps = pool_scale[l].reshape(1, POOL_WIDTH)

    bias_pw = jnp.swapaxes(_pair_window_layout(bias), 2, 3)
    sk = attn_sinks[l].astype(F32)
    sink_pw = jnp.concatenate([jnp.full((2 * CHUNK,), 1.0, F32) * sk[4 * g + par + 2 * hp]
                               for g in range(N_KV_HEADS) for par in range(2) for hp in range(2)]).reshape(1, -1)
    x1p, nk_p, nv_p, np_p = _mixer_prompt(x_prompt, mod_p, g_mix, w_in_b, w_pg, ps, bias_pw, sink_pw,
                                          w_pp, w_ap, w_out_b)
    ck = cache_k[l].reshape(db, WINDOW, KV_WIDTH)
    cv = cache_v[l].reshape(db, WINDOW, KV_WIDTH)
    sp16 = jnp.pad(state_pool[l], ((0, 0), (HIST - POOL_PAD, 0), (0, 0)))
    x1s, nk_s, nv_s, np_s = _mixer_sample(x_sample, mod_s, ck, cv, sp16, g_mix, w_in_b, w_pg, ps,
                                          bias4, sink4, w_pp, w_ap, w_out_b)

    y_p, y_s = _moe_sparse(x1p, x1s, mod_p[:, :, 3 * d:], mod_s[:, :, 3 * d:], g_ffn, g_final.reshape(1, d),
                           w_router[l].T.astype(BF16), router_bias[l].astype(F32).reshape(N_EXPERTS, 1),
                           w_sh_gate[l].astype(BF16), w_sh_up[l].astype(BF16), w_sh_down[l].astype(BF16),
                           w_exp_gate[l], w_exp_up[l], w_exp_down[l])

    kv_shape = lambda n: (1, n, WINDOW, N_KV_HEADS, HEAD_DIM)
    return (y_p, y_s,
            nk_p.reshape(kv_shape(b)), nv_p.reshape(kv_shape(b)),
            np_p[:, HIST - POOL_PAD:, :][None],
            nk_s.reshape(kv_shape(db)), nv_s.reshape(kv_shape(db)),
            np_s[:, HIST - POOL_PAD:, :][None])
```

```python
import functools
import math

import jax
import jax.numpy as jnp
from jax import lax
from jax.experimental import pallas as pl
from jax.experimental.pallas import tpu as pltpu

F32 = jnp.float32
BF16 = jnp.bfloat16

D_MODEL = 1024
CHUNK = 64
EPS = 1e-6
POOL_WIDTH = 512
POOL_WINDOWS = (2, 4, 8, 16)
POOL_GW = 128
POOL_PAD = 15
HEAD_DIM = 64
N_HEADS = 8
N_KV_HEADS = 2
ATTN_WIDTH = 512
KV_WIDTH = 128
WINDOW = 128
BAND = WINDOW + CHUNK
PAIR_KEYS = BAND + CHUNK
NUM_BUCKETS = 32
REL_MAX_DIST = 128
PAST_LEN = 4096
N_EXPERTS = 32
TOP_K = 4
N_GROUPS = 4
TOPK_GROUPS = 2
EXPERTS_PER_GROUP = 8
D_EXPERT = 256
ROUTED_SCALE = 2.5

OFF_U = 0
OFF_Q = OFF_U + POOL_WIDTH
OFF_K = OFF_Q + ATTN_WIDTH
OFF_V = OFF_K + KV_WIDTH
OFF_GP = OFF_V + KV_WIDTH
OFF_GA = OFF_GP + D_MODEL
IN_WIDTH = OFF_GA + D_MODEL

HIST = 16
MIX_ROWS = 512
MERGE_PANEL = 256
VMEM_LIMIT = 56 * 1024 * 1024

GROUP_ROWS = 64
TC = 1024
PAGE = 16
PAGE_SHIFT = 4
TILE_M = 512
PAGES_PER_TILE = TILE_M // PAGE
TILE_SHIFT = 5
LOCAL_ROWS = TOP_K * TC + N_EXPERTS * PAGE
LANE = 128
PLANES = D_MODEL // LANE
PLANE_PAD = 8
HP_STRIDE = TC + PLANE_PAD
LOC_STRIDE = LOCAL_ROWS + PLANE_PAD
COMBINE_PHASES = 4
PHASE_ROWS = TC // COMBINE_PHASES
RUN_COPY = 4
RUN_COPY_SHIFT = 2
Y_STRIDE = PHASE_ROWS + PLANE_PAD


def _dot(a, b):
    return jnp.dot(a, b, preferred_element_type=F32)


def _dot_nt(a, b):
    return lax.dot_general(a, b, (((1,), (1,)), ((), ())), preferred_element_type=F32)


def _norm_mod(x, g, shift, scale):
    ms = jnp.mean(x * x, axis=-1, keepdims=True)
    y = x * lax.rsqrt(ms + EPS) * g
    return y * (1.0 + scale) + shift


def _silu(x):
    return x * jax.nn.sigmoid(x)


def _ada_kernel(c_ref, w_ref, b_ref, o_ref):
    a = _silu(c_ref[...]).astype(BF16)
    o_ref[...] = _dot(a, w_ref[...].astype(BF16)) + b_ref[...]


def _ada(c_all, w_ada, b_ada):
    rows = c_all.shape[0]
    n = w_ada.shape[1]
    tn = 768
    return pl.pallas_call(
        _ada_kernel,
        grid=(n // tn,),
        in_specs=[
            pl.BlockSpec((rows, D_MODEL), lambda j: (0, 0)),
            pl.BlockSpec((D_MODEL, tn), lambda j: (0, j)),
            pl.BlockSpec((1, tn), lambda j: (0, j)),
        ],
        out_specs=pl.BlockSpec((rows, tn), lambda j: (0, j)),
        out_shape=jax.ShapeDtypeStruct((rows, n), F32),
        compiler_params=pltpu.CompilerParams(dimension_semantics=("arbitrary",)),
        name="ada",
    )(c_all, w_ada, b_ada.reshape(1, n))


def _pool_feats(ue, pos0, rows):
    s2 = ue + pltpu.roll(ue, 1, 0)
    s4 = s2 + pltpu.roll(s2, 2, 0)
    s8 = s4 + pltpu.roll(s4, 4, 0)
    s16 = s8 + pltpu.roll(s8, 8, 0)
    pos = pos0 + lax.broadcasted_iota(jnp.int32, (rows, POOL_GW), 0)
    outs = []
    for g, (w, sw) in enumerate(zip(POOL_WINDOWS, (s2, s4, s8, s16))):
        sl = slice(g * POOL_GW, (g + 1) * POOL_GW)
        cnt = jnp.minimum(pos + 1, w).astype(F32)
        outs.append(sw[HIST:, sl] / cnt - ue[HIST:, sl])
    return jnp.concatenate(outs, axis=1)


def _kv_variants(t):
    lane = lax.broadcasted_iota(jnp.int32, t.shape, 1)
    low = lane < HEAD_DIM
    swapped = pltpu.roll(t, HEAD_DIM, 1)
    zero = jnp.zeros_like(t)
    return (jnp.where(low, t, zero), jnp.where(low, zero, swapped),
            jnp.where(low, swapped, zero), jnp.where(low, zero, t))


def _merge_out(h, pool_y, attn_y, win_ref, wpp_ref, wap_ref, wout_ref, m_scr):
    rows = h.shape[0]
    pool_b = pool_y.astype(BF16)
    attn_b = attn_y.astype(BF16)
    for n in range(D_MODEL // MERGE_PANEL):
        lo, hi = n * MERGE_PANEL, (n + 1) * MERGE_PANEL
        gp = _dot(h, win_ref[:, OFF_GP + lo:OFF_GP + hi])
        ga = _dot(h, win_ref[:, OFF_GA + lo:OFF_GA + hi])
        pp = _dot(pool_b, wpp_ref[:, lo:hi])
        ap = _dot(attn_b, wap_ref[:, lo:hi])
        m_scr[0:rows, lo:hi] = (jax.nn.sigmoid(gp) * pp + jax.nn.sigmoid(ga) * ap).astype(BF16)
    return _dot(m_scr[0:rows, :], wout_ref[...])


def _mixer_prompt_kernel(x_ref, mod_ref, g_ref, win_ref, wpg_ref, ps_ref, bias_ref, sink_ref,
                         wpp_ref, wap_ref, wout_ref,
                         x1_ref, nk_ref, nv_ref, npool_ref,
                         uext, k0, k1, k2, k3, v0, v1, v2, v3, m_scr, s_scr, p_scr, a_scr):
    ts = MIX_ROWS
    s = pl.program_id(1)
    kext = (k0, k1, k2, k3)
    vext = (v0, v1, v2, v3)

    @pl.when(s == 0)
    def _():
        uext[0:HIST, :] = jnp.zeros((HIST, POOL_WIDTH), F32)
        for r in kext:
            r[0:WINDOW, :] = jnp.zeros((WINDOW, KV_WIDTH), BF16)
        for r in vext:
            r[:, 0:WINDOW] = jnp.zeros((KV_WIDTH, WINDOW), BF16)

    x = x_ref[0]
    shift = mod_ref[0, :, 0:D_MODEL]
    scale = mod_ref[0, :, D_MODEL:2 * D_MODEL]
    gate = mod_ref[0, :, 2 * D_MODEL:3 * D_MODEL]
    h = _norm_mod(x, g_ref[...], shift, scale).astype(BF16)

    u = _dot(h, win_ref[:, OFF_U:OFF_U + POOL_WIDTH])
    q = (_dot(h, win_ref[:, OFF_Q:OFF_Q + ATTN_WIDTH]) * (HEAD_DIM ** -0.5)).astype(BF16)
    kv = _dot(h, win_ref[:, OFF_K:OFF_V + KV_WIDTH])
    k = kv[:, 0:KV_WIDTH]
    v = kv[:, KV_WIDTH:2 * KV_WIDTH]
    nk_ref[0] = k[ts - WINDOW:ts, :]
    nv_ref[0] = v[ts - WINDOW:ts, :]
    npool_ref[0] = u[ts - HIST:ts, :]

    pos0 = s * ts
    uext[HIST:HIST + ts, :] = u
    p = _pool_feats(uext[...], pos0, ts)
    uext[0:HIST, :] = u[ts - HIST:ts, :]
    pool_y = _dot(p.astype(BF16), wpg_ref[...]) * ps_ref[...]

    for r, t in zip(kext, _kv_variants(k)):
        r[WINDOW:WINDOW + ts, :] = t.astype(BF16)
    for r, t in zip(vext, _kv_variants(v)):
        r[:, WINDOW:WINDOW + ts] = t.T.astype(BF16)
    pair_rows = 2 * CHUNK
    blk_q = 2 * pair_rows
    n_pairs = ts // pair_rows
    per_pair = N_KV_HEADS * 2
    key_i = lax.broadcasted_iota(jnp.int32, (PAIR_KEYS, blk_q), 0)
    for pr in range(n_pairs):
        lo = pr * pair_rows
        valid = (pos0 + lo + key_i) >= WINDOW
        for g in range(N_KV_HEADS):
            qs = jnp.concatenate([q[lo:lo + pair_rows, (2 * g) * LANE:(2 * g + 1) * LANE],
                                  q[lo:lo + pair_rows, (2 * g + 1) * LANE:(2 * g + 2) * LANE]], axis=0)
            for par in range(2):
                s = _dot_nt(kext[2 * g + par][lo:lo + PAIR_KEYS, :], qs) + bias_ref[g, par]
                s_scr[(pr * N_KV_HEADS + g) * 2 + par] = jnp.where(valid, s, -1e30)
    sink = sink_ref[...]
    for pr in range(n_pairs):
        s = jnp.concatenate([s_scr[pr * per_pair + b] for b in range(per_pair)], axis=1)
        m = jnp.maximum(jnp.max(s, axis=0, keepdims=True), sink)
        e = jnp.exp(s - m)
        denom = jnp.sum(e, axis=0, keepdims=True) + jnp.exp(sink - m)
        p = (e / denom).astype(BF16)
        for b in range(per_pair):
            p_scr[pr * per_pair + b] = p[:, b * blk_q:(b + 1) * blk_q]
    for pr in range(n_pairs):
        lo = pr * pair_rows
        for g in range(N_KV_HEADS):
            blk = (pr * N_KV_HEADS + g) * 2
            ot = (_dot(vext[2 * g][:, lo:lo + PAIR_KEYS], p_scr[blk])
                  + _dot(vext[2 * g + 1][:, lo:lo + PAIR_KEYS], p_scr[blk + 1]))
            o = ot.T
            a_scr[lo:lo + pair_rows, (2 * g) * LANE:(2 * g + 1) * LANE] = o[0:pair_rows].astype(BF16)
            a_scr[lo:lo + pair_rows, (2 * g + 1) * LANE:(2 * g + 2) * LANE] = o[pair_rows:].astype(BF16)
    for r in kext:
        r[0:WINDOW, :] = r[ts:ts + WINDOW, :]
    for r in vext:
        r[:, 0:WINDOW] = r[:, ts:ts + WINDOW]

    y = _merge_out(h, pool_y, a_scr[...], win_ref, wpp_ref, wap_ref, wout_ref, m_scr)
    x1_ref[0] = x + gate * y


def _const_spec(shape):
    nd = len(shape)
    return pl.BlockSpec(shape, lambda *_: (0,) * nd)


def _mixer_prompt(x, mod, g_mix, w_in, w_pg, pool_scale, bias4, sink4, w_pp, w_ap, w_out):
    b, s_len, d = x.shape
    ts = MIX_ROWS
    pair_rows = 2 * CHUNK
    n_blocks = (ts // pair_rows) * N_KV_HEADS * 2
    kv_scratch = ([pltpu.VMEM((WINDOW + ts, KV_WIDTH), BF16) for _ in range(4)]
                  + [pltpu.VMEM((KV_WIDTH, WINDOW + ts), BF16) for _ in range(4)])
    return pl.pallas_call(
        _mixer_prompt_kernel,
        grid=(b, s_len // ts),
        in_specs=[
            pl.BlockSpec((1, ts, d), lambda i, j: (i, j, 0)),
            pl.BlockSpec((1, 1, 6 * d), lambda i, j: (i, 0, 0)),
            _const_spec((1, d)),
            _const_spec((d, IN_WIDTH)),
            _const_spec((POOL_WIDTH, POOL_WIDTH)),
            _const_spec((1, POOL_WIDTH)),
            _const_spec(bias4.shape),
            _const_spec(sink4.shape),
            _const_spec((POOL_WIDTH, d)),
            _const_spec((ATTN_WIDTH, d)),
            _const_spec((d, d)),
        ],
        out_specs=[
            pl.BlockSpec((1, ts, d), lambda i, j: (i, j, 0)),
            pl.BlockSpec((1, WINDOW, KV_WIDTH), lambda i, j: (i, 0, 0)),
            pl.BlockSpec((1, WINDOW, KV_WIDTH), lambda i, j: (i, 0, 0)),
            pl.BlockSpec((1, HIST, POOL_WIDTH), lambda i, j: (i, 0, 0)),
        ],
        out_shape=[
            jax.ShapeDtypeStruct((b, s_len, d), F32),
            jax.ShapeDtypeStruct((b, WINDOW, KV_WIDTH), F32),
            jax.ShapeDtypeStruct((b, WINDOW, KV_WIDTH), F32),
            jax.ShapeDtypeStruct((b, HIST, POOL_WIDTH), F32),
        ],
        scratch_shapes=[pltpu.VMEM((HIST + ts, POOL_WIDTH), F32)] + kv_scratch
                       + [pltpu.VMEM((ts, d), BF16),
                          pltpu.VMEM((n_blocks, 2 * pair_rows, PAIR_KEYS), F32),
                          pltpu.VMEM((n_blocks, 2 * pair_rows, PAIR_KEYS), BF16),
                          pltpu.VMEM((ts, ATTN_WIDTH), BF16)],
        compiler_params=pltpu.CompilerParams(
            dimension_semantics=("arbitrary", "arbitrary"), vmem_limit_bytes=VMEM_LIMIT),
        name="mixer_prompt",
    )(x, mod, g_mix, w_in, w_pg, pool_scale, bias4, sink4, w_pp, w_ap, w_out)


def _mixer_sample_kernel(x_ref, mod_ref, ck_ref, cv_ref, sp_ref, g_ref, win_ref, wpg_ref, ps_ref,
                         bias_ref, sink_ref, wpp_ref, wap_ref, wout_ref,
                         x1_ref, nk_ref, nv_ref, npool_ref, m_scr, ks, vs, s_scr, p_scr, a_scr):
    nseq, ds, d = x_ref.shape
    rows = nseq * ds
    x = x_ref[...]
    shift = mod_ref[:, :, 0:D_MODEL]
    scale = mod_ref[:, :, D_MODEL:2 * D_MODEL]
    gate = mod_ref[:, :, 2 * D_MODEL:3 * D_MODEL]
    h = _norm_mod(x, g_ref[...], shift, scale).reshape(rows, d).astype(BF16)

    u = _dot(h, win_ref[:, OFF_U:OFF_U + POOL_WIDTH])
    q = (_dot(h, win_ref[:, OFF_Q:OFF_Q + ATTN_WIDTH]) * (HEAD_DIM ** -0.5)).astype(BF16)
    kv = _dot(h, win_ref[:, OFF_K:OFF_V + KV_WIDTH])
    k = kv[:, 0:KV_WIDTH]
    v = kv[:, KV_WIDTH:2 * KV_WIDTH]

    pool_parts = []
    key_pad = jnp.zeros((PAIR_KEYS - BAND, KV_WIDTH), F32)
    for b in range(nseq):
        lo = b * ds
        ub = u[lo:lo + ds, :]
        kb = k[lo:lo + ds, :]
        vb = v[lo:lo + ds, :]
        ck = ck_ref[b]
        cv = cv_ref[b]
        nk_ref[b] = jnp.concatenate([ck[ds:WINDOW, :], kb], axis=0)
        nv_ref[b] = jnp.concatenate([cv[ds:WINDOW, :], vb], axis=0)
        npool_ref[b] = ub[ds - HIST:ds, :]
        ue = jnp.concatenate([sp_ref[b], ub], axis=0)
        pool_parts.append(_pool_feats(ue, PAST_LEN, ds))
        for var, t in enumerate(_kv_variants(jnp.concatenate([ck, kb, key_pad], axis=0))):
            ks[var, b] = t.astype(BF16)
        for var, t in enumerate(_kv_variants(jnp.concatenate([cv, vb, key_pad], axis=0))):
            vs[var, b] = t.astype(BF16)
    p = jnp.concatenate(pool_parts, axis=0)
    pool_y = _dot(p.astype(BF16), wpg_ref[...]) * ps_ref[...]

    per_seq = N_KV_HEADS * 2
    for b in range(nseq):
        lo = b * ds
        for g in range(N_KV_HEADS):
            qs = jnp.concatenate([q[lo:lo + ds, (2 * g) * LANE:(2 * g + 1) * LANE],
                                  q[lo:lo + ds, (2 * g + 1) * LANE:(2 * g + 2) * LANE]], axis=0)
            for par in range(2):
                s_scr[b * per_seq + 2 * g + par] = _dot_nt(qs, ks[2 * g + par, b]) + bias_ref[g, par]
    group = 2
    grp_rows = group * per_seq * 2 * ds
    sink1 = sink_ref[...].reshape(per_seq * 2 * ds, LANE)[:, 0:1]
    sink = jnp.concatenate([sink1] * group, axis=0)
    for gi in range(nseq // group):
        blks = slice(gi * group * per_seq, (gi + 1) * group * per_seq)
        s = s_scr[blks].reshape(grp_rows, PAIR_KEYS)
        m = jnp.maximum(jnp.max(s, axis=-1, keepdims=True), sink)
        e = jnp.exp(s - m)
        esum = _dot(e.astype(BF16), jnp.ones((PAIR_KEYS, LANE), BF16))
        denom = esum + jnp.exp(sink - m)
        p_scr[blks] = ((e / jnp.concatenate([denom, denom], axis=1)).astype(BF16)
                       .reshape(group * per_seq, 2 * ds, PAIR_KEYS))
    for b in range(nseq):
        lo = b * ds
        for g in range(N_KV_HEADS):
            blk = b * per_seq + 2 * g
            o = _dot(p_scr[blk], vs[2 * g, b]) + _dot(p_scr[blk + 1], vs[2 * g + 1, b])
            a_scr[lo:lo + ds, (2 * g) * LANE:(2 * g + 1) * LANE] = o[0:ds].astype(BF16)
            a_scr[lo:lo + ds, (2 * g + 1) * LANE:(2 * g + 2) * LANE] = o[ds:].astype(BF16)

    y = _merge_out(h, pool_y, a_scr[...], win_ref, wpp_ref, wap_ref, wout_ref, m_scr)
    x1_ref[...] = x + gate * y.reshape(nseq, ds, d)


def _mixer_sample(x, mod, ck, cv, sp16, g_mix, w_in, w_pg, pool_scale, bias4, sink4, w_pp, w_ap, w_out):
    db, ds, d = x.shape
    nseq = MIX_ROWS // ds
    seq_spec = lambda shape: pl.BlockSpec((nseq,) + shape, lambda i: (i, 0, 0))
    return pl.pallas_call(
        _mixer_sample_kernel,
        grid=(db // nseq,),
        in_specs=[
            seq_spec((ds, d)),
            seq_spec((1, 6 * d)),
            seq_spec((WINDOW, KV_WIDTH)),
            seq_spec((WINDOW, KV_WIDTH)),
            seq_spec((HIST, POOL_WIDTH)),
            _const_spec((1, d)),
            _const_spec((d, IN_WIDTH)),
            _const_spec((POOL_WIDTH, POOL_WIDTH)),
            _const_spec((1, POOL_WIDTH)),
            _const_spec(bias4.shape),
            _const_spec(sink4.shape),
            _const_spec((POOL_WIDTH, d)),
            _const_spec((ATTN_WIDTH, d)),
            _const_spec((d, d)),
        ],
        out_specs=[
            seq_spec((ds, d)),
            seq_spec((WINDOW, KV_WIDTH)),
            seq_spec((WINDOW, KV_WIDTH)),
            seq_spec((HIST, POOL_WIDTH)),
        ],
        out_shape=[
            jax.ShapeDtypeStruct((db, ds, d), F32),
            jax.ShapeDtypeStruct((db, WINDOW, KV_WIDTH), F32),
            jax.ShapeDtypeStruct((db, WINDOW, KV_WIDTH), F32),
            jax.ShapeDtypeStruct((db, HIST, POOL_WIDTH), F32),
        ],
        scratch_shapes=[pltpu.VMEM((MIX_ROWS, d), BF16),
                        pltpu.VMEM((4, nseq, PAIR_KEYS, KV_WIDTH), BF16),
                        pltpu.VMEM((4, nseq, PAIR_KEYS, KV_WIDTH), BF16),
                        pltpu.VMEM((nseq * N_KV_HEADS * 2, 2 * ds, PAIR_KEYS), F32),
                        pltpu.VMEM((nseq * N_KV_HEADS * 2, 2 * ds, PAIR_KEYS), BF16),
                        pltpu.VMEM((MIX_ROWS, ATTN_WIDTH), BF16)],
        compiler_params=pltpu.CompilerParams(
            dimension_semantics=("arbitrary",), vmem_limit_bytes=VMEM_LIMIT),
        name="mixer_sample",
    )(x, mod, ck, cv, sp16, g_mix, w_in, w_pg, pool_scale, bias4, sink4, w_pp, w_ap, w_out)


def _first_argmax(vals, idx, size):
    m = jnp.max(vals, axis=0, keepdims=True)
    first = jnp.min(jnp.where(vals == m, idx, size), axis=0, keepdims=True)
    return m, first


def _router_picks(h, wr_ref, rb_ref):
    t = h.shape[0]
    logits = _dot_nt(wr_ref[...], h)
    scores = jax.nn.sigmoid(logits)
    sel = scores + rb_ref[...]
    neg = jnp.float32(-jnp.inf)
    eidx = lax.broadcasted_iota(jnp.int32, (N_EXPERTS, t), 0)

    gscores = []
    bidx = lax.broadcasted_iota(jnp.int32, (EXPERTS_PER_GROUP, t), 0)
    for g in range(N_GROUPS):
        blk = sel[g * EXPERTS_PER_GROUP:(g + 1) * EXPERTS_PER_GROUP, :]
        m1, i1 = _first_argmax(blk, bidx, EXPERTS_PER_GROUP)
        m2 = jnp.max(jnp.where(bidx == i1, neg, blk), axis=0, keepdims=True)
        gscores.append(m1 + m2)
    gs = jnp.concatenate(gscores, axis=0)
    gidx = lax.broadcasted_iota(jnp.int32, (N_GROUPS, t), 0)
    assert TOPK_GROUPS == 2
    _, g1 = _first_argmax(gs, gidx, N_GROUPS)
    _, g2 = _first_argmax(jnp.where(gidx == g1, neg, gs), gidx, N_GROUPS)
    egrp = eidx // EXPERTS_PER_GROUP
    cand = jnp.where((egrp == g1) | (egrp == g2), sel, neg)

    picks = []
    for _ in range(TOP_K):
        _, i = _first_argmax(cand, eidx, N_EXPERTS)
        hit = eidx == i
        picks.append(hit)
        cand = jnp.where(hit, neg, cand)
    return picks, scores


def _swiglu_rows(h, wg, wu, wd):
    act = _silu(_dot(h, wg)) * _dot(h, wu)
    return _dot(act.astype(BF16), wd)


def _route_kernel(xp_ref, xs_ref, modp_ref, mods_ref, g_ref, wr_ref, rb_ref, tri_ref,
                  wsg_ref, wsu_ref, wsd_ref, h_ref, pos_ref, w_ref, cnt_ref, sh_ref, *, n_prompt_chunks):
    is_prompt = pl.program_id(0) < n_prompt_chunks
    x = jnp.where(is_prompt, xp_ref[...], xs_ref[...])
    sh, sc = slice(3 * D_MODEL, 4 * D_MODEL), slice(4 * D_MODEL, 5 * D_MODEL)
    shift = jnp.where(is_prompt, modp_ref[:, :, sh], mods_ref[:, :, sh])
    scale = jnp.where(is_prompt, modp_ref[:, :, sc], mods_ref[:, :, sc])
    h = _norm_mod(x, g_ref[...], shift, scale).reshape(TC, D_MODEL).astype(BF16)
    h_ref[...] = h
    sh_ref[...] = _swiglu_rows(h, wsg_ref[...], wsu_ref[...], wsd_ref[...])
    picks, scores = _router_picks(h, wr_ref, rb_ref)
    picked = picks[0] | picks[1] | picks[2] | picks[3]
    cnt = _dot(jnp.where(picked, 1.0, 0.0).astype(BF16), tri_ref[...])
    n_e = jnp.max(cnt, axis=1, keepdims=True).astype(jnp.int32)
    n_b = jnp.broadcast_to(n_e, (N_EXPERTS, LANE))
    cnt_ref[0] = n_b
    padded = lax.shift_left(lax.shift_right_logical(n_b + (PAGE - 1), PAGE_SHIFT), PAGE_SHIFT)
    row = lax.broadcasted_iota(jnp.int32, (N_EXPERTS, LANE), 0)
    scan = padded
    for s in (1, 2, 4, 8, 16):
        scan = scan + jnp.where(row >= s, pltpu.roll(scan, s, 0), 0)
    base = (scan - padded)[:, 0:1].astype(F32)
    slot = base + cnt - 1.0
    pos = [jnp.sum(jnp.where(p, slot, 0.0), axis=0, keepdims=True) for p in picks]
    wk = [jnp.sum(jnp.where(p, scores, 0.0), axis=0, keepdims=True) for p in picks]
    wsum = wk[0] + wk[1] + wk[2] + wk[3]
    pos_ref[0] = jnp.concatenate(pos, axis=0).astype(jnp.int32)
    w_ref[0] = jnp.concatenate(wk, axis=0) / wsum * ROUTED_SCALE


def _plan_kernel(cnt_ref, te_ref, nt_ref, ebase_ref, cbase_ref, etp_ref, slot0_ref, slot1_ref, sel_ref,
                 *, n_chunks, nt_max):
    def per_expert(e, tile0):
        def per_chunk(c, acc):
            cbase_ref[c * N_EXPERTS + e] = acc
            return acc + lax.shift_right_logical(cnt_ref[c, e] + (PAGE - 1), PAGE_SHIFT)
        tp = lax.fori_loop(0, n_chunks, per_chunk, 0)
        etp_ref[e] = tp
        ebase_ref[e] = tile0 * PAGES_PER_TILE
        nt = lax.shift_right_logical(tp + (PAGES_PER_TILE - 1), TILE_SHIFT)

        def fill(j, carry):
            te_ref[tile0 + j] = e
            return carry
        lax.fori_loop(0, nt, fill, 0)
        return tile0 + nt
    total = lax.fori_loop(0, N_EXPERTS, per_expert, 0)
    nt_ref[0] = total

    def tail(j, carry):
        te_ref[j] = N_EXPERTS - 1
        return carry
    lax.fori_loop(total, nt_max, tail, 0)

    def per_step(i, carry):
        c0, c1 = carry
        used = 2 * i < total
        a = te_ref[jnp.minimum(2 * i, total - 1)]
        b = te_ref[jnp.minimum(2 * i + 1, total - 1)]
        hit_a = (a == c0) | (a == c1)
        evict0 = hit_a | (c0 != b)
        n0 = jnp.where(used & ~hit_a & evict0, a, c0)
        n1 = jnp.where(used & ~hit_a & ~evict0, a, c1)
        sa = jnp.where(a == n0, 0, 1)
        hit_b = (b == n0) | (b == n1)
        m0 = jnp.where(used & ~hit_b & (sa == 1), b, n0)
        m1 = jnp.where(used & ~hit_b & (sa == 0), b, n1)
        slot0_ref[i] = jnp.maximum(m0, 0)
        slot1_ref[i] = jnp.maximum(m1, 0)
        sel_ref[2 * i] = sa
        sel_ref[2 * i + 1] = jnp.where(b == m0, 0, 1)
        return (m0, m1)
    lax.fori_loop(0, nt_max // 2, per_step, (-1, -1))


def _page_copy(src, src_page, dst, dst_page, sem, pages=1):
    rows = pages * PAGE
    s = src.at[pl.ds(pl.multiple_of(src_page * PAGE, PAGE), rows), :]
    t = dst.at[pl.ds(pl.multiple_of(dst_page * PAGE, PAGE), rows), :]
    return pltpu.make_async_copy(s, t, sem)


def _for_each_run(c, cnt_ref, ebase_ref, cbase_ref, fn):
    def per_expert(e, carry):
        lpage, nbig, nsmall = carry
        n = cnt_ref[c, e]
        npages = lax.shift_right_logical(n + (PAGE - 1), PAGE_SHIFT)
        fn(n, lpage, ebase_ref[e] + cbase_ref[c * N_EXPERTS + e], npages)
        return (lpage + npages, nbig + lax.shift_right_logical(npages, RUN_COPY_SHIFT),
                nsmall + (npages & (RUN_COPY - 1)))
    return lax.fori_loop(0, N_EXPERTS, per_expert, (0, 0, 0))


def _start_run_copies(src, src_page, dst, dst_page, npages, sem_big, sem_small):
    nbig = lax.shift_right_logical(npages, RUN_COPY_SHIFT)

    def big(j, carry):
        _page_copy(src, src_page + j * RUN_COPY, dst, dst_page + j * RUN_COPY, sem_big, RUN_COPY).start(priority=0)
        return carry
    lax.fori_loop(0, nbig, big, 0)

    def small(j, carry):
        _page_copy(src, src_page + j, dst, dst_page + j, sem_small).start(priority=1)
        return carry
    lax.fori_loop(nbig * RUN_COPY, npages, small, 0)


def _wait_copies(src, dst, nbig, nsmall, sem_big, sem_small):
    def big(j, carry):
        _page_copy(src, 0, dst, 0, sem_big, RUN_COPY).wait()
        return carry
    lax.fori_loop(0, nbig, big, 0)

    def small(j, carry):
        _page_copy(src, 0, dst, 0, sem_small).wait()
        return carry
    lax.fori_loop(0, nsmall, small, 0)


def _row_slab(ref, row, stride):
    return ref.at[pl.ds(row, PLANES, stride=stride), :]


def _dispatch_kernel(cnt_ref, ebase_ref, cbase_ref, etp_ref, nt_ref, pos_ref, h_ref, xs_hbm,
                     hpl, loc, locb, zpage, ztile, pend, sem, bsem, tsem, *, n_chunks, nt_max):
    c = pl.program_id(0)

    @pl.when(c == 0)
    def _():
        pend[0] = 0
        pend[1] = 0
        loc[...] = jnp.zeros(loc.shape, F32)

    h = h_ref[...]
    for ch in range(PLANES):
        hpl[ch * HP_STRIDE:ch * HP_STRIDE + TC, :] = h[:, ch * LANE:(ch + 1) * LANE].astype(F32)
    zpage[...] = jnp.zeros(zpage.shape, BF16)
    ztile[...] = jnp.zeros(ztile.shape, BF16)

    def scatter(t8, carry):
        for u in range(8):
            t = t8 * 8 + u
            slab = _row_slab(hpl, t, HP_STRIDE)[...]
            for k in range(TOP_K):
                _row_slab(loc, pos_ref[0, 0, k * TC + t], LOC_STRIDE)[...] = slab
        return carry
    lax.fori_loop(0, TC // 8, scatter, 0)

    _wait_copies(locb, xs_hbm, pend[0], pend[1], bsem, sem)
    for ch in range(PLANES):
        locb[:, ch * LANE:(ch + 1) * LANE] = loc[ch * LOC_STRIDE:ch * LOC_STRIDE + LOCAL_ROWS, :].astype(BF16)

    def send_run(n, lpage, gpage, npages):
        _start_run_copies(locb, lpage, xs_hbm, gpage, npages, bsem, sem)
    _, n_big, n_small = _for_each_run(c, cnt_ref, ebase_ref, cbase_ref, send_run)
    pend[0] = n_big
    pend[1] = n_small

    @pl.when(c == n_chunks - 1)
    def _():
        def pad_expert(e, n):
            tp = etp_ref[e]
            full = lax.shift_left(lax.shift_right_logical(tp + (PAGES_PER_TILE - 1), TILE_SHIFT), TILE_SHIFT)

            def per_page(j, carry):
                _page_copy(zpage, 0, xs_hbm, ebase_ref[e] + j, sem).start()
                return carry
            lax.fori_loop(tp, full, per_page, 0)
            return n + full - tp
        n_pad = lax.fori_loop(0, N_EXPERTS, pad_expert, 0)

        def tile_copy(i):
            dst = xs_hbm.at[pl.ds(pl.multiple_of(i * TILE_M, TILE_M), TILE_M), :]
            return pltpu.make_async_copy(ztile, dst, tsem)

        def tail_start(i, carry):
            tile_copy(i).start()
            return carry
        lax.fori_loop(nt_ref[0], nt_max, tail_start, 0)
        _wait_copies(locb, xs_hbm, n_big, n_small + n_pad, bsem, sem)

        def tail_wait(i, carry):
            tile_copy(i).wait()
            return carry
        lax.fori_loop(nt_ref[0], nt_max, tail_wait, 0)


def _expert_kernel(slot0_ref, slot1_ref, sel_ref, nt_ref, xs_ref, wg0_ref, wu0_ref, wd0_ref,
                   wg1_ref, wu1_ref, wd1_ref, o_ref, wgb, wub, wdb):
    i = pl.program_id(0)
    slots = ((slot0_ref, wg0_ref, wu0_ref, wd0_ref), (slot1_ref, wg1_ref, wu1_ref, wd1_ref))

    @pl.when(2 * i < nt_ref[0])
    def _():
        for slot, (idx_ref, wg_ref, wu_ref, wd_ref) in enumerate(slots):
            @pl.when((i == 0) | (idx_ref[i] != idx_ref[jnp.maximum(i - 1, 0)]))
            def _():
                wgb[slot] = wg_ref[0].astype(BF16)
                wub[slot] = wu_ref[0].astype(BF16)
                wdb[slot] = wd_ref[0].astype(BF16)
        for t in range(2):
            rows = slice(t * TILE_M, (t + 1) * TILE_M)
            sl = sel_ref[2 * i + t]
            o_ref[rows, :] = _swiglu_rows(xs_ref[rows, :], wgb[sl], wub[sl], wdb[sl]).astype(BF16)


def _combine_kernel(cnt_ref, ebase_ref, cbase_ref, pos_ref, w_ref, xp_ref, xs_ref, gp_ref, gs_ref,
                    sh_ref, gf_ref, eo_hbm, yp_ref, ys_ref,
                    loc, locb, ypl, pend, sem, bsem, *, n_prompt_chunks, n_chunks):
    c = pl.program_id(0)
    ph = pl.program_id(1)

    def fetch(chunk):
        buf = locb.at[chunk % 2]

        def fetch_run(n, lpage, gpage, npages):
            _start_run_copies(eo_hbm, gpage, buf, lpage, npages, bsem, sem)
        _, n_big, n_small = _for_each_run(chunk, cnt_ref, ebase_ref, cbase_ref, fetch_run)
        pend[0] = n_big
        pend[1] = n_small

    @pl.when((c == 0) & (ph == 0))
    def _():
        locb[...] = jnp.zeros(locb.shape, BF16)
        fetch(c)

    @pl.when(ph == 0)
    def _():
        cur = locb.at[c % 2]
        _wait_copies(eo_hbm, cur, pend[0], pend[1], bsem, sem)

        @pl.when(c + 1 < n_chunks)
        def _():
            fetch(c + 1)
        for ch in range(PLANES):
            loc[ch * LOC_STRIDE:ch * LOC_STRIDE + LOCAL_ROWS, :] = cur[:, ch * LANE:(ch + 1) * LANE].astype(F32)

    shared = sh_ref[...]
    t0 = ph * PHASE_ROWS

    def gather(t8, carry):
        for u in range(8):
            t = t8 * 8 + u
            acc = jnp.zeros((PLANES, LANE), F32)
            for k in range(TOP_K):
                idx = k * TC + t0 + t
                acc = acc + w_ref[0, 0, idx] * _row_slab(loc, pos_ref[0, 0, idx], LOC_STRIDE)[...]
            _row_slab(ypl, t, Y_STRIDE)[...] = acc
        return carry
    lax.fori_loop(0, PHASE_ROWS // 8, gather, 0)
    routed = jnp.concatenate([ypl[ch * Y_STRIDE:ch * Y_STRIDE + PHASE_ROWS, :] for ch in range(PLANES)], axis=1)
    f = shared + routed

    def finish(x_ref, gate_ref, y_ref):
        xo = x_ref[...] + gate_ref[:, :, 5 * D_MODEL:6 * D_MODEL] * f.reshape(x_ref.shape)
        ms = jnp.mean(xo * xo, axis=-1, keepdims=True)
        y_ref[...] = xo * lax.rsqrt(ms + EPS) * gf_ref[...]

    @pl.when(c < n_prompt_chunks)
    def _():
        finish(xp_ref, gp_ref, yp_ref)

    @pl.when(c >= n_prompt_chunks)
    def _():
        finish(xs_ref, gs_ref, ys_ref)


def _smem_spec():
    return pl.BlockSpec(memory_space=pltpu.SMEM)


def _moe_sparse(x1p, x1s, modp, mods, g_ffn, g_final, w_r_t, rb, wsg, wsu, wsd, weg, weu, wed):
    b, s_len, d = x1p.shape
    db, ds, _ = x1s.shape
    r = GROUP_ROWS
    xp = x1p.reshape(b * s_len // r, r, d)
    xs = x1s.reshape(db * ds // r, r, d)
    mods_g = jnp.repeat(mods, ds // r, axis=0) if ds != r else mods
    npc = b * s_len // TC
    nsc = db * ds // TC
    n_chunks = npc + nsc
    n_tok = n_chunks * TC
    cpb = s_len // TC
    gpc = TC // r
    gpp = PHASE_ROWS // r
    max_pages = (TOP_K * n_tok) // PAGE + n_chunks * N_EXPERTS
    nt_max = -(-(max_pages // PAGES_PER_TILE + N_EXPERTS) // 8) * 8
    sorted_rows = nt_max * TILE_M
    arb1 = pltpu.CompilerParams(dimension_semantics=("arbitrary",), vmem_limit_bytes=VMEM_LIMIT)
    arb2 = pltpu.CompilerParams(dimension_semantics=("arbitrary", "arbitrary"), vmem_limit_bytes=VMEM_LIMIT)
    cst = lambda shape: pl.BlockSpec(shape, lambda *_: (0,) * len(shape))
    tri = jnp.triu(jnp.ones((TC, TC), BF16))
    pclamp = lambda i: jnp.minimum(i, npc - 1)
    sclamp = lambda i: jnp.maximum(i - npc, 0)

    h2, pos, wts, cnt, shared = pl.pallas_call(
        functools.partial(_route_kernel, n_prompt_chunks=npc),
        grid=(n_chunks,),
        in_specs=[
            pl.BlockSpec((gpc, r, d), lambda i: (pclamp(i), 0, 0)),
            pl.BlockSpec((gpc, r, d), lambda i: (sclamp(i), 0, 0)),
            pl.BlockSpec((1, 1, 6 * d), lambda i: (pclamp(i) // cpb, 0, 0)),
            pl.BlockSpec((gpc, 1, 6 * d), lambda i: (sclamp(i), 0, 0)),
            cst((1, d)), cst((N_EXPERTS, d)), cst((N_EXPERTS, 1)), cst((TC, TC)),
            cst((d, D_EXPERT)), cst((d, D_EXPERT)), cst((D_EXPERT, d)),
        ],
        out_specs=[
            pl.BlockSpec((TC, d), lambda i: (i, 0)),
            pl.BlockSpec((1, TOP_K, TC), lambda i: (i, 0, 0)),
            pl.BlockSpec((1, TOP_K, TC), lambda i: (i, 0, 0)),
            pl.BlockSpec((1, N_EXPERTS, LANE), lambda i: (i, 0, 0)),
            pl.BlockSpec((TC, d), lambda i: (i, 0)),
        ],
        out_shape=[
            jax.ShapeDtypeStruct((n_tok, d), BF16),
            jax.ShapeDtypeStruct((n_chunks, TOP_K, TC), jnp.int32),
            jax.ShapeDtypeStruct((n_chunks, TOP_K, TC), F32),
            jax.ShapeDtypeStruct((n_chunks, N_EXPERTS, LANE), jnp.int32),
            jax.ShapeDtypeStruct((n_tok, d), F32),
        ],
        compiler_params=arb1,
        name="moe_route",
    )(xp, xs, modp, mods_g, g_ffn, w_r_t, rb, tri, wsg, wsu, wsd)
    counts = cnt[:, :, 0]
    pos = pos.reshape(n_chunks, 1, TOP_K * TC)
    wts = wts.reshape(n_chunks, 1, TOP_K * TC)

    te, nt, ebase, cbase, etp, slot0, slot1, sel = pl.pallas_call(
        functools.partial(_plan_kernel, n_chunks=n_chunks, nt_max=nt_max),
        in_specs=[_smem_spec()],
        out_specs=[_smem_spec()] * 8,
        out_shape=[
            jax.ShapeDtypeStruct((nt_max,), jnp.int32),
            jax.ShapeDtypeStruct((1,), jnp.int32),
            jax.ShapeDtypeStruct((N_EXPERTS,), jnp.int32),
            jax.ShapeDtypeStruct((n_chunks * N_EXPERTS,), jnp.int32),
            jax.ShapeDtypeStruct((N_EXPERTS,), jnp.int32),
            jax.ShapeDtypeStruct((nt_max // 2,), jnp.int32),
            jax.ShapeDtypeStruct((nt_max // 2,), jnp.int32),
            jax.ShapeDtypeStruct((nt_max,), jnp.int32),
        ],
        name="moe_plan",
    )(counts)

    xsort = pl.pallas_call(
        functools.partial(_dispatch_kernel, n_chunks=n_chunks, nt_max=nt_max),
        grid=(n_chunks,),
        in_specs=[_smem_spec(), _smem_spec(), _smem_spec(), _smem_spec(), _smem_spec(),
                  pl.BlockSpec((1, 1, TOP_K * TC), lambda i: (i, 0, 0), memory_space=pltpu.SMEM),
                  pl.BlockSpec((TC, d), lambda i: (i, 0))],
        out_specs=pl.BlockSpec(memory_space=pl.ANY),
        out_shape=jax.ShapeDtypeStruct((sorted_rows, d), BF16),
        scratch_shapes=[pltpu.VMEM((PLANES * HP_STRIDE, LANE), F32),
                        pltpu.VMEM((PLANES * LOC_STRIDE, LANE), F32),
                        pltpu.VMEM((LOCAL_ROWS, d), BF16),
                        pltpu.VMEM((PAGE, d), BF16),
                        pltpu.VMEM((TILE_M, d), BF16),
                        pltpu.SMEM((2,), jnp.int32),
                        pltpu.SemaphoreType.DMA, pltpu.SemaphoreType.DMA, pltpu.SemaphoreType.DMA],
        compiler_params=arb1,
        name="moe_dispatch",
    )(counts, ebase, cbase, etp, nt, pos, h2)

    tile_map = lambda i, s0, s1, sl, nt_ref: (jnp.minimum(i, (nt_ref[0] - 1) // 2), 0)
    slot_maps = (lambda i, s0, s1, sl, nt_ref: (s0[i], 0, 0), lambda i, s0, s1, sl, nt_ref: (s1[i], 0, 0))
    w_specs = [pl.BlockSpec(shape, slot_maps[slot]) for slot in range(2)
               for shape in ((1, d, D_EXPERT), (1, d, D_EXPERT), (1, D_EXPERT, d))]
    eo = pl.pallas_call(
        _expert_kernel,
        grid_spec=pltpu.PrefetchScalarGridSpec(
            num_scalar_prefetch=4,
            grid=(nt_max // 2,),
            in_specs=[pl.BlockSpec((2 * TILE_M, d), tile_map)] + w_specs,
            out_specs=pl.BlockSpec((2 * TILE_M, d), tile_map),
            scratch_shapes=[pltpu.VMEM((2, d, D_EXPERT), BF16), pltpu.VMEM((2, d, D_EXPERT), BF16),
                            pltpu.VMEM((2, D_EXPERT, d), BF16)],
        ),
        out_shape=jax.ShapeDtypeStruct((sorted_rows, d), BF16),
        input_output_aliases={4: 0},
        compiler_params=arb1,
        name="moe_experts",
    )(slot0, slot1, sel, nt, xsort, weg, weu, wed, weg, weu, wed)

    nph = COMBINE_PHASES
    pstep = lambda i, j: jnp.minimum(i * nph + j, npc * nph - 1)
    sstep = lambda i, j: jnp.maximum((i - npc) * nph + j, 0)
    smem_chunk = pl.BlockSpec((1, 1, TOP_K * TC), lambda i, j: (i, 0, 0), memory_space=pltpu.SMEM)
    yp, ys = pl.pallas_call(
        functools.partial(_combine_kernel, n_prompt_chunks=npc, n_chunks=n_chunks),
        grid=(n_chunks, nph),
        in_specs=[_smem_spec(), _smem_spec(), _smem_spec(), smem_chunk, smem_chunk,
                  pl.BlockSpec((gpp, r, d), lambda i, j: (pstep(i, j), 0, 0)),
                  pl.BlockSpec((gpp, r, d), lambda i, j: (sstep(i, j), 0, 0)),
                  pl.BlockSpec((1, 1, 6 * d), lambda i, j: (pclamp(i) // cpb, 0, 0)),
                  pl.BlockSpec((gpp, 1, 6 * d), lambda i, j: (sstep(i, j), 0, 0)),
                  pl.BlockSpec((PHASE_ROWS, d), lambda i, j: (i * nph + j, 0)),
                  cst((1, d)),
                  pl.BlockSpec(memory_space=pl.ANY)],
        out_specs=[pl.BlockSpec((gpp, r, d), lambda i, j: (pstep(i, j), 0, 0)),
                   pl.BlockSpec((gpp, r, d), lambda i, j: (sstep(i, j), 0, 0))],
        out_shape=[jax.ShapeDtypeStruct(xp.shape, F32), jax.ShapeDtypeStruct(xs.shape, F32)],
        scratch_shapes=[pltpu.VMEM((PLANES * LOC_STRIDE, LANE), F32),
                        pltpu.VMEM((2, LOCAL_ROWS, d), BF16),
                        pltpu.VMEM((PLANES * Y_STRIDE, LANE), F32),
                        pltpu.SMEM((2,), jnp.int32),
                        pltpu.SemaphoreType.DMA, pltpu.SemaphoreType.DMA],
        compiler_params=arb2,
        name="moe_combine",
    )(counts, ebase, cbase, pos, wts, xp, xs, modp, mods_g,
      shared, g_final, eo)
    return yp.reshape(b, s_len, d), ys.reshape(db, ds, d)


def _t5_bucket(rel):
    nb = NUM_BUCKETS // 2
    max_exact = nb // 2
    ret = jnp.where(rel > 0, nb, 0)
    n = jnp.abs(rel)
    nf = jnp.maximum(n, 1).astype(F32)
    large = max_exact + (jnp.log(nf / max_exact) / math.log(REL_MAX_DIST / max_exact)
                         * (nb - max_exact)).astype(jnp.int32)
    large = jnp.minimum(large, nb - 1)
    return ret + jnp.where(n < max_exact, n, large)


REL_SPAN = 2 * LANE


def _rel_bias_kernel(tab_ref, o_ref):
    tab = tab_ref[...]
    for i in range(CHUNK):
        shift = CHUNK - 1 - i
        o_ref[i] = pltpu.roll(tab, (REL_SPAN - shift) % REL_SPAN, 1)[:, 0:BAND]


def _rel_bias(rel_table):
    rel = jnp.arange(REL_SPAN) - (WINDOW + CHUNK - 1)
    tab = rel_table[_t5_bucket(rel)].astype(F32).T
    out = pl.pallas_call(
        _rel_bias_kernel,
        in_specs=[pl.BlockSpec((N_HEADS, REL_SPAN), lambda: (0, 0))],
        out_specs=pl.BlockSpec((CHUNK, N_HEADS, BAND), lambda: (0, 0, 0)),
        out_shape=jax.ShapeDtypeStruct((CHUNK, N_HEADS, BAND), F32),
        name="rel_bias",
    )(tab)
    return jnp.transpose(out, (1, 0, 2))


def kernel(x_prompt, x_sample, c_prompt, c_sample, cache_k, cache_v, state_pool, w_ada, b_ada, g_norm_mix, g_norm_ffn, w_in, w_pool_group, pool_scale, attn_sinks, w_pool_proj, w_attn_proj, w_out, rel_table, w_router, router_bias, w_exp_gate, w_exp_up, w_exp_down, w_sh_gate, w_sh_up, w_sh_down, g_final):
    b, s_len, d = x_prompt.shape
    db, ds, _ = x_sample.shape
    l = 0

    c_all = jnp.concatenate([c_prompt, c_sample, jnp.zeros((4, d), F32)], axis=0)
    mods = _ada(c_all, w_ada[l], b_ada[l])
    mod_p = mods[0:b].reshape(b, 1, 6 * d)
    mod_s = mods[b:b + db].reshape(db, 1, 6 * d)

    bias = _rel_bias(rel_table)
    pad_keys = PAIR_KEYS - BAND
    bt = jnp.transpose(bias.reshape(N_KV_HEADS, 2, 2, CHUNK, BAND), (0, 2, 1, 3, 4))
    lead = ((0, 0),) * 4
    bias4 = jnp.pad(bt.reshape(N_KV_HEADS, 2, 2 * CHUNK, BAND), lead[:3] + ((0, pad_keys),), constant_values=-1e30)
    bias_pw = jnp.stack([jnp.pad(bt, lead + ((0, pad_keys),), constant_values=-1e30),
                         jnp.pad(bt, lead + ((pad_keys, 0),), constant_values=-1e30)], axis=3)
    bias_pw = jnp.swapaxes(bias_pw.reshape(N_KV_HEADS, 2, 4 * CHUNK, PAIR_KEYS), 2, 3)
    skt = jnp.transpose(attn_sinks[l].astype(F32).reshape(N_KV_HEADS, 2, 2), (0, 2, 1))
    sink4 = jnp.broadcast_to(skt[:, :, :, None, None], (N_KV_HEADS, 2, 2, CHUNK, LANE)).reshape(
        N_KV_HEADS, 2, 2 * CHUNK, LANE)
    sink_pw = jnp.repeat(skt.reshape(-1), 2 * CHUNK).reshape(1, -1)
    blk = jnp.arange(POOL_WIDTH) // POOL_GW
    w_pg = jnp.where(blk[:, None] == blk[None, :],
                     jnp.tile(w_pool_group[l].reshape(POOL_WIDTH, POOL_GW), (1, len(POOL_WINDOWS))), 0.0).astype(BF16)
    w_in_b = w_in[l].astype(BF16)
    w_pp = w_pool_proj[l].astype(BF16)
    w_ap = w_attn_proj[l].astype(BF16)
    w_out_b = w_out[l].astype(BF16)
    g_mix = g_norm_mix[l].reshape(1, d)
    g_ffn = g_norm_ffn[l].reshape(1, d)
    ps = pool_scale[l].reshape(1, POOL_WIDTH)

    x1p, nk_p, nv_p, np_p = _mixer_prompt(x_prompt, mod_p, g_mix, w_in_b, w_pg, ps, bias_pw, sink_pw,
                                          w_pp, w_ap, w_out_b)
    ck = cache_k[l].reshape(db, WINDOW, KV_WIDTH)
    cv = cache_v[l].reshape(db, WINDOW, KV_WIDTH)
    sp16 = jnp.pad(state_pool[l], ((0, 0), (HIST - POOL_PAD, 0), (0, 0)))
    x1s, nk_s, nv_s, np_s = _mixer_sample(x_sample, mod_s, ck, cv, sp16, g_mix, w_in_b, w_pg, ps,
                                          bias4, sink4, w_pp, w_ap, w_out_b)

    y_p, y_s = _moe_sparse(x1p, x1s, mod_p, mod_s, g_ffn, g_final.reshape(1, d),
                           w_router[l].T.astype(BF16), router_bias[l].astype(F32).reshape(N_EXPERTS, 1),
                           w_sh_gate[l].astype(BF16), w_sh_up[l].astype(BF16), w_sh_down[l].astype(BF16),
                           w_exp_gate[l], w_exp_up[l], w_exp_down[l])

    kv_shape = lambda n: (1, n, WINDOW, N_KV_HEADS, HEAD_DIM)
    return (y_p, y_s,
            nk_p.reshape(kv_shape(b)), nv_p.reshape(kv_shape(b)),
            np_p[:, HIST - POOL_PAD:, :][None],
            nk_s.reshape(kv_shape(db)), nv_s.reshape(kv_shape(db)),
            np_s[:, HIST - POOL_PAD:, :][None])
```

```python
import functools
import math

import jax
import jax.numpy as jnp
from jax import lax
from jax.experimental import pallas as pl
from jax.experimental.pallas import tpu as pltpu

F32 = jnp.float32
BF16 = jnp.bfloat16

D_MODEL = 1024
CHUNK = 64
EPS = 1e-6
POOL_WIDTH = 512
POOL_WINDOWS = (2, 4, 8, 16)
POOL_GW = 128
POOL_PAD = 15
HEAD_DIM = 64
N_HEADS = 8
N_KV_HEADS = 2
ATTN_WIDTH = 512
KV_WIDTH = 128
WINDOW = 128
BAND = WINDOW + CHUNK
PAIR_KEYS = BAND + CHUNK
NUM_BUCKETS = 32
REL_MAX_DIST = 128
PAST_LEN = 4096
N_EXPERTS = 32
TOP_K = 4
N_GROUPS = 4
TOPK_GROUPS = 2
EXPERTS_PER_GROUP = 8
D_EXPERT = 256
ROUTED_SCALE = 2.5

OFF_U = 0
OFF_Q = OFF_U + POOL_WIDTH
OFF_K = OFF_Q + ATTN_WIDTH
OFF_V = OFF_K + KV_WIDTH
OFF_GP = OFF_V + KV_WIDTH
OFF_GA = OFF_GP + D_MODEL
IN_WIDTH = OFF_GA + D_MODEL

HIST = 16
MIX_ROWS = 512
MERGE_PANEL = 256
VMEM_LIMIT = 56 * 1024 * 1024

GROUP_ROWS = 64
TC = 1024
PAGE = 16
PAGE_SHIFT = 4
TILE_M = 512
PAGES_PER_TILE = TILE_M // PAGE
TILE_SHIFT = 5
LOCAL_ROWS = TOP_K * TC + N_EXPERTS * PAGE
LANE = 128
PLANES = D_MODEL // LANE
PLANE_PAD = 8
HP_STRIDE = TC + PLANE_PAD
LOC_STRIDE = LOCAL_ROWS + PLANE_PAD
COMBINE_PHASES = 4
PHASE_ROWS = TC // COMBINE_PHASES
RUN_COPY = 4
RUN_COPY_SHIFT = 2
Y_STRIDE = PHASE_ROWS + PLANE_PAD


def _dot(a, b):
    return jnp.dot(a, b, preferred_element_type=F32)


def _dot_nt(a, b):
    return lax.dot_general(a, b, (((1,), (1,)), ((), ())), preferred_element_type=F32)


def _norm_mod(x, g, shift, scale):
    ms = jnp.mean(x * x, axis=-1, keepdims=True)
    y = x * lax.rsqrt(ms + EPS) * g
    return y * (1.0 + scale) + shift


def _silu(x):
    return x * jax.nn.sigmoid(x)


def _ada_kernel(c_ref, w_ref, b_ref, o_ref):
    a = _silu(c_ref[...]).astype(BF16)
    o_ref[...] = _dot(a, w_ref[...].astype(BF16)) + b_ref[...]


def _ada(c_all, w_ada, b_ada):
    rows = c_all.shape[0]
    n = w_ada.shape[1]
    tn = 768
    return pl.pallas_call(
        _ada_kernel,
        grid=(n // tn,),
        in_specs=[
            pl.BlockSpec((rows, D_MODEL), lambda j: (0, 0)),
            pl.BlockSpec((D_MODEL, tn), lambda j: (0, j)),
            pl.BlockSpec((1, tn), lambda j: (0, j)),
        ],
        out_specs=pl.BlockSpec((rows, tn), lambda j: (0, j)),
        out_shape=jax.ShapeDtypeStruct((rows, n), F32),
        compiler_params=pltpu.CompilerParams(dimension_semantics=("arbitrary",)),
        name="ada",
    )(c_all, w_ada, b_ada.reshape(1, n))


def _pool_feats(ue, pos0, rows):
    s2 = ue + pltpu.roll(ue, 1, 0)
    s4 = s2 + pltpu.roll(s2, 2, 0)
    s8 = s4 + pltpu.roll(s4, 4, 0)
    s16 = s8 + pltpu.roll(s8, 8, 0)
    pos = pos0 + lax.broadcasted_iota(jnp.int32, (rows, POOL_GW), 0)
    outs = []
    for g, (w, sw) in enumerate(zip(POOL_WINDOWS, (s2, s4, s8, s16))):
        sl = slice(g * POOL_GW, (g + 1) * POOL_GW)
        cnt = jnp.minimum(pos + 1, w).astype(F32)
        outs.append(sw[HIST:, sl] / cnt - ue[HIST:, sl])
    return jnp.concatenate(outs, axis=1)


def _kv_variants(t):
    lane = lax.broadcasted_iota(jnp.int32, t.shape, 1)
    low = lane < HEAD_DIM
    swapped = pltpu.roll(t, HEAD_DIM, 1)
    zero = jnp.zeros_like(t)
    return (jnp.where(low, t, zero), jnp.where(low, zero, swapped),
            jnp.where(low, swapped, zero), jnp.where(low, zero, t))


def _merge_out(h, pool_y, attn_y, win_ref, wpp_ref, wap_ref, wout_ref, m_scr):
    rows = h.shape[0]
    pool_b = pool_y.astype(BF16)
    attn_b = attn_y.astype(BF16)
    for n in range(D_MODEL // MERGE_PANEL):
        lo, hi = n * MERGE_PANEL, (n + 1) * MERGE_PANEL
        gp = _dot(h, win_ref[:, OFF_GP + lo:OFF_GP + hi])
        ga = _dot(h, win_ref[:, OFF_GA + lo:OFF_GA + hi])
        pp = _dot(pool_b, wpp_ref[:, lo:hi])
        ap = _dot(attn_b, wap_ref[:, lo:hi])
        m_scr[0:rows, lo:hi] = (jax.nn.sigmoid(gp) * pp + jax.nn.sigmoid(ga) * ap).astype(BF16)
    return _dot(m_scr[0:rows, :], wout_ref[...])


def _mixer_prompt_kernel(x_ref, mod_ref, g_ref, win_ref, wpg_ref, ps_ref, bias_ref, sink_ref,
                         wpp_ref, wap_ref, wout_ref,
                         x1_ref, nk_ref, nv_ref, npool_ref,
                         uext, k0, k1, k2, k3, v0, v1, v2, v3, m_scr, s_scr, p_scr, a_scr):
    ts = MIX_ROWS
    s = pl.program_id(1)
    kext = (k0, k1, k2, k3)
    vext = (v0, v1, v2, v3)

    @pl.when(s == 0)
    def _():
        uext[0:HIST, :] = jnp.zeros((HIST, POOL_WIDTH), F32)
        for r in kext:
            r[0:WINDOW, :] = jnp.zeros((WINDOW, KV_WIDTH), BF16)
        for r in vext:
            r[:, 0:WINDOW] = jnp.zeros((KV_WIDTH, WINDOW), BF16)

    x = x_ref[0]
    shift = mod_ref[0, :, 0:D_MODEL]
    scale = mod_ref[0, :, D_MODEL:2 * D_MODEL]
    gate = mod_ref[0, :, 2 * D_MODEL:3 * D_MODEL]
    h = _norm_mod(x, g_ref[...], shift, scale).astype(BF16)

    u = _dot(h, win_ref[:, OFF_U:OFF_U + POOL_WIDTH])
    q = (_dot(h, win_ref[:, OFF_Q:OFF_Q + ATTN_WIDTH]) * (HEAD_DIM ** -0.5)).astype(BF16)
    kv = _dot(h, win_ref[:, OFF_K:OFF_V + KV_WIDTH])
    k = kv[:, 0:KV_WIDTH]
    v = kv[:, KV_WIDTH:2 * KV_WIDTH]
    nk_ref[0] = k[ts - WINDOW:ts, :]
    nv_ref[0] = v[ts - WINDOW:ts, :]
    npool_ref[0] = u[ts - HIST:ts, :]

    pos0 = s * ts
    uext[HIST:HIST + ts, :] = u
    p = _pool_feats(uext[...], pos0, ts)
    uext[0:HIST, :] = u[ts - HIST:ts, :]
    pool_y = _dot(p.astype(BF16), wpg_ref[...]) * ps_ref[...]

    for r, t in zip(kext, _kv_variants(k)):
        r[WINDOW:WINDOW + ts, :] = t.astype(BF16)
    for r, t in zip(vext, _kv_variants(v)):
        r[:, WINDOW:WINDOW + ts] = t.T.astype(BF16)
    pair_rows = 2 * CHUNK
    blk_q = 2 * pair_rows
    n_pairs = ts // pair_rows
    per_pair = N_KV_HEADS * 2
    key_i = lax.broadcasted_iota(jnp.int32, (PAIR_KEYS, blk_q), 0)
    for pr in range(n_pairs):
        lo = pr * pair_rows
        valid = (pos0 + lo + key_i) >= WINDOW
        for g in range(N_KV_HEADS):
            qs = jnp.concatenate([q[lo:lo + pair_rows, (2 * g) * LANE:(2 * g + 1) * LANE],
                                  q[lo:lo + pair_rows, (2 * g + 1) * LANE:(2 * g + 2) * LANE]], axis=0)
            for par in range(2):
                s = _dot_nt(kext[2 * g + par][lo:lo + PAIR_KEYS, :], qs) + bias_ref[g, par]
                s_scr[(pr * N_KV_HEADS + g) * 2 + par] = jnp.where(valid, s, -1e30)
    sink = sink_ref[...]
    for pr in range(n_pairs):
        s = jnp.concatenate([s_scr[pr * per_pair + b] for b in range(per_pair)], axis=1)
        m = jnp.maximum(jnp.max(s, axis=0, keepdims=True), sink)
        e = jnp.exp(s - m)
        denom = jnp.sum(e, axis=0, keepdims=True) + jnp.exp(sink - m)
        p = (e / denom).astype(BF16)
        for b in range(per_pair):
            p_scr[pr * per_pair + b] = p[:, b * blk_q:(b + 1) * blk_q]
    for pr in range(n_pairs):
        lo = pr * pair_rows
        for g in range(N_KV_HEADS):
            blk = (pr * N_KV_HEADS + g) * 2
            ot = (_dot(vext[2 * g][:, lo:lo + PAIR_KEYS], p_scr[blk])
                  + _dot(vext[2 * g + 1][:, lo:lo + PAIR_KEYS], p_scr[blk + 1]))
            o = ot.T
            a_scr[lo:lo + pair_rows, (2 * g) * LANE:(2 * g + 1) * LANE] = o[0:pair_rows].astype(BF16)
            a_scr[lo:lo + pair_rows, (2 * g + 1) * LANE:(2 * g + 2) * LANE] = o[pair_rows:].astype(BF16)
    for r in kext:
        r[0:WINDOW, :] = r[ts:ts + WINDOW, :]
    for r in vext:
        r[:, 0:WINDOW] = r[:, ts:ts + WINDOW]

    y = _merge_out(h, pool_y, a_scr[...], win_ref, wpp_ref, wap_ref, wout_ref, m_scr)
    x1_ref[0] = x + gate * y


def _const_spec(shape):
    nd = len(shape)
    return pl.BlockSpec(shape, lambda *_: (0,) * nd)


def _mixer_prompt(x, mod, g_mix, w_in, w_pg, pool_scale, bias4, sink4, w_pp, w_ap, w_out):
    b, s_len, d = x.shape
    ts = MIX_ROWS
    pair_rows = 2 * CHUNK
    n_blocks = (ts // pair_rows) * N_KV_HEADS * 2
    kv_scratch = ([pltpu.VMEM((WINDOW + ts, KV_WIDTH), BF16) for _ in range(4)]
                  + [pltpu.VMEM((KV_WIDTH, WINDOW + ts), BF16) for _ in range(4)])
    return pl.pallas_call(
        _mixer_prompt_kernel,
        grid=(b, s_len // ts),
        in_specs=[
            pl.BlockSpec((1, ts, d), lambda i, j: (i, j, 0)),
            pl.BlockSpec((1, 1, 6 * d), lambda i, j: (i, 0, 0)),
            _const_spec((1, d)),
            _const_spec((d, IN_WIDTH)),
            _const_spec((POOL_WIDTH, POOL_WIDTH)),
            _const_spec((1, POOL_WIDTH)),
            _const_spec(bias4.shape),
            _const_spec(sink4.shape),
            _const_spec((POOL_WIDTH, d)),
            _const_spec((ATTN_WIDTH, d)),
            _const_spec((d, d)),
        ],
        out_specs=[
            pl.BlockSpec((1, ts, d), lambda i, j: (i, j, 0)),
            pl.BlockSpec((1, WINDOW, KV_WIDTH), lambda i, j: (i, 0, 0)),
            pl.BlockSpec((1, WINDOW, KV_WIDTH), lambda i, j: (i, 0, 0)),
            pl.BlockSpec((1, HIST, POOL_WIDTH), lambda i, j: (i, 0, 0)),
        ],
        out_shape=[
            jax.ShapeDtypeStruct((b, s_len, d), F32),
            jax.ShapeDtypeStruct((b, WINDOW, KV_WIDTH), F32),
            jax.ShapeDtypeStruct((b, WINDOW, KV_WIDTH), F32),
            jax.ShapeDtypeStruct((b, HIST, POOL_WIDTH), F32),
        ],
        scratch_shapes=[pltpu.VMEM((HIST + ts, POOL_WIDTH), F32)] + kv_scratch
                       + [pltpu.VMEM((ts, d), BF16),
                          pltpu.VMEM((n_blocks, 2 * pair_rows, PAIR_KEYS), F32),
                          pltpu.VMEM((n_blocks, 2 * pair_rows, PAIR_KEYS), BF16),
                          pltpu.VMEM((ts, ATTN_WIDTH), BF16)],
        compiler_params=pltpu.CompilerParams(
            dimension_semantics=("arbitrary", "arbitrary"), vmem_limit_bytes=VMEM_LIMIT),
        name="mixer_prompt",
    )(x, mod, g_mix, w_in, w_pg, pool_scale, bias4, sink4, w_pp, w_ap, w_out)


def _mixer_sample_kernel(x_ref, mod_ref, ck_ref, cv_ref, sp_ref, g_ref, win_ref, wpg_ref, ps_ref,
                         bias_ref, sink_ref, wpp_ref, wap_ref, wout_ref,
                         x1_ref, nk_ref, nv_ref, npool_ref, m_scr, ks, vs, s_scr, p_scr, a_scr):
    nseq, ds, d = x_ref.shape
    rows = nseq * ds
    x = x_ref[...]
    shift = mod_ref[:, :, 0:D_MODEL]
    scale = mod_ref[:, :, D_MODEL:2 * D_MODEL]
    gate = mod_ref[:, :, 2 * D_MODEL:3 * D_MODEL]
    h = _norm_mod(x, g_ref[...], shift, scale).reshape(rows, d).astype(BF16)

    u = _dot(h, win_ref[:, OFF_U:OFF_U + POOL_WIDTH])
    q = (_dot(h, win_ref[:, OFF_Q:OFF_Q + ATTN_WIDTH]) * (HEAD_DIM ** -0.5)).astype(BF16)
    kv = _dot(h, win_ref[:, OFF_K:OFF_V + KV_WIDTH])
    k = kv[:, 0:KV_WIDTH]
    v = kv[:, KV_WIDTH:2 * KV_WIDTH]

    pool_parts = []
    key_pad = jnp.zeros((PAIR_KEYS - BAND, KV_WIDTH), F32)
    for b in range(nseq):
        lo = b * ds
        ub = u[lo:lo + ds, :]
        kb = k[lo:lo + ds, :]
        vb = v[lo:lo + ds, :]
        ck = ck_ref[b]
        cv = cv_ref[b]
        nk_ref[b] = jnp.concatenate([ck[ds:WINDOW, :], kb], axis=0)
        nv_ref[b] = jnp.concatenate([cv[ds:WINDOW, :], vb], axis=0)
        npool_ref[b] = ub[ds - HIST:ds, :]
        ue = jnp.concatenate([sp_ref[b], ub], axis=0)
        pool_parts.append(_pool_feats(ue, PAST_LEN, ds))
        for var, t in enumerate(_kv_variants(jnp.concatenate([ck, kb, key_pad], axis=0))):
            ks[var, b] = t.astype(BF16)
        for var, t in enumerate(_kv_variants(jnp.concatenate([cv, vb, key_pad], axis=0))):
            vs[var, b] = t.astype(BF16)
    p = jnp.concatenate(pool_parts, axis=0)
    pool_y = _dot(p.astype(BF16), wpg_ref[...]) * ps_ref[...]

    per_seq = N_KV_HEADS * 2
    for b in range(nseq):
        lo = b * ds
        for g in range(N_KV_HEADS):
            qs = jnp.concatenate([q[lo:lo + ds, (2 * g) * LANE:(2 * g + 1) * LANE],
                                  q[lo:lo + ds, (2 * g + 1) * LANE:(2 * g + 2) * LANE]], axis=0)
            for par in range(2):
                s_scr[b * per_seq + 2 * g + par] = _dot_nt(qs, ks[2 * g + par, b]) + bias_ref[g, par]
    group = 2
    grp_rows = group * per_seq * 2 * ds
    sink1 = sink_ref[...].reshape(per_seq * 2 * ds, LANE)[:, 0:1]
    sink = jnp.concatenate([sink1] * group, axis=0)
    for gi in range(nseq // group):
        blks = slice(gi * group * per_seq, (gi + 1) * group * per_seq)
        s = s_scr[blks].reshape(grp_rows, PAIR_KEYS)
        m = jnp.maximum(jnp.max(s, axis=-1, keepdims=True), sink)
        e = jnp.exp(s - m)
        esum = _dot(e.astype(BF16), jnp.ones((PAIR_KEYS, LANE), BF16))
        denom = esum + jnp.exp(sink - m)
        p_scr[blks] = ((e / jnp.concatenate([denom, denom], axis=1)).astype(BF16)
                       .reshape(group * per_seq, 2 * ds, PAIR_KEYS))
    for b in range(nseq):
        lo = b * ds
        for g in range(N_KV_HEADS):
            blk = b * per_seq + 2 * g
            o = _dot(p_scr[blk], vs[2 * g, b]) + _dot(p_scr[blk + 1], vs[2 * g + 1, b])
            a_scr[lo:lo + ds, (2 * g) * LANE:(2 * g + 1) * LANE] = o[0:ds].astype(BF16)
            a_scr[lo:lo + ds, (2 * g + 1) * LANE:(2 * g + 2) * LANE] = o[ds:].astype(BF16)

    y = _merge_out(h, pool_y, a_scr[...], win_ref, wpp_ref, wap_ref, wout_ref, m_scr)
    x1_ref[...] = x + gate * y.reshape(nseq, ds, d)


def _mixer_sample(x, mod, ck, cv, sp16, g_mix, w_in, w_pg, pool_scale, bias4, sink4, w_pp, w_ap, w_out):
    db, ds, d = x.shape
    nseq = MIX_ROWS // ds
    seq_spec = lambda shape: pl.BlockSpec((nseq,) + shape, lambda i: (i, 0, 0))
    return pl.pallas_call(
        _mixer_sample_kernel,
        grid=(db // nseq,),
        in_specs=[
            seq_spec((ds, d)),
            seq_spec((1, 6 * d)),
            seq_spec((WINDOW, KV_WIDTH)),
            seq_spec((WINDOW, KV_WIDTH)),
            seq_spec((HIST, POOL_WIDTH)),
            _const_spec((1, d)),
            _const_spec((d, IN_WIDTH)),
            _const_spec((POOL_WIDTH, POOL_WIDTH)),
            _const_spec((1, POOL_WIDTH)),
            _const_spec(bias4.shape),
            _const_spec(sink4.shape),
            _const_spec((POOL_WIDTH, d)),
            _const_spec((ATTN_WIDTH, d)),
            _const_spec((d, d)),
        ],
        out_specs=[
            seq_spec((ds, d)),
            seq_spec((WINDOW, KV_WIDTH)),
            seq_spec((WINDOW, KV_WIDTH)),
            seq_spec((HIST, POOL_WIDTH)),
        ],
        out_shape=[
            jax.ShapeDtypeStruct((db, ds, d), F32),
            jax.ShapeDtypeStruct((db, WINDOW, KV_WIDTH), F32),
            jax.ShapeDtypeStruct((db, WINDOW, KV_WIDTH), F32),
            jax.ShapeDtypeStruct((db, HIST, POOL_WIDTH), F32),
        ],
        scratch_shapes=[pltpu.VMEM((MIX_ROWS, d), BF16),
                        pltpu.VMEM((4, nseq, PAIR_KEYS, KV_WIDTH), BF16),
                        pltpu.VMEM((4, nseq, PAIR_KEYS, KV_WIDTH), BF16),
                        pltpu.VMEM((nseq * N_KV_HEADS * 2, 2 * ds, PAIR_KEYS), F32),
                        pltpu.VMEM((nseq * N_KV_HEADS * 2, 2 * ds, PAIR_KEYS), BF16),
                        pltpu.VMEM((MIX_ROWS, ATTN_WIDTH), BF16)],
        compiler_params=pltpu.CompilerParams(
            dimension_semantics=("arbitrary",), vmem_limit_bytes=VMEM_LIMIT),
        name="mixer_sample",
    )(x, mod, ck, cv, sp16, g_mix, w_in, w_pg, pool_scale, bias4, sink4, w_pp, w_ap, w_out)


def _first_argmax(vals, idx, size):
    m = jnp.max(vals, axis=0, keepdims=True)
    first = jnp.min(jnp.where(vals == m, idx, size), axis=0, keepdims=True)
    return m, first


def _router_picks(h, wr_ref, rb_ref):
    t = h.shape[0]
    logits = _dot_nt(wr_ref[...], h)
    scores = jax.nn.sigmoid(logits)
    sel = scores + rb_ref[...]
    neg = jnp.float32(-jnp.inf)
    eidx = lax.broadcasted_iota(jnp.int32, (N_EXPERTS, t), 0)

    gscores = []
    bidx = lax.broadcasted_iota(jnp.int32, (EXPERTS_PER_GROUP, t), 0)
    for g in range(N_GROUPS):
        blk = sel[g * EXPERTS_PER_GROUP:(g + 1) * EXPERTS_PER_GROUP, :]
        m1, i1 = _first_argmax(blk, bidx, EXPERTS_PER_GROUP)
        m2 = jnp.max(jnp.where(bidx == i1, neg, blk), axis=0, keepdims=True)
        gscores.append(m1 + m2)
    gs = jnp.concatenate(gscores, axis=0)
    gidx = lax.broadcasted_iota(jnp.int32, (N_GROUPS, t), 0)
    assert TOPK_GROUPS == 2
    _, g1 = _first_argmax(gs, gidx, N_GROUPS)
    _, g2 = _first_argmax(jnp.where(gidx == g1, neg, gs), gidx, N_GROUPS)
    egrp = eidx // EXPERTS_PER_GROUP
    cand = jnp.where((egrp == g1) | (egrp == g2), sel, neg)

    picks = []
    for _ in range(TOP_K):
        _, i = _first_argmax(cand, eidx, N_EXPERTS)
        hit = eidx == i
        picks.append(hit)
        cand = jnp.where(hit, neg, cand)
    return picks, scores


def _swiglu_rows(h, wg, wu, wd):
    act = _silu(_dot(h, wg)) * _dot(h, wu)
    return _dot(act.astype(BF16), wd)


def _route_kernel(xp_ref, xs_ref, modp_ref, mods_ref, g_ref, wr_ref, rb_ref, tri_ref,
                  wsg_ref, wsu_ref, wsd_ref, h_ref, pos_ref, w_ref, cnt_ref, sh_ref, *, n_prompt_chunks):
    is_prompt = pl.program_id(0) < n_prompt_chunks
    x = jnp.where(is_prompt, xp_ref[...], xs_ref[...])
    sh, sc = slice(3 * D_MODEL, 4 * D_MODEL), slice(4 * D_MODEL, 5 * D_MODEL)
    shift = jnp.where(is_prompt, modp_ref[:, :, sh], mods_ref[:, :, sh])
    scale = jnp.where(is_prompt, modp_ref[:, :, sc], mods_ref[:, :, sc])
    h = _norm_mod(x, g_ref[...], shift, scale).reshape(TC, D_MODEL).astype(BF16)
    h_ref[...] = h
    sh_ref[...] = _swiglu_rows(h, wsg_ref[...], wsu_ref[...], wsd_ref[...])
    picks, scores = _router_picks(h, wr_ref, rb_ref)
    picked = picks[0] | picks[1] | picks[2] | picks[3]
    cnt = _dot(jnp.where(picked, 1.0, 0.0).astype(BF16), tri_ref[...])
    n_e = jnp.max(cnt, axis=1, keepdims=True).astype(jnp.int32)
    n_b = jnp.broadcast_to(n_e, (N_EXPERTS, LANE))
    cnt_ref[0] = n_b
    padded = lax.shift_left(lax.shift_right_logical(n_b + (PAGE - 1), PAGE_SHIFT), PAGE_SHIFT)
    row = lax.broadcasted_iota(jnp.int32, (N_EXPERTS, LANE), 0)
    scan = padded
    for s in (1, 2, 4, 8, 16):
        scan = scan + jnp.where(row >= s, pltpu.roll(scan, s, 0), 0)
    base = (scan - padded)[:, 0:1].astype(F32)
    slot = base + cnt - 1.0
    pos = [jnp.sum(jnp.where(p, slot, 0.0), axis=0, keepdims=True) for p in picks]
    wk = [jnp.sum(jnp.where(p, scores, 0.0), axis=0, keepdims=True) for p in picks]
    wsum = wk[0] + wk[1] + wk[2] + wk[3]
    pos_ref[0] = jnp.concatenate(pos, axis=0).astype(jnp.int32)
    w_ref[0] = jnp.concatenate(wk, axis=0) / wsum * ROUTED_SCALE


def _plan_kernel(cnt_ref, te_ref, nt_ref, ebase_ref, cbase_ref, etp_ref, slot0_ref, slot1_ref, sel_ref,
                 *, n_chunks, nt_max):
    def per_expert(e, tile0):
        def per_chunk(c, acc):
            cbase_ref[c * N_EXPERTS + e] = acc
            return acc + lax.shift_right_logical(cnt_ref[c, e] + (PAGE - 1), PAGE_SHIFT)
        tp = lax.fori_loop(0, n_chunks, per_chunk, 0)
        etp_ref[e] = tp
        ebase_ref[e] = tile0 * PAGES_PER_TILE
        nt = lax.shift_right_logical(tp + (PAGES_PER_TILE - 1), TILE_SHIFT)

        def fill(j, carry):
            te_ref[tile0 + j] = e
            return carry
        lax.fori_loop(0, nt, fill, 0)
        return tile0 + nt
    total = lax.fori_loop(0, N_EXPERTS, per_expert, 0)
    nt_ref[0] = total

    def tail(j, carry):
        te_ref[j] = N_EXPERTS - 1
        return carry
    lax.fori_loop(total, nt_max, tail, 0)

    def per_step(i, carry):
        c0, c1 = carry
        used = 2 * i < total
        a = te_ref[jnp.minimum(2 * i, total - 1)]
        b = te_ref[jnp.minimum(2 * i + 1, total - 1)]
        hit_a = (a == c0) | (a == c1)
        evict0 = hit_a | (c0 != b)
        n0 = jnp.where(used & ~hit_a & evict0, a, c0)
        n1 = jnp.where(used & ~hit_a & ~evict0, a, c1)
        sa = jnp.where(a == n0, 0, 1)
        hit_b = (b == n0) | (b == n1)
        m0 = jnp.where(used & ~hit_b & (sa == 1), b, n0)
        m1 = jnp.where(used & ~hit_b & (sa == 0), b, n1)
        slot0_ref[i] = jnp.maximum(m0, 0)
        slot1_ref[i] = jnp.maximum(m1, 0)
        sel_ref[2 * i] = sa
        sel_ref[2 * i + 1] = jnp.where(b == m0, 0, 1)
        return (m0, m1)
    lax.fori_loop(0, nt_max // 2, per_step, (-1, -1))


def _page_copy(src, src_page, dst, dst_page, sem, pages=1):
    rows = pages * PAGE
    s = src.at[pl.ds(pl.multiple_of(src_page * PAGE, PAGE), rows), :]
    t = dst.at[pl.ds(pl.multiple_of(dst_page * PAGE, PAGE), rows), :]
    return pltpu.make_async_copy(s, t, sem)


def _for_each_run(c, cnt_ref, ebase_ref, cbase_ref, fn):
    def per_expert(e, carry):
        lpage, nbig, nsmall = carry
        n = cnt_ref[c, e]
        npages = lax.shift_right_logical(n + (PAGE - 1), PAGE_SHIFT)
        fn(n, lpage, ebase_ref[e] + cbase_ref[c * N_EXPERTS + e], npages)
        return (lpage + npages, nbig + lax.shift_right_logical(npages, RUN_COPY_SHIFT),
                nsmall + (npages & (RUN_COPY - 1)))
    return lax.fori_loop(0, N_EXPERTS, per_expert, (0, 0, 0))


def _start_run_copies(src, src_page, dst, dst_page, npages, sem_big, sem_small):
    nbig = lax.shift_right_logical(npages, RUN_COPY_SHIFT)

    def big(j, carry):
        _page_copy(src, src_page + j * RUN_COPY, dst, dst_page + j * RUN_COPY, sem_big, RUN_COPY).start(priority=0)
        return carry
    lax.fori_loop(0, nbig, big, 0)

    def small(j, carry):
        _page_copy(src, src_page + j, dst, dst_page + j, sem_small).start(priority=1)
        return carry
    lax.fori_loop(nbig * RUN_COPY, npages, small, 0)


def _wait_copies(src, dst, nbig, nsmall, sem_big, sem_small):
    def big(j, carry):
        _page_copy(src, 0, dst, 0, sem_big, RUN_COPY).wait()
        return carry
    lax.fori_loop(0, nbig, big, 0)

    def small(j, carry):
        _page_copy(src, 0, dst, 0, sem_small).wait()
        return carry
    lax.fori_loop(0, nsmall, small, 0)


def _row_slab(ref, row, stride):
    return ref.at[pl.ds(row, PLANES, stride=stride), :]


def _dispatch_kernel(cnt_ref, ebase_ref, cbase_ref, etp_ref, nt_ref, pos_ref, h_ref, xs_hbm,
                     hpl, loc, locb, zpage, ztile, pend, sem, bsem, tsem, *, n_chunks, nt_max):
    c = pl.program_id(0)

    @pl.when(c == 0)
    def _():
        pend[0] = 0
        pend[1] = 0
        loc[...] = jnp.zeros(loc.shape, F32)

    h = h_ref[...]
    for ch in range(PLANES):
        hpl[ch * HP_STRIDE:ch * HP_STRIDE + TC, :] = h[:, ch * LANE:(ch + 1) * LANE].astype(F32)
    zpage[...] = jnp.zeros(zpage.shape, BF16)
    ztile[...] = jnp.zeros(ztile.shape, BF16)

    def scatter(t8, carry):
        for u in range(8):
            t = t8 * 8 + u
            slab = _row_slab(hpl, t, HP_STRIDE)[...]
            for k in range(TOP_K):
                _row_slab(loc, pos_ref[0, 0, k * TC + t], LOC_STRIDE)[...] = slab
        return carry
    lax.fori_loop(0, TC // 8, scatter, 0)

    _wait_copies(locb, xs_hbm, pend[0], pend[1], bsem, sem)
    for ch in range(PLANES):
        locb[:, ch * LANE:(ch + 1) * LANE] = loc[ch * LOC_STRIDE:ch * LOC_STRIDE + LOCAL_ROWS, :].astype(BF16)

    def send_run(n, lpage, gpage, npages):
        _start_run_copies(locb, lpage, xs_hbm, gpage, npages, bsem, sem)
    _, n_big, n_small = _for_each_run(c, cnt_ref, ebase_ref, cbase_ref, send_run)
    pend[0] = n_big
    pend[1] = n_small

    @pl.when(c == n_chunks - 1)
    def _():
        def pad_expert(e, n):
            tp = etp_ref[e]
            full = lax.shift_left(lax.shift_right_logical(tp + (PAGES_PER_TILE - 1), TILE_SHIFT), TILE_SHIFT)

            def per_page(j, carry):
                _page_copy(zpage, 0, xs_hbm, ebase_ref[e] + j, sem).start()
                return carry
            lax.fori_loop(tp, full, per_page, 0)
            return n + full - tp
        n_pad = lax.fori_loop(0, N_EXPERTS, pad_expert, 0)

        def tile_copy(i):
            dst = xs_hbm.at[pl.ds(pl.multiple_of(i * TILE_M, TILE_M), TILE_M), :]
            return pltpu.make_async_copy(ztile, dst, tsem)

        def tail_start(i, carry):
            tile_copy(i).start()
            return carry
        lax.fori_loop(nt_ref[0], nt_max, tail_start, 0)
        _wait_copies(locb, xs_hbm, n_big, n_small + n_pad, bsem, sem)

        def tail_wait(i, carry):
            tile_copy(i).wait()
            return carry
        lax.fori_loop(nt_ref[0], nt_max, tail_wait, 0)


def _expert_kernel(slot0_ref, slot1_ref, sel_ref, nt_ref, xs_ref, wg0_ref, wu0_ref, wd0_ref,
                   wg1_ref, wu1_ref, wd1_ref, o_ref, wgb, wub, wdb):
    i = pl.program_id(0)
    slots = ((slot0_ref, wg0_ref, wu0_ref, wd0_ref), (slot1_ref, wg1_ref, wu1_ref, wd1_ref))

    @pl.when(2 * i < nt_ref[0])
    def _():
        for slot, (idx_ref, wg_ref, wu_ref, wd_ref) in enumerate(slots):
            @pl.when((i == 0) | (idx_ref[i] != idx_ref[jnp.maximum(i - 1, 0)]))
            def _():
                wgb[slot] = wg_ref[0].astype(BF16)
                wub[slot] = wu_ref[0].astype(BF16)
                wdb[slot] = wd_ref[0].astype(BF16)
        for t in range(2):
            rows = slice(t * TILE_M, (t + 1) * TILE_M)
            sl = sel_ref[2 * i + t]
            o_ref[rows, :] = _swiglu_rows(xs_ref[rows, :], wgb[sl], wub[sl], wdb[sl]).astype(BF16)


def _combine_kernel(cnt_ref, ebase_ref, cbase_ref, pos_ref, w_ref, xp_ref, xs_ref, gp_ref, gs_ref,
                    sh_ref, gf_ref, eo_hbm, yp_ref, ys_ref,
                    loc, locb, ypl, pend, sem, bsem, *, n_prompt_chunks, n_chunks):
    c = pl.program_id(0)
    ph = pl.program_id(1)

    def fetch(chunk):
        buf = locb.at[chunk % 2]

        def fetch_run(n, lpage, gpage, npages):
            _start_run_copies(eo_hbm, gpage, buf, lpage, npages, bsem, sem)
        _, n_big, n_small = _for_each_run(chunk, cnt_ref, ebase_ref, cbase_ref, fetch_run)
        pend[0] = n_big
        pend[1] = n_small

    @pl.when((c == 0) & (ph == 0))
    def _():
        locb[...] = jnp.zeros(locb.shape, BF16)
        fetch(c)

    @pl.when(ph == 0)
    def _():
        cur = locb.at[c % 2]
        _wait_copies(eo_hbm, cur, pend[0], pend[1], bsem, sem)

        for ch in range(PLANES):
            loc[ch * LOC_STRIDE:ch * LOC_STRIDE + LOCAL_ROWS, :] = cur[:, ch * LANE:(ch + 1) * LANE].astype(F32)

        @pl.when(c + 1 < n_chunks)
        def _():
            fetch(c + 1)

    shared = sh_ref[...]
    t0 = ph * PHASE_ROWS

    def gather(t8, carry):
        for u in range(8):
            t = t8 * 8 + u
            acc = jnp.zeros((PLANES, LANE), F32)
            for k in range(TOP_K):
                idx = k * TC + t0 + t
                acc = acc + w_ref[0, 0, idx] * _row_slab(loc, pos_ref[0, 0, idx], LOC_STRIDE)[...]
            _row_slab(ypl, t, Y_STRIDE)[...] = acc
        return carry
    lax.fori_loop(0, PHASE_ROWS // 8, gather, 0)
    routed = jnp.concatenate([ypl[ch * Y_STRIDE:ch * Y_STRIDE + PHASE_ROWS, :] for ch in range(PLANES)], axis=1)
    f = shared + routed

    def finish(x_ref, gate_ref, y_ref):
        xo = x_ref[...] + gate_ref[:, :, 5 * D_MODEL:6 * D_MODEL] * f.reshape(x_ref.shape)
        ms = jnp.mean(xo * xo, axis=-1, keepdims=True)
        y_ref[...] = xo * lax.rsqrt(ms + EPS) * gf_ref[...]

    @pl.when(c < n_prompt_chunks)
    def _():
        finish(xp_ref, gp_ref, yp_ref)

    @pl.when(c >= n_prompt_chunks)
    def _():
        finish(xs_ref, gs_ref, ys_ref)


def _smem_spec():
    return pl.BlockSpec(memory_space=pltpu.SMEM)


def _moe_sparse(x1p, x1s, modp, mods, g_ffn, g_final, w_r_t, rb, wsg, wsu, wsd, weg, weu, wed):
    b, s_len, d = x1p.shape
    db, ds, _ = x1s.shape
    r = GROUP_ROWS
    xp = x1p.reshape(b * s_len // r, r, d)
    xs = x1s.reshape(db * ds // r, r, d)
    mods_g = jnp.repeat(mods, ds // r, axis=0) if ds != r else mods
    npc = b * s_len // TC
    nsc = db * ds // TC
    n_chunks = npc + nsc
    n_tok = n_chunks * TC
    cpb = s_len // TC
    gpc = TC // r
    gpp = PHASE_ROWS // r
    max_pages = (TOP_K * n_tok) // PAGE + n_chunks * N_EXPERTS
    nt_max = -(-(max_pages // PAGES_PER_TILE + N_EXPERTS) // 8) * 8
    sorted_rows = nt_max * TILE_M
    arb1 = pltpu.CompilerParams(dimension_semantics=("arbitrary",), vmem_limit_bytes=VMEM_LIMIT)
    arb2 = pltpu.CompilerParams(dimension_semantics=("arbitrary", "arbitrary"), vmem_limit_bytes=VMEM_LIMIT)
    cst = lambda shape: pl.BlockSpec(shape, lambda *_: (0,) * len(shape))
    tri = jnp.triu(jnp.ones((TC, TC), BF16))
    pclamp = lambda i: jnp.minimum(i, npc - 1)
    sclamp = lambda i: jnp.maximum(i - npc, 0)

    h2, pos, wts, cnt, shared = pl.pallas_call(
        functools.partial(_route_kernel, n_prompt_chunks=npc),
        grid=(n_chunks,),
        in_specs=[
            pl.BlockSpec((gpc, r, d), lambda i: (pclamp(i), 0, 0)),
            pl.BlockSpec((gpc, r, d), lambda i: (sclamp(i), 0, 0)),
            pl.BlockSpec((1, 1, 6 * d), lambda i: (pclamp(i) // cpb, 0, 0)),
            pl.BlockSpec((gpc, 1, 6 * d), lambda i: (sclamp(i), 0, 0)),
            cst((1, d)), cst((N_EXPERTS, d)), cst((N_EXPERTS, 1)), cst((TC, TC)),
            cst((d, D_EXPERT)), cst((d, D_EXPERT)), cst((D_EXPERT, d)),
        ],
        out_specs=[
            pl.BlockSpec((TC, d), lambda i: (i, 0)),
            pl.BlockSpec((1, TOP_K, TC), lambda i: (i, 0, 0)),
            pl.BlockSpec((1, TOP_K, TC), lambda i: (i, 0, 0)),
            pl.BlockSpec((1, N_EXPERTS, LANE), lambda i: (i, 0, 0)),
            pl.BlockSpec((TC, d), lambda i: (i, 0)),
        ],
        out_shape=[
            jax.ShapeDtypeStruct((n_tok, d), BF16),
            jax.ShapeDtypeStruct((n_chunks, TOP_K, TC), jnp.int32),
            jax.ShapeDtypeStruct((n_chunks, TOP_K, TC), F32),
            jax.ShapeDtypeStruct((n_chunks, N_EXPERTS, LANE), jnp.int32),
            jax.ShapeDtypeStruct((n_tok, d), F32),
        ],
        compiler_params=arb1,
        name="moe_route",
    )(xp, xs, modp, mods_g, g_ffn, w_r_t, rb, tri, wsg, wsu, wsd)
    counts = cnt[:, :, 0]
    pos = pos.reshape(n_chunks, 1, TOP_K * TC)
    wts = wts.reshape(n_chunks, 1, TOP_K * TC)

    te, nt, ebase, cbase, etp, slot0, slot1, sel = pl.pallas_call(
        functools.partial(_plan_kernel, n_chunks=n_chunks, nt_max=nt_max),
        in_specs=[_smem_spec()],
        out_specs=[_smem_spec()] * 8,
        out_shape=[
            jax.ShapeDtypeStruct((nt_max,), jnp.int32),
            jax.ShapeDtypeStruct((1,), jnp.int32),
            jax.ShapeDtypeStruct((N_EXPERTS,), jnp.int32),
            jax.ShapeDtypeStruct((n_chunks * N_EXPERTS,), jnp.int32),
            jax.ShapeDtypeStruct((N_EXPERTS,), jnp.int32),
            jax.ShapeDtypeStruct((nt_max // 2,), jnp.int32),
            jax.ShapeDtypeStruct((nt_max // 2,), jnp.int32),
            jax.ShapeDtypeStruct((nt_max,), jnp.int32),
        ],
        name="moe_plan",
    )(counts)

    xsort = pl.pallas_call(
        functools.partial(_dispatch_kernel, n_chunks=n_chunks, nt_max=nt_max),
        grid=(n_chunks,),
        in_specs=[_smem_spec(), _smem_spec(), _smem_spec(), _smem_spec(), _smem_spec(),
                  pl.BlockSpec((1, 1, TOP_K * TC), lambda i: (i, 0, 0), memory_space=pltpu.SMEM),
                  pl.BlockSpec((TC, d), lambda i: (i, 0))],
        out_specs=pl.BlockSpec(memory_space=pl.ANY),
        out_shape=jax.ShapeDtypeStruct((sorted_rows, d), BF16),
        scratch_shapes=[pltpu.VMEM((PLANES * HP_STRIDE, LANE), F32),
                        pltpu.VMEM((PLANES * LOC_STRIDE, LANE), F32),
                        pltpu.VMEM((LOCAL_ROWS, d), BF16),
                        pltpu.VMEM((PAGE, d), BF16),
                        pltpu.VMEM((TILE_M, d), BF16),
                        pltpu.SMEM((2,), jnp.int32),
                        pltpu.SemaphoreType.DMA, pltpu.SemaphoreType.DMA, pltpu.SemaphoreType.DMA],
        compiler_params=arb1,
        name="moe_dispatch",
    )(counts, ebase, cbase, etp, nt, pos, h2)

    tile_map = lambda i, s0, s1, sl, nt_ref: (jnp.minimum(i, (nt_ref[0] - 1) // 2), 0)
    slot_maps = (lambda i, s0, s1, sl, nt_ref: (s0[i], 0, 0), lambda i, s0, s1, sl, nt_ref: (s1[i], 0, 0))
    w_specs = [pl.BlockSpec(shape, slot_maps[slot]) for slot in range(2)
               for shape in ((1, d, D_EXPERT), (1, d, D_EXPERT), (1, D_EXPERT, d))]
    eo = pl.pallas_call(
        _expert_kernel,
        grid_spec=pltpu.PrefetchScalarGridSpec(
            num_scalar_prefetch=4,
            grid=(nt_max // 2,),
            in_specs=[pl.BlockSpec((2 * TILE_M, d), tile_map)] + w_specs,
            out_specs=pl.BlockSpec((2 * TILE_M, d), tile_map),
            scratch_shapes=[pltpu.VMEM((2, d, D_EXPERT), BF16), pltpu.VMEM((2, d, D_EXPERT), BF16),
                            pltpu.VMEM((2, D_EXPERT, d), BF16)],
        ),
        out_shape=jax.ShapeDtypeStruct((sorted_rows, d), BF16),
        input_output_aliases={4: 0},
        compiler_params=arb1,
        name="moe_experts",
    )(slot0, slot1, sel, nt, xsort, weg, weu, wed, weg, weu, wed)

    nph = COMBINE_PHASES
    pstep = lambda i, j: jnp.minimum(i * nph + j, npc * nph - 1)
    sstep = lambda i, j: jnp.maximum((i - npc) * nph + j, 0)
    smem_chunk = pl.BlockSpec((1, 1, TOP_K * TC), lambda i, j: (i, 0, 0), memory_space=pltpu.SMEM)
    yp, ys = pl.pallas_call(
        functools.partial(_combine_kernel, n_prompt_chunks=npc, n_chunks=n_chunks),
        grid=(n_chunks, nph),
        in_specs=[_smem_spec(), _smem_spec(), _smem_spec(), smem_chunk, smem_chunk,
                  pl.BlockSpec((gpp, r, d), lambda i, j: (pstep(i, j), 0, 0)),
                  pl.BlockSpec((gpp, r, d), lambda i, j: (sstep(i, j), 0, 0)),
                  pl.BlockSpec((1, 1, 6 * d), lambda i, j: (pclamp(i) // cpb, 0, 0)),
                  pl.BlockSpec((gpp, 1, 6 * d), lambda i, j: (sstep(i, j), 0, 0)),
                  pl.BlockSpec((PHASE_ROWS, d), lambda i, j: (i * nph + j, 0)),
                  cst((1, d)),
                  pl.BlockSpec(memory_space=pl.ANY)],
        out_specs=[pl.BlockSpec((gpp, r, d), lambda i, j: (pstep(i, j), 0, 0)),
                   pl.BlockSpec((gpp, r, d), lambda i, j: (sstep(i, j), 0, 0))],
        out_shape=[jax.ShapeDtypeStruct(xp.shape, F32), jax.ShapeDtypeStruct(xs.shape, F32)],
        scratch_shapes=[pltpu.VMEM((PLANES * LOC_STRIDE, LANE), F32),
                        pltpu.VMEM((2, LOCAL_ROWS, d), BF16),
                        pltpu.VMEM((PLANES * Y_STRIDE, LANE), F32),
                        pltpu.SMEM((2,), jnp.int32),
                        pltpu.SemaphoreType.DMA, pltpu.SemaphoreType.DMA],
        compiler_params=arb2,
        name="moe_combine",
    )(counts, ebase, cbase, pos, wts, xp, xs, modp, mods_g,
      shared, g_final, eo)
    return yp.reshape(b, s_len, d), ys.reshape(db, ds, d)


def _t5_bucket(rel):
    nb = NUM_BUCKETS // 2
    max_exact = nb // 2
    ret = jnp.where(rel > 0, nb, 0)
    n = jnp.abs(rel)
    nf = jnp.maximum(n, 1).astype(F32)
    large = max_exact + (jnp.log(nf / max_exact) / math.log(REL_MAX_DIST / max_exact)
                         * (nb - max_exact)).astype(jnp.int32)
    large = jnp.minimum(large, nb - 1)
    return ret + jnp.where(n < max_exact, n, large)


REL_SPAN = 2 * LANE


def _rel_bias_kernel(tab_ref, o_ref):
    tab = tab_ref[...]
    for i in range(CHUNK):
        shift = CHUNK - 1 - i
        o_ref[i] = pltpu.roll(tab, (REL_SPAN - shift) % REL_SPAN, 1)[:, 0:BAND]


def _rel_bias(rel_table):
    rel = jnp.arange(REL_SPAN) - (WINDOW + CHUNK - 1)
    tab = rel_table[_t5_bucket(rel)].astype(F32).T
    out = pl.pallas_call(
        _rel_bias_kernel,
        in_specs=[pl.BlockSpec((N_HEADS, REL_SPAN), lambda: (0, 0))],
        out_specs=pl.BlockSpec((CHUNK, N_HEADS, BAND), lambda: (0, 0, 0)),
        out_shape=jax.ShapeDtypeStruct((CHUNK, N_HEADS, BAND), F32),
        name="rel_bias",
    )(tab)
    return jnp.transpose(out, (1, 0, 2))


def kernel(x_prompt, x_sample, c_prompt, c_sample, cache_k, cache_v, state_pool, w_ada, b_ada, g_norm_mix, g_norm_ffn, w_in, w_pool_group, pool_scale, attn_sinks, w_pool_proj, w_attn_proj, w_out, rel_table, w_router, router_bias, w_exp_gate, w_exp_up, w_exp_down, w_sh_gate, w_sh_up, w_sh_down, g_final):
    b, s_len, d = x_prompt.shape
    db, ds, _ = x_sample.shape
    l = 0

    c_all = jnp.concatenate([c_prompt, c_sample, jnp.zeros((4, d), F32)], axis=0)
    mods = _ada(c_all, w_ada[l], b_ada[l])
    mod_p = mods[0:b].reshape(b, 1, 6 * d)
    mod_s = mods[b:b + db].reshape(db, 1, 6 * d)

    bias = _rel_bias(rel_table)
    pad_keys = PAIR_KEYS - BAND
    bt = jnp.transpose(bias.reshape(N_KV_HEADS, 2, 2, CHUNK, BAND), (0, 2, 1, 3, 4))
    lead = ((0, 0),) * 4
    bias4 = jnp.pad(bt.reshape(N_KV_HEADS, 2, 2 * CHUNK, BAND), lead[:3] + ((0, pad_keys),), constant_values=-1e30)
    bias_pw = jnp.stack([jnp.pad(bt, lead + ((0, pad_keys),), constant_values=-1e30),
                         jnp.pad(bt, lead + ((pad_keys, 0),), constant_values=-1e30)], axis=3)
    bias_pw = jnp.swapaxes(bias_pw.reshape(N_KV_HEADS, 2, 4 * CHUNK, PAIR_KEYS), 2, 3)
    skt = jnp.transpose(attn_sinks[l].astype(F32).reshape(N_KV_HEADS, 2, 2), (0, 2, 1))
    sink4 = jnp.broadcast_to(skt[:, :, :, None, None], (N_KV_HEADS, 2, 2, CHUNK, LANE)).reshape(
        N_KV_HEADS, 2, 2 * CHUNK, LANE)
    sink_pw = jnp.repeat(skt.reshape(-1), 2 * CHUNK).reshape(1, -1)
    blk = jnp.arange(POOL_WIDTH) // POOL_GW
    w_pg = jnp.where(blk[:, None] == blk[None, :],
                     jnp.tile(w_pool_group[l].reshape(POOL_WIDTH, POOL_GW), (1, len(POOL_WINDOWS))), 0.0).astype(BF16)
    w_in_b = w_in[l].astype(BF16)
    w_pp = w_pool_proj[l].astype(BF16)
    w_ap = w_attn_proj[l].astype(BF16)
    w_out_b = w_out[l].astype(BF16)
    g_mix = g_norm_mix[l].reshape(1, d)
    g_ffn = g_norm_ffn[l].reshape(1, d)
    ps = pool_scale[l].reshape(1, POOL_WIDTH)

    x1p, nk_p, nv_p, np_p = _mixer_prompt(x_prompt, mod_p, g_mix, w_in_b, w_pg, ps, bias_pw, sink_pw,
                                          w_pp, w_ap, w_out_b)
    ck = cache_k[l].reshape(db, WINDOW, KV_WIDTH)
    cv = cache_v[l].reshape(db, WINDOW, KV_WIDTH)
    sp16 = jnp.pad(state_pool[l], ((0, 0), (HIST - POOL_PAD, 0), (0, 0)))
    x1s, nk_s, nv_s, np_s = _mixer_sample(x_sample, mod_s, ck, cv, sp16, g_mix, w_in_b, w_pg, ps,
                                          bias4, sink4, w_pp, w_ap, w_out_b)

    y_p, y_s = _moe_sparse(x1p, x1s, mod_p, mod_s, g_ffn, g_final.reshape(1, d),
                           w_router[l].T.astype(BF16), router_bias[l].astype(F32).reshape(N_EXPERTS, 1),
                           w_sh_gate[l].astype(BF16), w_sh_up[l].astype(BF16), w_sh_down[l].astype(BF16),
                           w_exp_gate[l], w_exp_up[l], w_exp_down[l])

    kv_shape = lambda n: (1, n, WINDOW, N_KV_HEADS, HEAD_DIM)
    return (y_p, y_s,
            nk_p.reshape(kv_shape(b)), nv_p.reshape(kv_shape(b)),
            np_p[:, HIST - POOL_PAD:, :][None],
            nk_s.reshape(kv_shape(db)), nv_s.reshape(kv_shape(db)),
            np_s[:, HIST - POOL_PAD:, :][None])
```
